```python
import math
import jax, jax.numpy as jnp
from jax import lax
import numpy as np

D_MODEL = 2048
BATCH = 4
SEQ = 4096
DEPTH = 2

GRID_W = 64
CTX_LEN = 256
MIX_WIDTH = D_MODEL
CONV_WIDTH = MIX_WIDTH // 2
FOURIER_WIDTH = MIX_WIDTH - CONV_WIDTH
FOURIER_GROUPS = 4
FOURIER_GROUP_DIM = FOURIER_WIDTH // FOURIER_GROUPS
CONV_K = 31
HEAD_DIM = 128
NA_WIDTH = (3 * MIX_WIDTH) // 4
NA_HEADS = NA_WIDTH // HEAD_DIM
SSM_WIDTH = MIX_WIDTH - NA_WIDTH
SSM_GROUP = 16
SSM_GROUPS = SSM_WIDTH // SSM_GROUP
SSM_STATE = 64
NA_ROWS = 8
NA_COLS = 16
N_EVEN = (DEPTH + 1) // 2
N_ODD = DEPTH // 2
EVEN_IN = 3 * CONV_WIDTH + 2 * FOURIER_WIDTH
ODD_IN = 4 * NA_WIDTH + 2 * SSM_WIDTH
EVEN_SPLITS = (CONV_WIDTH, 2 * CONV_WIDTH, 3 * CONV_WIDTH, 3 * CONV_WIDTH + FOURIER_WIDTH)
ODD_SPLITS = (NA_WIDTH, 2 * NA_WIDTH, 3 * NA_WIDTH, 4 * NA_WIDTH, 4 * NA_WIDTH + SSM_WIDTH)
EPS = 1e-6
NEG_INF = -1e30

kernel_name = "hybrid_conv_fourier_natten_s5_dit_block"

F32 = jnp.float32


def rms_norm(x, g):
    xf = x.astype(F32)
    y = xf * lax.rsqrt(jnp.mean(xf * xf, axis=-1, keepdims=True) + EPS)
    return (y * g.astype(F32)).astype(x.dtype)


def layer_norm(x, g, b):
    xf = x.astype(F32)
    mu = jnp.mean(xf, axis=-1, keepdims=True)
    var = jnp.mean(jnp.square(xf - mu), axis=-1, keepdims=True)
    y = (xf - mu) * lax.rsqrt(var + EPS) * g.astype(F32) + b.astype(F32)
    return y.astype(x.dtype)


def modulate(h, shift, scale):
    return h * (1.0 + scale[:, None]) + shift[:, None]


def conv_fourier_mixer(u, w_in, w_out, conv_w, conv_b, ln_g, ln_b, fourier_g):
    bsz, length, _ = u.shape
    p = u @ w_in
    a_val, a_glu, a_gate, b_in, b_gate = jnp.split(p, EVEN_SPLITS, axis=-1)
    a = a_val * jax.nn.sigmoid(a_glu)
    a = lax.conv_general_dilated(
        a, conv_w[:, None, :], window_strides=(1,),
        padding=[(CONV_K // 2, CONV_K // 2)],
        dimension_numbers=("NWC", "WIO", "NWC"),
        feature_group_count=CONV_WIDTH) + conv_b
    a = jax.nn.silu(layer_norm(a, ln_g, ln_b)) * jax.nn.silu(a_gate)
    bn = rms_norm(b_in.reshape(bsz, length, FOURIER_GROUPS, FOURIER_GROUP_DIM), fourier_g)
    f = jnp.fft.fft2(bn.astype(F32), axes=(1, 3), norm="ortho").real.astype(u.dtype)
    b = f.reshape(bsz, length, FOURIER_WIDTH) * jax.nn.silu(b_gate)
    return jnp.concatenate([a, b], axis=-1) @ w_out


def neighbourhood_attention(q, k, v, k_c, v_c, rpb):
    bsz, length = q.shape[:2]
    rows = length // GRID_W
    kr = min(NA_ROWS, rows)
    qg = q.reshape(bsz, rows, GRID_W, NA_HEADS, HEAD_DIM)
    kg = k.reshape(bsz, rows, GRID_W, NA_HEADS, HEAD_DIM)
    vg = v.reshape(bsz, rows, GRID_W, NA_HEADS, HEAD_DIM)
    key_col = jnp.tile(jnp.arange(GRID_W), kr)
    key_row_off = jnp.repeat(jnp.arange(kr), GRID_W)
    q_col = jnp.arange(GRID_W)
    c_start = jnp.clip(q_col - NA_COLS // 2, 0, GRID_W - NA_COLS)
    col_mask = (key_col[None] >= c_start[:, None]) & (key_col[None] < c_start[:, None] + NA_COLS)
    dc = jnp.clip(key_col[None] - q_col[:, None] + NA_COLS - 1, 0, 2 * NA_COLS - 2)
    scale = HEAD_DIM ** -0.5
    n_loc = kr * GRID_W

    def row_block(r):
        r_start = jnp.clip(r - kr // 2, 0, rows - kr)
        q_r = lax.dynamic_index_in_dim(qg, r, axis=1, keepdims=False)
        k_blk = lax.dynamic_slice_in_dim(kg, r_start, kr, axis=1).reshape(bsz, n_loc, NA_HEADS, HEAD_DIM)
        v_blk = lax.dynamic_slice_in_dim(vg, r_start, kr, axis=1).reshape(bsz, n_loc, NA_HEADS, HEAD_DIM)
        dr = jnp.clip(r_start + key_row_off - r + NA_ROWS - 1, 0, 2 * NA_ROWS - 2)
        bias = rpb[:, dr[None, :], dc].astype(F32)
        s_loc = jnp.einsum("bqhd,bkhd->bhqk", q_r, k_blk).astype(F32) * scale + bias
        s_loc = jnp.where(col_mask, s_loc, NEG_INF)
        s_ctx = jnp.einsum("bqhd,bkhd->bhqk", q_r, k_c).astype(F32) * scale
        p = jax.nn.softmax(jnp.concatenate([s_loc, s_ctx], axis=-1), axis=-1).astype(v.dtype)
        return (jnp.einsum("bhqk,bkhd->bqhd", p[..., :n_loc], v_blk)
                + jnp.einsum("bhqk,bkhd->bqhd", p[..., n_loc:], v_c))

    out = lax.map(row_block, jnp.arange(rows))
    return jnp.moveaxis(out, 0, 1).reshape(bsz, length, NA_WIDTH)


def context_attention(q_c, k_c, v_c):
    bsz, lc = q_c.shape[:2]
    s = jnp.einsum("bqhd,bkhd->bhqk", q_c, k_c).astype(F32) * (HEAD_DIM ** -0.5)
    p = jax.nn.softmax(s, axis=-1).astype(v_c.dtype)
    return jnp.einsum("bhqk,bkhd->bqhd", p, v_c).reshape(bsz, lc, NA_WIDTH)


def diag_scan(lam_bar, bu):
    a = jnp.broadcast_to(lam_bar, bu.shape)

    def combine(e1, e2):
        a1, b1 = e1
        a2, b2 = e2
        return a1 * a2, a2 * b1 + b2

    _, h = lax.associative_scan(combine, (a, bu), axis=1)
    return h


def s5_scan_direction(u_c, u_x, a_re, a_im, log_dt, b_re, b_im):
    lam = lax.complex(a_re.astype(F32), a_im.astype(F32))
    dt = jnp.exp(log_dt.astype(F32))[:, None]
    lam_dt = lam * dt
    lam_bar = jnp.exp(lam_dt)
    b_bar = ((lam_bar - 1.0) / lam)[..., None] * lax.complex(b_re.astype(F32), b_im.astype(F32))
    bu_c = jnp.einsum("blgh,gph->blgp", u_c.astype(F32), b_bar)
    bu_x = jnp.einsum("blgh,gph->blgp", u_x.astype(F32), b_bar)
    h_c = diag_scan(lam_bar, bu_c)
    length = u_x.shape[1]
    steps = jnp.arange(1, length + 1, dtype=F32)[:, None, None]
    lam_pow = jnp.exp(lam_dt[None] * steps)
    h_x = diag_scan(lam_bar, bu_x) + lam_pow[None] * h_c[:, -1][:, None]
    return h_c, h_x


def s5_readout(h_f, h_b, u, c_re, c_im, d_skip, w_glu):
    bsz, length = u.shape[:2]
    c_f = lax.complex(c_re[0].astype(F32), c_im[0].astype(F32))
    c_b = lax.complex(c_re[1].astype(F32), c_im[1].astype(F32))
    y = (jnp.einsum("blgp,ghp->blgh", h_f, c_f).real
         + jnp.einsum("blgp,ghp->blgh", h_b, c_b).real
         + d_skip.reshape(SSM_GROUPS, SSM_GROUP).astype(F32) * u.astype(F32))
    y = jax.nn.gelu(y.reshape(bsz, length, SSM_WIDTH)).astype(u.dtype)
    return y * jax.nn.sigmoid(y @ w_glu)


def na_ssm_mixer(u_c, u_x, w_in, w_out, rpb, a_re, a_im, log_dt, b_re, b_im, c_re, c_im,
                 d_skip, w_glu, need_ctx):
    bsz, length, _ = u_x.shape
    lc = u_c.shape[1]
    q_x, k_x, v_x, g_x, d_x, dg_x = jnp.split(u_x @ w_in, ODD_SPLITS, axis=-1)
    q_c, k_c, v_c, g_c, d_c, dg_c = jnp.split(u_c @ w_in, ODD_SPLITS, axis=-1)
    heads = lambda t, n: t.reshape(bsz, n, NA_HEADS, HEAD_DIM)
    k_c4, v_c4 = heads(k_c, lc), heads(v_c, lc)
    na_x = neighbourhood_attention(heads(q_x, length), heads(k_x, length), heads(v_x, length),
                                   k_c4, v_c4, rpb)
    u_dc = d_c.reshape(bsz, lc, SSM_GROUPS, SSM_GROUP)
    u_dx = d_x.reshape(bsz, length, SSM_GROUPS, SSM_GROUP)
    hcf, hxf = s5_scan_direction(u_dc, u_dx, a_re[0], a_im[0], log_dt[0], b_re[0], b_im[0])
    hcb, hxb = s5_scan_direction(u_dc[:, ::-1], u_dx[:, ::-1], a_re[1], a_im[1], log_dt[1],
                                 b_re[1], b_im[1])
    hcb, hxb = hcb[:, ::-1], hxb[:, ::-1]
    ssm_x = s5_readout(hxf, hxb, u_dx, c_re, c_im, d_skip, w_glu)
    out_x = jnp.concatenate([na_x * jax.nn.silu(g_x), ssm_x * jax.nn.silu(dg_x)], axis=-1) @ w_out
    out_c = None
    if need_ctx:
        na_c = context_attention(heads(q_c, lc), k_c4, v_c4)
        ssm_c = s5_readout(hcf, hcb, u_dc, c_re, c_im, d_skip, w_glu)
        out_c = jnp.concatenate([na_c * jax.nn.silu(g_c), ssm_c * jax.nn.silu(dg_c)], axis=-1) @ w_out
    return out_c, out_x


def setup_inputs(seed: int = 0) -> dict:
    key = jax.random.key(seed)
    ks = jax.random.split(key, 32)
    nrm = lambda k, shape, s: jax.random.normal(k, shape, F32) * s
    G, P, H = SSM_GROUPS, SSM_STATE, SSM_GROUP
    a_im_base = math.pi * jnp.arange(P, dtype=F32)
    return {
        "x": nrm(ks[0], (BATCH, SEQ, D_MODEL), 1.0),
        "c": nrm(ks[1], (BATCH, D_MODEL), 1.0),
        "ctx": nrm(ks[2], (BATCH, CTX_LEN, D_MODEL), 1.0),
        "c_ctx": nrm(ks[3], (D_MODEL,), 1.0),
        "pre_g": 1.0 + nrm(ks[4], (DEPTH, D_MODEL), 0.05),
        "post_g": 1.0 + nrm(ks[5], (DEPTH, D_MODEL), 0.05),
        "ada_w": nrm(ks[6], (DEPTH, D_MODEL, 3 * D_MODEL), D_MODEL ** -0.5),
        "ada_b": nrm(ks[7], (DEPTH, 3 * D_MODEL), 0.02),
        "ab_w_in": nrm(ks[8], (N_EVEN, D_MODEL, EVEN_IN), D_MODEL ** -0.5),
        "ab_w_out": nrm(ks[9], (N_EVEN, MIX_WIDTH, D_MODEL), MIX_WIDTH ** -0.5),
        "conv_w": nrm(ks[10], (N_EVEN, CONV_K, CONV_WIDTH), CONV_K ** -0.5),
        "conv_b": nrm(ks[11], (N_EVEN, CONV_WIDTH), 0.02),
        "conv_ln_g": 1.0 + nrm(ks[12], (N_EVEN, CONV_WIDTH), 0.05),
        "conv_ln_b": nrm(ks[13], (N_EVEN, CONV_WIDTH), 0.02),
        "fourier_g": 1.0 + nrm(ks[14], (N_EVEN, FOURIER_GROUPS, FOURIER_GROUP_DIM), 0.05),
        "cd_w_in": nrm(ks[15], (N_ODD, D_MODEL, ODD_IN), D_MODEL ** -0.5),
        "cd_w_out": nrm(ks[16], (N_ODD, MIX_WIDTH, D_MODEL), MIX_WIDTH ** -0.5),
        "na_rpb": nrm(ks[17], (N_ODD, NA_HEADS, 2 * NA_ROWS - 1, 2 * NA_COLS - 1), 0.1),
        "s5_a_re": -0.5 + nrm(ks[18], (N_ODD, 2, G, P), 0.01),
        "s5_a_im": a_im_base + nrm(ks[19], (N_ODD, 2, G, P), 0.01),
        "s5_log_dt": jax.random.uniform(ks[20], (N_ODD, 2, G), F32, math.log(1e-3), math.log(1e-1)),
        "s5_b_re": nrm(ks[21], (N_ODD, 2, G, P, H), (2 * H) ** -0.5),
        "s5_b_im": nrm(ks[22], (N_ODD, 2, G, P, H), (2 * H) ** -0.5),
        "s5_c_re": nrm(ks[23], (N_ODD, 2, G, H, P), P ** -0.5),
        "s5_c_im": nrm(ks[24], (N_ODD, 2, G, H, P), P ** -0.5),
        "s5_d": nrm(ks[25], (N_ODD, SSM_WIDTH), 1.0),
        "s5_w_glu": nrm(ks[26], (N_ODD, SSM_WIDTH, SSM_WIDTH), SSM_WIDTH ** -0.5),
    }


def reference(x, c, ctx, c_ctx, pre_g, post_g, ada_w, ada_b, ab_w_in, ab_w_out, conv_w, conv_b,
              conv_ln_g, conv_ln_b, fourier_g, cd_w_in, cd_w_out, na_rpb, s5_a_re, s5_a_im,
              s5_log_dt, s5_b_re, s5_b_im, s5_c_re, s5_c_im, s5_d, s5_w_glu):
    h_x, h_c = x, ctx
    cond_x = jax.nn.silu(c)
    cond_c = jax.nn.silu(c_ctx)[None]
    for i in range(DEPTH):
        need_ctx = i < DEPTH - 1
        sh_x, sc_x, gt_x = jnp.split(cond_x @ ada_w[i] + ada_b[i], 3, axis=-1)
        sh_c, sc_c, gt_c = jnp.split(cond_c @ ada_w[i] + ada_b[i], 3, axis=-1)
        u_x = modulate(rms_norm(h_x, pre_g[i]), sh_x, sc_x)
        u_c = modulate(rms_norm(h_c, pre_g[i]), sh_c, sc_c)
        j = i // 2
        if i % 2 == 0:
            params = (ab_w_in[j], ab_w_out[j], conv_w[j], conv_b[j], conv_ln_g[j], conv_ln_b[j],
                      fourier_g[j])
            out_x = conv_fourier_mixer(u_x, *params)
            out_c = conv_fourier_mixer(u_c, *params) if need_ctx else None
        else:
            out_c, out_x = na_ssm_mixer(u_c, u_x, cd_w_in[j], cd_w_out[j], na_rpb[j],
                                        s5_a_re[j], s5_a_im[j], s5_log_dt[j], s5_b_re[j],
                                        s5_b_im[j], s5_c_re[j], s5_c_im[j], s5_d[j],
                                        s5_w_glu[j], need_ctx)
        h_x = h_x + gt_x[:, None] * rms_norm(out_x, post_g[i])
        if need_ctx:
            h_c = h_c + gt_c[:, None] * rms_norm(out_c, post_g[i])
    return h_x
```

```python
import functools
import math

import jax
import jax.numpy as jnp
from jax import lax
from jax.experimental import pallas as pl
from jax.experimental.pallas import tpu as pltpu

F32 = jnp.float32
BF16 = jnp.bfloat16

EPS = 1e-6
NEG_INF = -1e30

GRID_W = 64
CONV_K = 31
FOURIER_GROUPS = 4
HEAD_DIM = 128
NA_ROWS = 8
NA_COLS = 16
SSM_GROUP = 16
SSM_STATE = 64

LANES = 128
SUBLANES = 8
VMEM_LIMIT = 56 * 1024 * 1024

NA_PAIR = 2 * GRID_W
NA_WIN_ROWS = NA_ROWS + 2
NA_WIN = NA_WIN_ROWS * GRID_W
NA_VARIANTS = 5


def _params(sem, vmem=VMEM_LIMIT):
    return pltpu.CompilerParams(dimension_semantics=sem, vmem_limit_bytes=vmem)


def _silu(x):
    return x * jax.nn.sigmoid(x)


def _rms(x):
    return x * lax.rsqrt(jnp.mean(x * x, axis=-1, keepdims=True) + EPS)


def _dot(a, b):
    return jnp.dot(a, b, preferred_element_type=F32)


def _ada_kernel(c_ref, w_ref, b_ref, o_ref):
    cond = _silu(c_ref[...])
    o_ref[0] = _dot(cond.astype(BF16), w_ref[0].astype(BF16)) + b_ref[0]


def ada_modulation(cond_rows, ada_w, ada_b, tn=1024):
    depth, d, n = ada_w.shape
    rows = cond_rows.shape[0]
    return pl.pallas_call(
        _ada_kernel,
        grid=(depth, n // tn),
        in_specs=[
            pl.BlockSpec((rows, d), lambda i, j: (0, 0)),
            pl.BlockSpec((1, d, tn), lambda i, j: (i, 0, j)),
            pl.BlockSpec((1, 1, tn), lambda i, j: (i, 0, j)),
        ],
        out_specs=pl.BlockSpec((1, rows, tn), lambda i, j: (i, 0, j)),
        out_shape=jax.ShapeDtypeStruct((depth, rows, n), F32),
        compiler_params=_params(("parallel", "parallel")),
    )(cond_rows, ada_w, ada_b.reshape(depth, 1, n))


def _prenorm_modulate(h_ref, g_ref, sh_ref, sc_ref):
    u = _rms(h_ref[0]) * g_ref[...]
    return (u * (1.0 + sc_ref[0]) + sh_ref[0]).astype(BF16)


def _even_in_kernel(h_ref, g_ref, sh_ref, sc_ref, wv_ref, wg_ref, wt_ref, wb_ref, wbg_ref,
                    fg_ref, cs_ref, a_ref, sga_ref, p_ref, q_ref, sgb_ref, xn_ref):
    @pl.when(pl.program_id(2) == 0)
    def _():
        xn_ref[...] = _prenorm_modulate(h_ref, g_ref, sh_ref, sc_ref)

    xn = xn_ref[...]
    a = _dot(xn, wv_ref[...]) * jax.nn.sigmoid(_dot(xn, wg_ref[...]))
    a_ref[0] = a.astype(BF16)
    sga_ref[0] = _silu(_dot(xn, wt_ref[...])).astype(BF16)
    bn = _rms(_dot(xn, wb_ref[...])) * fg_ref[0]
    pq = _dot(bn.astype(BF16), cs_ref[...])
    gd = pq.shape[1] // 2
    p_ref[0] = pq[:, :gd].astype(BF16)
    q_ref[0] = pq[:, gd:].astype(BF16)
    sgb_ref[0] = _silu(_dot(xn, wbg_ref[...])).astype(BF16)


def even_in_proj(h, pre_g, shift, scale, w5, fourier_g, cs_small, tm):
    bsz, length, d = h.shape
    width = w5[0].shape[1]
    gd = width // FOURIER_GROUPS
    wspec = pl.BlockSpec((d, gd), lambda b, i, n: (0, n))
    ospec = pl.BlockSpec((1, tm, gd), lambda b, i, n: (b, i, n))
    vec = pl.BlockSpec((1, 1, d), lambda b, i, n: (b, 0, 0))
    out = jax.ShapeDtypeStruct((bsz, length, width), BF16)
    return pl.pallas_call(
        _even_in_kernel,
        grid=(bsz, length // tm, FOURIER_GROUPS),
        in_specs=[
            pl.BlockSpec((1, tm, d), lambda b, i, n: (b, i, 0)),
            pl.BlockSpec((1, d), lambda b, i, n: (0, 0)),
            vec, vec,
            wspec, wspec, wspec, wspec, wspec,
            pl.BlockSpec((1, 1, gd), lambda b, i, n: (n, 0, 0)),
            pl.BlockSpec((gd, 2 * gd), lambda b, i, n: (0, 0)),
        ],
        out_specs=[ospec] * 5,
        out_shape=[out] * 5,
        scratch_shapes=[pltpu.VMEM((tm, d), BF16)],
        compiler_params=_params(("parallel", "parallel", "arbitrary")),
    )(h, pre_g.reshape(1, d), shift, scale, *w5, fourier_g.reshape(FOURIER_GROUPS, 1, gd), cs_small)


CONV_HALO = 16
CONV_ROWS = 32


def _conv_kernel(a_ref, prev_ref, next_ref, w_ref, cb_ref, lg_ref, lb_ref, sga_ref, o_ref,
                 ext_ref, sh_ref, acc_ref):
    i = pl.program_id(1)
    last = pl.num_programs(1) - 1
    tl, width = acc_ref.shape
    ext_ref[pl.ds(CONV_HALO, tl), :] = a_ref[0].astype(F32)
    ext_ref[pl.ds(0, CONV_HALO), :] = jnp.where(i > 0, prev_ref[0].astype(F32), 0.0)
    ext_ref[pl.ds(CONV_HALO + tl, CONV_HALO), :] = jnp.where(i < last, next_ref[0].astype(F32), 0.0)
    base = CONV_HALO - CONV_K // 2
    span = sh_ref.shape[1]
    for r in range(1, SUBLANES):
        sh_ref[r - 1] = ext_ref[pl.ds(r, span), :]

    def rows(rb, carry):
        r0 = pl.multiple_of(rb * CONV_ROWS, CONV_ROWS)
        for cb in range(width // LANES):
            cols = pl.ds(cb * LANES, LANES)
            acc = jnp.zeros((CONV_ROWS, LANES), F32)
            for k in range(CONV_K):
                q, r = divmod(base + k, SUBLANES)
                rsel = pl.ds(r0 + q * SUBLANES, CONV_ROWS)
                tap = ext_ref[rsel, cols] if r == 0 else sh_ref[r - 1, rsel, cols]
                acc = acc + tap * w_ref[pl.ds(k, 1), cols]
            acc_ref[pl.ds(r0, CONV_ROWS), cols] = acc + cb_ref[:, cols]
        return carry

    lax.fori_loop(0, tl // CONV_ROWS, rows, 0)
    x = acc_ref[...]
    mu = jnp.mean(x, axis=-1, keepdims=True)
    xc = x - mu
    var = jnp.mean(xc * xc, axis=-1, keepdims=True)
    y = xc * lax.rsqrt(var + EPS) * lg_ref[...] + lb_ref[...]
    o_ref[0] = (_silu(y) * sga_ref[0].astype(F32)).astype(BF16)


def conv_branch(a, sga, conv_w, conv_b, ln_g, ln_b, tl):
    bsz, length, width = a.shape
    hb = tl // CONV_HALO
    nh = length // CONV_HALO
    row = pl.BlockSpec((1, width), lambda b, i: (0, 0))
    main = pl.BlockSpec((1, tl, width), lambda b, i: (b, i, 0))
    return pl.pallas_call(
        _conv_kernel,
        grid=(bsz, length // tl),
        in_specs=[
            main,
            pl.BlockSpec((1, CONV_HALO, width), lambda b, i: (b, jnp.maximum(i * hb - 1, 0), 0)),
            pl.BlockSpec((1, CONV_HALO, width), lambda b, i: (b, jnp.minimum((i + 1) * hb, nh - 1), 0)),
            pl.BlockSpec((CONV_K, width), lambda b, i: (0, 0)),
            row, row, row,
            main,
        ],
        out_specs=main,
        out_shape=jax.ShapeDtypeStruct((bsz, length, width), BF16),
        scratch_shapes=[pltpu.VMEM((tl + 2 * CONV_HALO, width), F32),
                        pltpu.VMEM((SUBLANES - 1, tl + 2 * CONV_HALO - SUBLANES, width), F32),
                        pltpu.VMEM((tl, width), F32)],
        compiler_params=_params(("parallel", "parallel")),
    )(a, a, a, conv_w, conv_b.reshape(1, width), ln_g.reshape(1, width), ln_b.reshape(1, width), sga)


DFT_SPLIT = 64


def _dft_gen_kernel(ca_ref, sa_ref, cb_ref, sb_ref, c_ref, s_ref):
    ca, sa = ca_ref[0], sa_ref[0]
    cb, sb = cb_ref[...], sb_ref[...]
    c_ref[...] = (ca * cb - sa * sb).astype(BF16)
    s_ref[...] = (-(sa * cb + ca * sb)).astype(BF16)


def dft_matrices(length):
    na = length // DFT_SPLIT
    k = jnp.arange(length, dtype=jnp.int32)[None, :]
    ia = jnp.arange(na, dtype=jnp.int32)[:, None]
    ib = jnp.arange(DFT_SPLIT, dtype=jnp.int32)[:, None]
    w = 2.0 * math.pi / length
    ang_a = ((DFT_SPLIT * ia * k) % length).astype(F32) * w
    ang_b = ((ib * k) % length).astype(F32) * w
    ca, sa = jnp.cos(ang_a).reshape(na, 1, length), jnp.sin(ang_a).reshape(na, 1, length)
    cb, sb = jnp.cos(ang_b), jnp.sin(ang_b)
    tab_a = pl.BlockSpec((1, 1, length), lambda a: (a, 0, 0))
    tab_b = pl.BlockSpec((DFT_SPLIT, length), lambda a: (0, 0))
    out = pl.BlockSpec((DFT_SPLIT, length), lambda a: (a, 0))
    shp = jax.ShapeDtypeStruct((length, length), BF16)
    return pl.pallas_call(
        _dft_gen_kernel,
        grid=(na,),
        in_specs=[tab_a, tab_a, tab_b, tab_b],
        out_specs=[out, out],
        out_shape=[shp, shp],
        compiler_params=_params(("parallel",)),
    )(ca, sa, cb, sb)


def _dft_kernel(c_ref, s_ref, p_ref, q_ref, sgb_ref, o_ref, acc_ref, *, scale):
    kk = pl.program_id(2)

    @pl.when(kk == 0)
    def _():
        acc_ref[...] = jnp.zeros_like(acc_ref)

    acc_ref[...] += _dot(c_ref[...], p_ref[0]) + _dot(s_ref[...], q_ref[0])

    @pl.when(kk == pl.num_programs(2) - 1)
    def _():
        o_ref[0] = (acc_ref[...] * scale * sgb_ref[0].astype(F32)).astype(BF16)


def fourier_branch(cmat, smat, p, q, sgb, scale, tm, tk):
    bsz, length, width = p.shape
    return pl.pallas_call(
        functools.partial(_dft_kernel, scale=scale),
        grid=(bsz, length // tm, length // tk),
        in_specs=[
            pl.BlockSpec((tm, tk), lambda b, m, k: (m, k)),
            pl.BlockSpec((tm, tk), lambda b, m, k: (m, k)),
            pl.BlockSpec((1, tk, width), lambda b, m, k: (b, k, 0)),
            pl.BlockSpec((1, tk, width), lambda b, m, k: (b, k, 0)),
            pl.BlockSpec((1, tm, width), lambda b, m, k: (b, m, 0)),
        ],
        out_specs=pl.BlockSpec((1, tm, width), lambda b, m, k: (b, m, 0)),
        out_shape=jax.ShapeDtypeStruct((bsz, length, width), BF16),
        scratch_shapes=[pltpu.VMEM((tm, width), F32)],
        compiler_params=_params(("parallel", "parallel", "arbitrary")),
    )(cmat, smat, p, q, sgb)


def _postnorm_residual(y, h_ref, pg_ref, gt_ref, o_ref):
    o_ref[0] = h_ref[0] + gt_ref[0] * (_rms(y) * pg_ref[...])


def _even_out_kernel(a_ref, b_ref, wa_ref, wb_ref, h_ref, pg_ref, gt_ref, o_ref):
    y = _dot(a_ref[0], wa_ref[...]) + _dot(b_ref[0], wb_ref[...])
    _postnorm_residual(y, h_ref, pg_ref, gt_ref, o_ref)


def even_out_proj(a, b, wa, wb, h, post_g, gate, tm):
    bsz, length, d = h.shape
    width = a.shape[2]
    half = pl.BlockSpec((1, tm, width), lambda bb, i: (bb, i, 0))
    wspec = pl.BlockSpec((width, d), lambda bb, i: (0, 0))
    full = pl.BlockSpec((1, tm, d), lambda bb, i: (bb, i, 0))
    return pl.pallas_call(
        _even_out_kernel,
        grid=(bsz, length // tm),
        in_specs=[half, half, wspec, wspec, full,
                  pl.BlockSpec((1, d), lambda bb, i: (0, 0)),
                  pl.BlockSpec((1, 1, d), lambda bb, i: (bb, 0, 0))],
        out_specs=full,
        out_shape=jax.ShapeDtypeStruct((bsz, length, d), F32),
        compiler_params=_params(("parallel", "parallel")),
    )(a, b, wa, wb, h, post_g.reshape(1, d), gate)


ODD_NT = 4


def _odd_in_kernel(h_ref, g_ref, sh_ref, sc_ref, wq_ref, wk_ref, wv_ref, wg_ref, wd_ref, wdg_ref,
                   q_ref, k_ref, v_ref, sg_ref, d_ref, sdg_ref, xn_ref):
    @pl.when(pl.program_id(2) == 0)
    def _():
        xn_ref[...] = _prenorm_modulate(h_ref, g_ref, sh_ref, sc_ref)

    xn = xn_ref[...]
    q_ref[0] = (_dot(xn, wq_ref[...]) * (HEAD_DIM ** -0.5)).astype(BF16)
    k_ref[0] = _dot(xn, wk_ref[...]).astype(BF16)
    v_ref[0] = _dot(xn, wv_ref[...]).astype(BF16)
    sg_ref[0] = _silu(_dot(xn, wg_ref[...])).astype(BF16)
    d_ref[0] = _dot(xn, wd_ref[...])
    sdg_ref[0] = _silu(_dot(xn, wdg_ref[...])).astype(BF16)


def odd_in_proj(h, pre_g, shift, scale, w6, tm):
    bsz, length, d = h.shape
    na_w = w6[0].shape[1]
    ssm_w = w6[4].shape[1]
    tn, ts = na_w // ODD_NT, ssm_w // ODD_NT
    vec = pl.BlockSpec((1, 1, d), lambda b, i, n: (b, 0, 0))
    wn = pl.BlockSpec((d, tn), lambda b, i, n: (0, n))
    ws = pl.BlockSpec((d, ts), lambda b, i, n: (0, n))
    on = pl.BlockSpec((1, tm, tn), lambda b, i, n: (b, i, n))
    os_ = pl.BlockSpec((1, tm, ts), lambda b, i, n: (b, i, n))
    na_shape = jax.ShapeDtypeStruct((bsz, length, na_w), BF16)
    return pl.pallas_call(
        _odd_in_kernel,
        grid=(bsz, length // tm, ODD_NT),
        in_specs=[pl.BlockSpec((1, tm, d), lambda b, i, n: (b, i, 0)),
                  pl.BlockSpec((1, d), lambda b, i, n: (0, 0)),
                  vec, vec, wn, wn, wn, wn, ws, ws],
        out_specs=[on, on, on, on, os_, os_],
        out_shape=[na_shape, na_shape, na_shape, na_shape,
                   jax.ShapeDtypeStruct((bsz, length, ssm_w), F32),
                   jax.ShapeDtypeStruct((bsz, length, ssm_w), BF16)],
        scratch_shapes=[pltpu.VMEM((tm, d), BF16)],
        compiler_params=_params(("parallel", "parallel", "arbitrary")),
    )(h, pre_g.reshape(1, d), shift, scale, *w6)


def _na_pair_geometry(variant, rows):
    r0 = {0: 4, 1: 0, 2: 2, 3: rows - 4, 4: rows - 2}[variant]
    ws = min(max(r0 - NA_ROWS // 2, 0), rows - NA_WIN_ROWS)
    return r0, ws


def _na_bias_kernel(rpb_ref, o_ref, *, rows):
    h = pl.program_id(0)
    n_dr, n_dc = 2 * NA_ROWS - 1, 2 * NA_COLS - 1
    qc = lax.broadcasted_iota(jnp.int32, (GRID_W, LANES), 0)
    lane = lax.broadcasted_iota(jnp.int32, (GRID_W, LANES), 1)
    kc = lane % GRID_W
    hi = lane // GRID_W
    diff = kc - qc
    c_start = jnp.clip(qc - NA_COLS // 2, 0, GRID_W - NA_COLS)
    col_ok = jnp.where(kc >= c_start, 1, 0) * jnp.where(kc < c_start + NA_COLS, 1, 0)
    for variant in range(NA_VARIANTS):
        r0, ws = _na_pair_geometry(variant, rows)
        for ri in range(2):
            r = r0 + ri
            r_start = min(max(r - NA_ROWS // 2, 0), rows - NA_ROWS)
            for wp in range(NA_WIN_ROWS // 2):
                key_rows = (ws + 2 * wp, ws + 2 * wp + 1)
                ok = [r_start <= a < r_start + NA_ROWS for a in key_rows]
                dr = [min(max(a - r + NA_ROWS - 1, 0), n_dr - 1) for a in key_rows]
                val = jnp.full((GRID_W, LANES), NEG_INF, F32)
                if ok[0] or ok[1]:
                    for dc in range(n_dc):
                        lo_s = rpb_ref[(h * n_dr + dr[0]) * n_dc + dc]
                        hi_s = rpb_ref[(h * n_dr + dr[1]) * n_dc + dc]
                        val = jnp.where(diff == dc - (NA_COLS - 1), jnp.where(hi == 1, hi_s, lo_s), val)
                    row_ok = jnp.where(hi == 1, int(ok[1]), int(ok[0]))
                    val = jnp.where(col_ok * row_ok == 1, val, NEG_INF)
                o_ref[0, variant, pl.ds(ri * GRID_W, GRID_W), pl.ds(wp * LANES, LANES)] = val


def na_bias_table(rpb, rows):
    heads = rpb.shape[0]
    return pl.pallas_call(
        functools.partial(_na_bias_kernel, rows=rows),
        grid=(heads,),
        in_specs=[pl.BlockSpec(memory_space=pltpu.SMEM)],
        out_specs=pl.BlockSpec((1, NA_VARIANTS, NA_PAIR, NA_WIN), lambda h: (h, 0, 0, 0)),
        out_shape=jax.ShapeDtypeStruct((heads, NA_VARIANTS, NA_PAIR, NA_WIN), F32),
        compiler_params=_params(("parallel",)),
    )(rpb.reshape(-1))


def _na_kernel(q_ref, k_ref, v_ref, kc_ref, vc_ref, bias_ref, sg_ref, o_ref, *, rows):
    pr = pl.program_id(2)
    npairs = rows // 2
    ws = jnp.clip(2 * pr - NA_ROWS // 2, 0, rows - NA_WIN_ROWS)
    start = pl.multiple_of(ws * GRID_W, NA_PAIR)
    variant = jnp.where(pr == 0, 1, jnp.where(pr == 1, 2, jnp.where(
        pr == npairs - 2, 3, jnp.where(pr == npairs - 1, 4, 0))))
    q = q_ref[0]
    kw = k_ref[0, pl.ds(start, NA_WIN), :]
    vw = v_ref[0, pl.ds(start, NA_WIN), :]
    nt = (((1,), (1,)), ((), ()))
    s_loc = lax.dot_general(q, kw, nt, preferred_element_type=F32) + bias_ref[0, variant]
    s_ctx = lax.dot_general(q, kc_ref[0], nt, preferred_element_type=F32)
    m = jnp.maximum(jnp.max(s_loc, axis=-1, keepdims=True), jnp.max(s_ctx, axis=-1, keepdims=True))
    p_loc = jnp.exp(s_loc - m)
    p_ctx = jnp.exp(s_ctx - m)
    denom = jnp.sum(p_loc, axis=-1, keepdims=True) + jnp.sum(p_ctx, axis=-1, keepdims=True)
    o = (_dot(p_loc.astype(BF16), vw) + _dot(p_ctx.astype(BF16), vc_ref[0])) / denom
    o_ref[0] = (o * sg_ref[0].astype(F32)).astype(BF16)


def neighbourhood_attention(q, k, v, k_c, v_c, bias, sg):
    bsz, length, na_w = q.shape
    lc = k_c.shape[1]
    heads = na_w // HEAD_DIM
    rows = length // GRID_W
    tile = pl.BlockSpec((1, NA_PAIR, HEAD_DIM), lambda h, b, p: (b, p, h))
    seq = pl.BlockSpec((1, length, HEAD_DIM), lambda h, b, p: (b, 0, h))
    cseq = pl.BlockSpec((1, lc, HEAD_DIM), lambda h, b, p: (b, 0, h))
    return pl.pallas_call(
        functools.partial(_na_kernel, rows=rows),
        grid=(heads, bsz, rows // 2),
        in_specs=[tile, seq, seq, cseq, cseq,
                  pl.BlockSpec((1, NA_VARIANTS, NA_PAIR, NA_WIN), lambda h, b, p: (h, 0, 0, 0)),
                  tile],
        out_specs=tile,
        out_shape=jax.ShapeDtypeStruct((bsz, length, na_w), BF16),
        compiler_params=_params(("parallel", "parallel", "arbitrary")),
    )(q, k, v, k_c, v_c, bias, sg)


S5_CHUNK = 128
S5_IN_BLOCK = LANES
S5_OUT_BLOCK = 256
S5_SCAN_COLS = 512


def _s5_kernel(uf_ref, ub_ref, bf_ref, bb_ref, crf_ref, cif_ref, crb_ref, cib_ref, lre_ref, lim_ref,
               yf_ref, yb_ref, buf_ref, state_ref, stage_ref):
    bsz, t_len, width = uf_ref.shape
    n_state = lre_ref.shape[1]
    n_in = width // S5_IN_BLOCK
    blk = n_state // n_in
    rows8 = 2 * bsz

    @pl.when(pl.program_id(0) == 0)
    def _():
        state_ref[...] = jnp.zeros_like(state_ref)

    nlb = n_state // LANES
    per_in = blk // LANES
    for di, (u_ref, w_ref) in enumerate(((uf_ref, bf_ref), (ub_ref, bb_ref))):
        u = u_ref[...].reshape(bsz * t_len, width).astype(BF16)
        for kb in range(n_in):
            res = _dot(u[:, kb * S5_IN_BLOCK:(kb + 1) * S5_IN_BLOCK], w_ref[kb])
            for b in range(bsz):
                rsel = pl.ds(di * bsz + b, t_len, stride=rows8)
                for c in range(2 * per_in):
                    dst = (c // per_in) * nlb + kb * per_in + c % per_in
                    buf_ref[dst, rsel, :] = res[b * t_len:(b + 1) * t_len, c * LANES:(c + 1) * LANES]

    lo = lax.broadcasted_iota(jnp.int32, (rows8, LANES), 0) < bsz
    per = S5_SCAN_COLS // LANES
    for cbk in range(nlb // per):
        blocks = [cbk * per + i for i in range(per)]
        lre = [lre_ref[:, pl.ds(c * LANES, LANES)] for c in blocks]
        lim = [lim_ref[:, pl.ds(c * LANES, LANES)] for c in blocks]

        def step(t, carry, blocks=blocks, lre=lre, lim=lim):
            ra = pl.ds(pl.multiple_of(t * rows8, rows8), rows8)
            rb = pl.ds(pl.multiple_of((t_len - 1 - t) * rows8, rows8), rows8)
            new = []
            for i, c in enumerate(blocks):
                hre, him = carry[2 * i], carry[2 * i + 1]
                a_re, a_im = buf_ref[c, ra, :], buf_ref[nlb + c, ra, :]
                b_re, b_im = buf_ref[c, rb, :], buf_ref[nlb + c, rb, :]
                nre = lre[i] * hre - lim[i] * him + jnp.where(lo, a_re, b_re)
                nim = lre[i] * him + lim[i] * hre + jnp.where(lo, a_im, b_im)
                buf_ref[c, ra, :] = jnp.where(lo, nre, a_re)
                buf_ref[nlb + c, ra, :] = jnp.where(lo, nim, a_im)
                buf_ref[c, rb, :] = jnp.where(lo, b_re, nre)
                buf_ref[nlb + c, rb, :] = jnp.where(lo, b_im, nim)
                new += [nre, nim]
            return tuple(new)

        init = tuple(state_ref[:, pl.ds(part * n_state + c * LANES, LANES)] for c in blocks for part in range(2))
        fin = lax.fori_loop(0, t_len, step, init)
        for i, c in enumerate(blocks):
            state_ref[:, pl.ds(c * LANES, LANES)] = fin[2 * i]
            state_ref[:, pl.ds(n_state + c * LANES, LANES)] = fin[2 * i + 1]

    n_out = width // S5_OUT_BLOCK
    per_out = nlb // n_out
    for di, (y_ref, cr_ref, ci_ref) in enumerate(((yf_ref, crf_ref, cif_ref), (yb_ref, crb_ref, cib_ref))):
        for j in range(n_out):
            for part in range(2):
                for b in range(bsz):
                    rsel = pl.ds(di * bsz + b, t_len, stride=rows8)
                    for c in range(per_out):
                        stage_ref[part, pl.ds(b * t_len, t_len), pl.ds(c * LANES, LANES)] = buf_ref[
                            part * nlb + j * per_out + c, rsel, :].astype(BF16)
            y = _dot(stage_ref[0], cr_ref[j]) + _dot(stage_ref[1], ci_ref[j])
            for b in range(bsz):
                y_ref[b, :, pl.ds(j * S5_OUT_BLOCK, S5_OUT_BLOCK)] = y[b * t_len:(b + 1) * t_len]


def _block_diag(x):
    nblk, g, r, c = x.shape
    eye = jnp.eye(g, dtype=x.dtype)
    return jnp.einsum("kgrc,gh->kgrhc", x, eye).reshape(nblk, g * r, g * c)


def _s5_direction_params(a_re, a_im, log_dt, b_re, b_im, c_re, c_im):
    groups, n_p = a_re.shape
    lam = lax.complex(a_re.astype(F32), a_im.astype(F32))
    dt = jnp.exp(log_dt.astype(F32))[:, None]
    lam_bar = jnp.exp(lam * dt)
    b_bar = ((lam_bar - 1.0) / lam)[..., None] * lax.complex(b_re.astype(F32), b_im.astype(F32))
    gi = S5_IN_BLOCK // SSM_GROUP
    bt = jnp.swapaxes(b_bar, 1, 2).reshape(groups // gi, gi, SSM_GROUP, n_p)
    b_mat = jnp.concatenate([_block_diag(jnp.real(bt)), _block_diag(jnp.imag(bt))], axis=-1)
    go = S5_OUT_BLOCK // SSM_GROUP
    ct = lambda c: jnp.swapaxes(c.astype(F32), 1, 2).reshape(groups // go, go, n_p, SSM_GROUP)
    return (b_mat.astype(BF16), _block_diag(ct(c_re)).astype(BF16), _block_diag(-ct(c_im)).astype(BF16),
            jnp.real(lam_bar).reshape(-1), jnp.imag(lam_bar).reshape(-1))


def s5_scan(u_fwd, u_bwd, fwd, bwd):
    bsz, ltot, width = u_fwd.shape
    t_len = S5_CHUNK
    nc = ltot // t_len
    n_state = fwd[3].shape[0]
    lre = jnp.concatenate([jnp.broadcast_to(fwd[3], (bsz, n_state)), jnp.broadcast_to(bwd[3], (bsz, n_state))])
    lim = jnp.concatenate([jnp.broadcast_to(fwd[4], (bsz, n_state)), jnp.broadcast_to(bwd[4], (bsz, n_state))])
    full = lambda x: pl.BlockSpec(x.shape, lambda c, nd=x.ndim: (0,) * nd)
    fblk = pl.BlockSpec((bsz, t_len, width), lambda c: (0, c, 0))
    bblk = pl.BlockSpec((bsz, t_len, width), lambda c: (0, nc - 1 - c, 0))
    out = jax.ShapeDtypeStruct((bsz, ltot, width), F32)
    consts = (fwd[0], bwd[0], fwd[1], fwd[2], bwd[1], bwd[2], lre, lim)
    return pl.pallas_call(
        _s5_kernel,
        grid=(nc,),
        in_specs=[fblk, bblk] + [full(x) for x in consts],
        out_specs=[fblk, bblk],
        out_shape=[out, out],
        scratch_shapes=[pltpu.VMEM((2 * n_state // LANES, t_len * 2 * bsz, LANES), F32),
                        pltpu.VMEM((2 * bsz, 2 * n_state), F32),
                        pltpu.VMEM((2, bsz * t_len, n_state // (width // S5_OUT_BLOCK)), BF16)],
        compiler_params=_params(("arbitrary",)),
    )(u_fwd, u_bwd, *consts)


def _odd_out_kernel(na_ref, yf_ref, yb_ref, d_ref, sdg_ref, dsk_ref, wglu_ref, wna_ref, wssm_ref,
                    h_ref, pg_ref, gt_ref, o_ref):
    y = yf_ref[0] + yb_ref[0] + dsk_ref[...] * d_ref[0]
    y = 0.5 * y * (1.0 + jnp.tanh(math.sqrt(2.0 / math.pi) * (y + 0.044715 * (y * y * y))))
    z = y * jax.nn.sigmoid(_dot(y.astype(BF16), wglu_ref[...]))
    s = (z * sdg_ref[0].astype(F32)).astype(BF16)
    out = _dot(na_ref[0], wna_ref[...]) + _dot(s, wssm_ref[...])
    _postnorm_residual(out, h_ref, pg_ref, gt_ref, o_ref)


def odd_out_proj(na, yf, yb, d, sdg, d_skip, w_glu, w_na, w_ssm, h, post_g, gate, ctx_len, tm):
    bsz, length, dm = h.shape
    na_w, ssm_w = na.shape[2], d.shape[2]
    off = ctx_len // tm
    tok = lambda w: pl.BlockSpec((1, tm, w), lambda b, i: (b, i, 0))
    const = lambda r, c: pl.BlockSpec((r, c), lambda b, i: (0, 0))
    return pl.pallas_call(
        _odd_out_kernel,
        grid=(bsz, length // tm),
        in_specs=[tok(na_w),
                  pl.BlockSpec((1, tm, ssm_w), lambda b, i: (b, i + off, 0)),
                  tok(ssm_w),
                  tok(ssm_w), tok(ssm_w),
                  const(1, ssm_w), const(ssm_w, ssm_w), const(na_w, dm), const(ssm_w, dm),
                  tok(dm), const(1, dm),
                  pl.BlockSpec((1, 1, dm), lambda b, i: (b, 0, 0))],
        out_specs=tok(dm),
        out_shape=jax.ShapeDtypeStruct((bsz, length, dm), F32),
        compiler_params=_params(("parallel", "parallel")),
    )(na, yf, yb, d, sdg, d_skip.reshape(1, ssm_w), w_glu, w_na, w_ssm, h, post_g.reshape(1, dm), gate)


def _small_dft(n):
    k = jnp.arange(n, dtype=jnp.int32)
    ang = ((k[:, None] * k[None, :]) % n).astype(F32) * (2.0 * math.pi / n)
    return jnp.concatenate([jnp.cos(ang), jnp.sin(ang)], axis=1).astype(BF16)


def _conv_fourier_layer(h, pre_g, post_g, shift, scale, gate, w5, w_out_a, w_out_b, conv_w, conv_b,
                        ln_g, ln_b, fourier_g, cs_small, tm, tl, tdft):
    length = h.shape[1]
    gd = fourier_g.shape[1]
    a, sga, p, q, sgb = even_in_proj(h, pre_g, shift, scale, w5, fourier_g, cs_small, tm)
    a = conv_branch(a, sga, conv_w, conv_b, ln_g, ln_b, tl)
    cmat, smat = dft_matrices(length)
    f = fourier_branch(cmat, smat, p, q, sgb, 1.0 / math.sqrt(length * gd), tdft, tdft)
    return even_out_proj(a, f, w_out_a, w_out_b, h, post_g, gate, tm)


def kernel(x, c, ctx, c_ctx, pre_g, post_g, ada_w, ada_b, ab_w_in, ab_w_out, conv_w, conv_b, conv_ln_g,
           conv_ln_b, fourier_g, cd_w_in, cd_w_out, na_rpb, s5_a_re, s5_a_im, s5_log_dt, s5_b_re, s5_b_im,
           s5_c_re, s5_c_im, s5_d, s5_w_glu):
    bsz, length, d = x.shape
    lc = ctx.shape[1]
    depth = ada_w.shape[0]
    assert depth == 2 and length % (2 * GRID_W) == 0 and length // GRID_W >= NA_WIN_ROWS + 2
    assert lc % S5_CHUNK == 0 and length % S5_CHUNK == 0

    cond_rows = jnp.zeros((SUBLANES, d), F32).at[:bsz].set(c).at[bsz].set(c_ctx)
    mods = ada_modulation(cond_rows, ada_w, ada_b)

    def mod_vectors(i):
        xs = [mods[i, :bsz, k * d:(k + 1) * d][:, None, :] for k in range(3)]
        cs = [jnp.broadcast_to(mods[i, bsz, k * d:(k + 1) * d], (bsz, 1, d)) for k in range(3)]
        return xs, cs

    (sh_x, sc_x, gt_x), (sh_c, sc_c, gt_c) = mod_vectors(0)
    w_in = ab_w_in[0]
    conv_width = conv_w.shape[2]
    cw = conv_width
    fw = (w_in.shape[1] - 3 * cw) // 2
    w5 = [w_in[:, s:e].astype(BF16) for s, e in
          ((0, cw), (cw, 2 * cw), (2 * cw, 3 * cw), (3 * cw, 3 * cw + fw), (3 * cw + fw, 3 * cw + 2 * fw))]
    w_out_a = ab_w_out[0, :cw].astype(BF16)
    w_out_b = ab_w_out[0, cw:].astype(BF16)
    cs_small = _small_dft(fourier_g.shape[2])
    layer0 = functools.partial(
        _conv_fourier_layer, pre_g=pre_g[0], post_g=post_g[0], w5=w5, w_out_a=w_out_a, w_out_b=w_out_b,
        conv_w=conv_w[0], conv_b=conv_b[0], ln_g=conv_ln_g[0], ln_b=conv_ln_b[0], fourier_g=fourier_g[0],
        cs_small=cs_small)
    h_x = layer0(x, shift=sh_x, scale=sc_x, gate=gt_x, tm=512, tl=256, tdft=1024)
    h_c = layer0(ctx, shift=sh_c, scale=sc_c, gate=gt_c, tm=lc, tl=lc, tdft=lc)

    (sh_x, sc_x, gt_x), (sh_c, sc_c, _) = mod_vectors(1)
    w_in = cd_w_in[0]
    ssm_w = s5_d.shape[1]
    na_w = (w_in.shape[1] - 2 * ssm_w) // 4
    edges = (0, na_w, 2 * na_w, 3 * na_w, 4 * na_w, 4 * na_w + ssm_w, 4 * na_w + 2 * ssm_w)
    w6 = [w_in[:, s:e].astype(BF16) for s, e in zip(edges[:-1], edges[1:])]
    q_x, k_x, v_x, sg_x, d_x, sdg_x = odd_in_proj(h_x, pre_g[1], sh_x, sc_x, w6, tm=512)
    _, k_c, v_c, _, d_c, _ = odd_in_proj(h_c, pre_g[1], sh_c, sc_c, w6, tm=lc)

    bias = na_bias_table(na_rpb[0], length // GRID_W)
    na = neighbourhood_attention(q_x, k_x, v_x, k_c, v_c, bias, sg_x)

    dirs = [_s5_direction_params(s5_a_re[0, i], s5_a_im[0, i], s5_log_dt[0, i], s5_b_re[0, i], s5_b_im[0, i],
                                 s5_c_re[0, i], s5_c_im[0, i]) for i in range(2)]
    y_f, y_b = s5_scan(jnp.concatenate([d_c, d_x], axis=1), jnp.concatenate([d_x, d_c], axis=1), *dirs)

    return odd_out_proj(na, y_f, y_b, d_x, sdg_x, s5_d[0], s5_w_glu[0].astype(BF16),
                        cd_w_out[0, :na_w].astype(BF16), cd_w_out[0, na_w:].astype(BF16),
                        h_x, post_g[1], gt_x, lc, tm=256)
```

```python
import functools
import math

import jax
import jax.numpy as jnp
from jax import lax
from jax.experimental import pallas as pl
from jax.experimental.pallas import tpu as pltpu

F32 = jnp.float32
BF16 = jnp.bfloat16

EPS = 1e-6
NEG_INF = -1e30

GRID_W = 64
CONV_K = 31
FOURIER_GROUPS = 4
HEAD_DIM = 128
NA_ROWS = 8
NA_COLS = 16
SSM_GROUP = 16
SSM_STATE = 64

LANES = 128
SUBLANES = 8
VMEM_LIMIT = 56 * 1024 * 1024

NA_PAIR = 2 * GRID_W
NA_WIN_ROWS = NA_ROWS + 2
NA_WIN = NA_WIN_ROWS * GRID_W
NA_VARIANTS = 5


def _params(sem, vmem=VMEM_LIMIT):
    return pltpu.CompilerParams(dimension_semantics=sem, vmem_limit_bytes=vmem)


def _silu(x):
    return x * jax.nn.sigmoid(x)


def _rms(x):
    return x * lax.rsqrt(jnp.mean(x * x, axis=-1, keepdims=True) + EPS)


def _dot(a, b):
    return jnp.dot(a, b, preferred_element_type=F32)


def _ada_kernel(c_ref, w_ref, b_ref, o_ref):
    cond = _silu(c_ref[...])
    o_ref[0] = _dot(cond.astype(BF16), w_ref[0].astype(BF16)) + b_ref[0]


def ada_modulation(cond_rows, ada_w, ada_b, tn=1024):
    depth, d, n = ada_w.shape
    rows = cond_rows.shape[0]
    return pl.pallas_call(
        _ada_kernel,
        name="ada_modulation",
        grid=(depth, n // tn),
        in_specs=[
            pl.BlockSpec((rows, d), lambda i, j: (0, 0)),
            pl.BlockSpec((1, d, tn), lambda i, j: (i, 0, j)),
            pl.BlockSpec((1, 1, tn), lambda i, j: (i, 0, j)),
        ],
        out_specs=pl.BlockSpec((1, rows, tn), lambda i, j: (i, 0, j)),
        out_shape=jax.ShapeDtypeStruct((depth, rows, n), F32),
        compiler_params=_params(("parallel", "parallel")),
    )(cond_rows, ada_w, ada_b.reshape(depth, 1, n))


def _prenorm_modulate(h_ref, g_ref, sh_ref, sc_ref):
    u = _rms(h_ref[0]) * g_ref[...]
    return (u * (1.0 + sc_ref[0]) + sh_ref[0]).astype(BF16)


def _even_in_kernel(h_ref, g_ref, sh_ref, sc_ref, wv_ref, wg_ref, wt_ref, wb_ref, wbg_ref,
                    fg_ref, cs_ref, a_ref, sga_ref, p_ref, q_ref, sgb_ref, xn_ref):
    @pl.when(pl.program_id(2) == 0)
    def _():
        xn_ref[...] = _prenorm_modulate(h_ref, g_ref, sh_ref, sc_ref)

    xn = xn_ref[...]
    a = _dot(xn, wv_ref[...]) * jax.nn.sigmoid(_dot(xn, wg_ref[...]))
    a_ref[0] = a.astype(BF16)
    sga_ref[0] = _silu(_dot(xn, wt_ref[...])).astype(BF16)
    bn = _rms(_dot(xn, wb_ref[...])) * fg_ref[0]
    pq = _dot(bn.astype(BF16), cs_ref[...])
    gd = pq.shape[1] // 2
    p_ref[0] = pq[:, :gd].astype(BF16)
    q_ref[0] = pq[:, gd:].astype(BF16)
    sgb_ref[0] = _silu(_dot(xn, wbg_ref[...])).astype(BF16)


def even_in_proj(h, pre_g, shift, scale, w5, fourier_g, cs_small, tm):
    bsz, length, d = h.shape
    width = w5[0].shape[1]
    gd = width // FOURIER_GROUPS
    wspec = pl.BlockSpec((d, gd), lambda b, i, n: (0, n))
    ospec = pl.BlockSpec((1, tm, gd), lambda b, i, n: (b, i, n))
    vec = pl.BlockSpec((1, 1, d), lambda b, i, n: (b, 0, 0))
    out = jax.ShapeDtypeStruct((bsz, length, width), BF16)
    return pl.pallas_call(
        _even_in_kernel,
        name="even_in_proj",
        grid=(bsz, length // tm, FOURIER_GROUPS),
        in_specs=[
            pl.BlockSpec((1, tm, d), lambda b, i, n: (b, i, 0)),
            pl.BlockSpec((1, d), lambda b, i, n: (0, 0)),
            vec, vec,
            wspec, wspec, wspec, wspec, wspec,
            pl.BlockSpec((1, 1, gd), lambda b, i, n: (n, 0, 0)),
            pl.BlockSpec((gd, 2 * gd), lambda b, i, n: (0, 0)),
        ],
        out_specs=[ospec] * 5,
        out_shape=[out] * 5,
        scratch_shapes=[pltpu.VMEM((tm, d), BF16)],
        compiler_params=_params(("parallel", "parallel", "arbitrary")),
    )(h, pre_g.reshape(1, d), shift, scale, *w5, fourier_g.reshape(FOURIER_GROUPS, 1, gd), cs_small)


CONV_HALO = 16
CONV_ROWS = 32


def _conv_kernel(a_ref, prev_ref, next_ref, w_ref, cb_ref, lg_ref, lb_ref, sga_ref, o_ref,
                 ext_ref, sh_ref, acc_ref):
    i = pl.program_id(1)
    last = pl.num_programs(1) - 1
    tl, width = acc_ref.shape
    ext_ref[pl.ds(CONV_HALO, tl), :] = a_ref[0].astype(F32)
    ext_ref[pl.ds(0, CONV_HALO), :] = jnp.where(i > 0, prev_ref[0].astype(F32), 0.0)
    ext_ref[pl.ds(CONV_HALO + tl, CONV_HALO), :] = jnp.where(i < last, next_ref[0].astype(F32), 0.0)
    base = CONV_HALO - CONV_K // 2
    span = sh_ref.shape[1]
    for r in range(1, SUBLANES):
        sh_ref[r - 1] = ext_ref[pl.ds(r, span), :]

    def rows(rb, carry):
        r0 = pl.multiple_of(rb * CONV_ROWS, CONV_ROWS)
        for cb in range(width // LANES):
            cols = pl.ds(cb * LANES, LANES)
            acc = jnp.zeros((CONV_ROWS, LANES), F32)
            for k in range(CONV_K):
                q, r = divmod(base + k, SUBLANES)
                rsel = pl.ds(r0 + q * SUBLANES, CONV_ROWS)
                tap = ext_ref[rsel, cols] if r == 0 else sh_ref[r - 1, rsel, cols]
                acc = acc + tap * w_ref[pl.ds(k, 1), cols]
            acc_ref[pl.ds(r0, CONV_ROWS), cols] = acc + cb_ref[:, cols]
        return carry

    lax.fori_loop(0, tl // CONV_ROWS, rows, 0)
    x = acc_ref[...]
    mu = jnp.mean(x, axis=-1, keepdims=True)
    xc = x - mu
    var = jnp.mean(xc * xc, axis=-1, keepdims=True)
    y = xc * lax.rsqrt(var + EPS) * lg_ref[...] + lb_ref[...]
    o_ref[0] = (_silu(y) * sga_ref[0].astype(F32)).astype(BF16)


def conv_branch(a, sga, conv_w, conv_b, ln_g, ln_b, tl):
    bsz, length, width = a.shape
    hb = tl // CONV_HALO
    nh = length // CONV_HALO
    row = pl.BlockSpec((1, width), lambda b, i: (0, 0))
    main = pl.BlockSpec((1, tl, width), lambda b, i: (b, i, 0))
    return pl.pallas_call(
        _conv_kernel,
        name="conv_branch",
        grid=(bsz, length // tl),
        in_specs=[
            main,
            pl.BlockSpec((1, CONV_HALO, width), lambda b, i: (b, jnp.maximum(i * hb - 1, 0), 0)),
            pl.BlockSpec((1, CONV_HALO, width), lambda b, i: (b, jnp.minimum((i + 1) * hb, nh - 1), 0)),
            pl.BlockSpec((CONV_K, width), lambda b, i: (0, 0)),
            row, row, row,
            main,
        ],
        out_specs=main,
        out_shape=jax.ShapeDtypeStruct((bsz, length, width), BF16),
        scratch_shapes=[pltpu.VMEM((tl + 2 * CONV_HALO, width), F32),
                        pltpu.VMEM((SUBLANES - 1, tl + 2 * CONV_HALO - SUBLANES, width), F32),
                        pltpu.VMEM((tl, width), F32)],
        compiler_params=_params(("parallel", "parallel")),
    )(a, a, a, conv_w, conv_b.reshape(1, width), ln_g.reshape(1, width), ln_b.reshape(1, width), sga)


DFT_SPLIT = 64


def _dft_gen_kernel(ca_ref, sa_ref, cb_ref, sb_ref, c_ref, s_ref):
    ca, sa = ca_ref[0], sa_ref[0]
    cb, sb = cb_ref[...], sb_ref[...]
    c_ref[...] = (ca * cb - sa * sb).astype(BF16)
    s_ref[...] = (-(sa * cb + ca * sb)).astype(BF16)


def dft_matrices(length):
    na = length // DFT_SPLIT
    k = jnp.arange(length, dtype=jnp.int32)[None, :]
    ia = jnp.arange(na, dtype=jnp.int32)[:, None]
    ib = jnp.arange(DFT_SPLIT, dtype=jnp.int32)[:, None]
    w = 2.0 * math.pi / length
    ang_a = ((DFT_SPLIT * ia * k) % length).astype(F32) * w
    ang_b = ((ib * k) % length).astype(F32) * w
    ca, sa = jnp.cos(ang_a).reshape(na, 1, length), jnp.sin(ang_a).reshape(na, 1, length)
    cb, sb = jnp.cos(ang_b), jnp.sin(ang_b)
    tab_a = pl.BlockSpec((1, 1, length), lambda a: (a, 0, 0))
    tab_b = pl.BlockSpec((DFT_SPLIT, length), lambda a: (0, 0))
    out = pl.BlockSpec((DFT_SPLIT, length), lambda a: (a, 0))
    shp = jax.ShapeDtypeStruct((length, length), BF16)
    return pl.pallas_call(
        _dft_gen_kernel,
        name="dft_matrices",
        grid=(na,),
        in_specs=[tab_a, tab_a, tab_b, tab_b],
        out_specs=[out, out],
        out_shape=[shp, shp],
        compiler_params=_params(("parallel",)),
    )(ca, sa, cb, sb)


def _dft_kernel(c_ref, s_ref, p_ref, q_ref, sgb_ref, o_ref, acc_ref, *, scale):
    kk = pl.program_id(2)

    @pl.when(kk == 0)
    def _():
        acc_ref[...] = jnp.zeros_like(acc_ref)

    acc_ref[...] += _dot(c_ref[...], p_ref[0]) + _dot(s_ref[...], q_ref[0])

    @pl.when(kk == pl.num_programs(2) - 1)
    def _():
        o_ref[0] = (acc_ref[...] * scale * sgb_ref[0].astype(F32)).astype(BF16)


def fourier_branch(cmat, smat, p, q, sgb, scale, tm, tk):
    bsz, length, width = p.shape
    return pl.pallas_call(
        functools.partial(_dft_kernel, scale=scale),
        name="fourier_branch",
        grid=(bsz, length // tm, length // tk),
        in_specs=[
            pl.BlockSpec((tm, tk), lambda b, m, k: (m, k)),
            pl.BlockSpec((tm, tk), lambda b, m, k: (m, k)),
            pl.BlockSpec((1, tk, width), lambda b, m, k: (b, k, 0)),
            pl.BlockSpec((1, tk, width), lambda b, m, k: (b, k, 0)),
            pl.BlockSpec((1, tm, width), lambda b, m, k: (b, m, 0)),
        ],
        out_specs=pl.BlockSpec((1, tm, width), lambda b, m, k: (b, m, 0)),
        out_shape=jax.ShapeDtypeStruct((bsz, length, width), BF16),
        scratch_shapes=[pltpu.VMEM((tm, width), F32)],
        compiler_params=_params(("parallel", "parallel", "arbitrary")),
    )(cmat, smat, p, q, sgb)


def _postnorm_residual(y, h_ref, pg_ref, gt_ref, o_ref):
    o_ref[0] = h_ref[0] + gt_ref[0] * (_rms(y) * pg_ref[...])


def _even_out_kernel(a_ref, b_ref, wa_ref, wb_ref, h_ref, pg_ref, gt_ref, o_ref):
    y = _dot(a_ref[0], wa_ref[...]) + _dot(b_ref[0], wb_ref[...])
    _postnorm_residual(y, h_ref, pg_ref, gt_ref, o_ref)


def even_out_proj(a, b, wa, wb, h, post_g, gate, tm):
    bsz, length, d = h.shape
    width = a.shape[2]
    half = pl.BlockSpec((1, tm, width), lambda bb, i: (bb, i, 0))
    wspec = pl.BlockSpec((width, d), lambda bb, i: (0, 0))
    full = pl.BlockSpec((1, tm, d), lambda bb, i: (bb, i, 0))
    return pl.pallas_call(
        _even_out_kernel,
        name="even_out_proj",
        grid=(bsz, length // tm),
        in_specs=[half, half, wspec, wspec, full,
                  pl.BlockSpec((1, d), lambda bb, i: (0, 0)),
                  pl.BlockSpec((1, 1, d), lambda bb, i: (bb, 0, 0))],
        out_specs=full,
        out_shape=jax.ShapeDtypeStruct((bsz, length, d), F32),
        compiler_params=_params(("parallel", "parallel")),
    )(a, b, wa, wb, h, post_g.reshape(1, d), gate)


ODD_TN = 1024


def _odd_in_kernel(h_ref, g_ref, sh_ref, sc_ref, w_ref, o_ref, xn_ref, *, first_gate_tile):
    n = pl.program_id(2)

    @pl.when(n == 0)
    def _():
        xn_ref[...] = _prenorm_modulate(h_ref, g_ref, sh_ref, sc_ref)

    acc = _dot(xn_ref[...], w_ref[...])

    @pl.when(n < first_gate_tile)
    def _():
        o_ref[0] = acc.astype(BF16)

    @pl.when(n >= first_gate_tile)
    def _():
        o_ref[0] = _silu(acc).astype(BF16)


def odd_in_proj(h, pre_g, shift, scale, w_all, n_plain, tm):
    bsz, length, d = h.shape
    n_all = w_all.shape[1]
    vec = pl.BlockSpec((1, 1, d), lambda b, i, n: (b, 0, 0))
    return pl.pallas_call(
        functools.partial(_odd_in_kernel, first_gate_tile=n_plain // ODD_TN),
        name="odd_in_proj",
        grid=(bsz, length // tm, n_all // ODD_TN),
        in_specs=[pl.BlockSpec((1, tm, d), lambda b, i, n: (b, i, 0)),
                  pl.BlockSpec((1, d), lambda b, i, n: (0, 0)),
                  vec, vec,
                  pl.BlockSpec((d, ODD_TN), lambda b, i, n: (0, n))],
        out_specs=pl.BlockSpec((1, tm, ODD_TN), lambda b, i, n: (b, i, n)),
        out_shape=jax.ShapeDtypeStruct((bsz, length, n_all), BF16),
        scratch_shapes=[pltpu.VMEM((tm, d), BF16)],
        compiler_params=_params(("parallel", "parallel", "arbitrary")),
    )(h, pre_g.reshape(1, d), shift, scale, w_all)


def _na_pair_geometry(variant, rows):
    r0 = {0: 4, 1: 0, 2: 2, 3: rows - 4, 4: rows - 2}[variant]
    ws = min(max(r0 - NA_ROWS // 2, 0), rows - NA_WIN_ROWS)
    return r0, ws


def _na_bias_kernel(rpb_ref, o_ref, *, rows):
    h = pl.program_id(0)
    n_dr, n_dc = 2 * NA_ROWS - 1, 2 * NA_COLS - 1
    qc = lax.broadcasted_iota(jnp.int32, (GRID_W, LANES), 0)
    lane = lax.broadcasted_iota(jnp.int32, (GRID_W, LANES), 1)
    kc = lane % GRID_W
    hi = lane // GRID_W
    diff = kc - qc
    c_start = jnp.clip(qc - NA_COLS // 2, 0, GRID_W - NA_COLS)
    col_ok = jnp.where(kc >= c_start, 1, 0) * jnp.where(kc < c_start + NA_COLS, 1, 0)
    for variant in range(NA_VARIANTS):
        r0, ws = _na_pair_geometry(variant, rows)
        for ri in range(2):
            r = r0 + ri
            r_start = min(max(r - NA_ROWS // 2, 0), rows - NA_ROWS)
            for wp in range(NA_WIN_ROWS // 2):
                key_rows = (ws + 2 * wp, ws + 2 * wp + 1)
                ok = [r_start <= a < r_start + NA_ROWS for a in key_rows]
                dr = [min(max(a - r + NA_ROWS - 1, 0), n_dr - 1) for a in key_rows]
                val = jnp.full((GRID_W, LANES), NEG_INF, F32)
                if ok[0] or ok[1]:
                    for dc in range(n_dc):
                        lo_s = rpb_ref[(h * n_dr + dr[0]) * n_dc + dc]
                        hi_s = rpb_ref[(h * n_dr + dr[1]) * n_dc + dc]
                        val = jnp.where(diff == dc - (NA_COLS - 1), jnp.where(hi == 1, hi_s, lo_s), val)
                    row_ok = jnp.where(hi == 1, int(ok[1]), int(ok[0]))
                    val = jnp.where(col_ok * row_ok == 1, val, NEG_INF)
                o_ref[0, variant, pl.ds(ri * GRID_W, GRID_W), pl.ds(wp * LANES, LANES)] = val


def na_bias_table(rpb, rows):
    heads = rpb.shape[0]
    return pl.pallas_call(
        functools.partial(_na_bias_kernel, rows=rows),
        name="na_bias_table",
        grid=(heads,),
        in_specs=[pl.BlockSpec(memory_space=pltpu.SMEM)],
        out_specs=pl.BlockSpec((1, NA_VARIANTS, NA_PAIR, NA_WIN), lambda h: (h, 0, 0, 0)),
        out_shape=jax.ShapeDtypeStruct((heads, NA_VARIANTS, NA_PAIR, NA_WIN), F32),
        compiler_params=_params(("parallel",)),
    )(rpb.reshape(-1))


def _na_kernel(q_ref, k_ref, v_ref, kc_ref, vc_ref, bias_ref, sg_ref, o_ref,
               vx_ref, vcx_ref, s0_ref, s1_ref, p0_ref, p1_ref, *, rows):
    npairs = rows // 2
    last = npairs - 1
    nt = (((1,), (1,)), ((), ()))
    vx_ref[:, :HEAD_DIM] = v_ref[0]
    vx_ref[:, HEAD_DIM:] = jnp.ones((vx_ref.shape[0], HEAD_DIM), BF16)
    vcx_ref[:, :HEAD_DIM] = vc_ref[0]
    vcx_ref[:, HEAD_DIM:] = jnp.ones((vcx_ref.shape[0], HEAD_DIM), BF16)

    def window(pr):
        ws = jnp.clip(2 * pr - NA_ROWS // 2, 0, rows - NA_WIN_ROWS)
        return pl.ds(pl.multiple_of(ws * GRID_W, NA_PAIR), NA_WIN)

    def qrows(pr):
        return pl.ds(pl.multiple_of(pr * NA_PAIR, NA_PAIR), NA_PAIR)

    def scores(pr, s_ref):
        variant = jnp.where(pr == 0, 1, jnp.where(pr == 1, 2, jnp.where(
            pr == last - 1, 3, jnp.where(pr == last, 4, 0))))
        q = q_ref[0, qrows(pr), :]
        s_ref[:, :NA_WIN] = lax.dot_general(q, k_ref[0, window(pr), :], nt,
                                            preferred_element_type=F32) + bias_ref[0, variant]
        s_ref[:, NA_WIN:] = lax.dot_general(q, kc_ref[0], nt, preferred_element_type=F32)

    def probs(s_ref, p_ref):
        s = s_ref[...]
        p_ref[...] = jnp.exp(s - jnp.max(s, axis=-1, keepdims=True)).astype(BF16)

    def values(pr, p_ref):
        acc = _dot(p_ref[:, :NA_WIN], vx_ref[window(pr), :]) + _dot(p_ref[:, NA_WIN:], vcx_ref[...])
        o = acc[:, :HEAD_DIM] / acc[:, HEAD_DIM:]
        o_ref[0, qrows(pr), :] = (o * sg_ref[0, qrows(pr), :].astype(F32)).astype(BF16)

    scores(0, s0_ref)
    scores(1, s1_ref)
    probs(s0_ref, p0_ref)

    def two_pairs(i2, carry):
        i = 2 * i2
        values(i, p0_ref)
        probs(s1_ref, p1_ref)
        scores(jnp.minimum(i + 2, last), s0_ref)
        values(i + 1, p1_ref)
        probs(s0_ref, p0_ref)
        scores(jnp.minimum(i + 3, last), s1_ref)
        return carry

    lax.fori_loop(0, npairs // 2, two_pairs, 0)


def neighbourhood_attention(proj, proj_c, bias, na_w, gate_col):
    bsz, length, _ = proj.shape
    lc = proj_c.shape[1]
    heads = na_w // HEAD_DIM
    rows = length // GRID_W
    seq = lambda off: pl.BlockSpec((1, length, HEAD_DIM), lambda h, b: (b, 0, off + h))
    cseq = lambda off: pl.BlockSpec((1, lc, HEAD_DIM), lambda h, b: (b, 0, off + h))
    return pl.pallas_call(
        functools.partial(_na_kernel, rows=rows),
        name="neighbourhood_attention",
        grid=(heads, bsz),
        in_specs=[seq(0), seq(heads), seq(2 * heads), cseq(heads), cseq(2 * heads),
                  pl.BlockSpec((1, NA_VARIANTS, NA_PAIR, NA_WIN), lambda h, b: (h, 0, 0, 0)),
                  seq(gate_col // HEAD_DIM)],
        out_specs=seq(0),
        out_shape=jax.ShapeDtypeStruct((bsz, length, na_w), BF16),
        scratch_shapes=[pltpu.VMEM((length, 2 * HEAD_DIM), BF16), pltpu.VMEM((lc, 2 * HEAD_DIM), BF16),
                        pltpu.VMEM((NA_PAIR, NA_WIN + lc), F32), pltpu.VMEM((NA_PAIR, NA_WIN + lc), F32),
                        pltpu.VMEM((NA_PAIR, NA_WIN + lc), BF16), pltpu.VMEM((NA_PAIR, NA_WIN + lc), BF16)],
        compiler_params=_params(("parallel", "parallel")),
    )(proj, proj, proj, proj_c, proj_c, bias, proj)


S5_CHUNK = 128
S5_IN_BLOCK = LANES
S5_OUT_BLOCK = 256
S5_SCAN_COLS = 1024


def _s5_kernel(uf_ref, ub_ref, bf_ref, bb_ref, crf_ref, cif_ref, crb_ref, cib_ref, lre_ref, lim_ref,
               yf_ref, yb_ref, buf_ref, state_ref, stage_ref):
    bsz, t_len, width = uf_ref.shape
    n_state = lre_ref.shape[1]
    n_in = width // S5_IN_BLOCK
    blk = n_state // n_in
    rows8 = 2 * bsz

    @pl.when(pl.program_id(0) == 0)
    def _():
        state_ref[...] = jnp.zeros_like(state_ref)

    nlb = n_state // LANES
    per_in = blk // LANES
    ti = lax.broadcasted_iota(jnp.int32, (t_len, t_len), 0)
    tj = lax.broadcasted_iota(jnp.int32, (t_len, t_len), 1)
    rev = jnp.where(ti + tj == t_len - 1, 1.0, 0.0).astype(BF16)
    u_fwd = uf_ref[...].reshape(bsz * t_len, width)
    u_bwd = jnp.concatenate([_dot(rev, ub_ref[b]).astype(BF16) for b in range(bsz)], axis=0)
    for di, (u, w_ref) in enumerate(((u_fwd, bf_ref), (u_bwd, bb_ref))):
        for kb in range(n_in):
            res = _dot(u[:, kb * S5_IN_BLOCK:(kb + 1) * S5_IN_BLOCK], w_ref[kb])
            for b in range(bsz):
                rsel = pl.ds(di * bsz + b, t_len, stride=rows8)
                for c in range(2 * per_in):
                    dst = (c // per_in) * nlb + kb * per_in + c % per_in
                    buf_ref[dst, rsel, :] = res[b * t_len:(b + 1) * t_len, c * LANES:(c + 1) * LANES]

    per = S5_SCAN_COLS // LANES
    for cbk in range(nlb // per):
        blocks = [cbk * per + i for i in range(per)]
        lre = [lre_ref[:, pl.ds(c * LANES, LANES)] for c in blocks]
        lim = [lim_ref[:, pl.ds(c * LANES, LANES)] for c in blocks]

        def step(t, carry, blocks=blocks, lre=lre, lim=lim):
            row = pl.ds(pl.multiple_of(t * rows8, rows8), rows8)
            new = []
            for i, c in enumerate(blocks):
                hre, him = carry[2 * i], carry[2 * i + 1]
                nre = lre[i] * hre - lim[i] * him + buf_ref[c, row, :]
                nim = lre[i] * him + lim[i] * hre + buf_ref[nlb + c, row, :]
                buf_ref[c, row, :] = nre
                buf_ref[nlb + c, row, :] = nim
                new += [nre, nim]
            return tuple(new)

        init = tuple(state_ref[:, pl.ds(part * n_state + c * LANES, LANES)] for c in blocks for part in range(2))
        fin = lax.fori_loop(0, t_len, step, init, unroll=2)
        for i, c in enumerate(blocks):
            state_ref[:, pl.ds(c * LANES, LANES)] = fin[2 * i]
            state_ref[:, pl.ds(n_state + c * LANES, LANES)] = fin[2 * i + 1]

    n_out = width // S5_OUT_BLOCK
    per_out = nlb // n_out
    for di, (y_ref, cr_ref, ci_ref) in enumerate(((yf_ref, crf_ref, cif_ref), (yb_ref, crb_ref, cib_ref))):
        for j in range(n_out):
            for part in range(2):
                for b in range(bsz):
                    rsel = pl.ds(di * bsz + b, t_len, stride=rows8)
                    for c in range(per_out):
                        stage_ref[part, pl.ds(b * t_len, t_len), pl.ds(c * LANES, LANES)] = buf_ref[
                            part * nlb + j * per_out + c, rsel, :].astype(BF16)
            y = _dot(stage_ref[0], cr_ref[j]) + _dot(stage_ref[1], ci_ref[j])
            for b in range(bsz):
                yb = y[b * t_len:(b + 1) * t_len]
                if di == 1:
                    hi = yb.astype(BF16)
                    r1 = yb - hi.astype(F32)
                    mid = r1.astype(BF16)
                    lo = (r1 - mid.astype(F32)).astype(BF16)
                    yb = _dot(rev, hi) + _dot(rev, mid) + _dot(rev, lo)
                y_ref[b, :, pl.ds(j * S5_OUT_BLOCK, S5_OUT_BLOCK)] = yb


def _block_diag(x):
    nblk, g, r, c = x.shape
    eye = jnp.eye(g, dtype=x.dtype)
    return jnp.einsum("kgrc,gh->kgrhc", x, eye).reshape(nblk, g * r, g * c)


def _s5_direction_params(a_re, a_im, log_dt, b_re, b_im, c_re, c_im):
    groups, n_p = a_re.shape
    lam = lax.complex(a_re.astype(F32), a_im.astype(F32))
    dt = jnp.exp(log_dt.astype(F32))[:, None]
    lam_bar = jnp.exp(lam * dt)
    b_bar = ((lam_bar - 1.0) / lam)[..., None] * lax.complex(b_re.astype(F32), b_im.astype(F32))
    gi = S5_IN_BLOCK // SSM_GROUP
    bt = jnp.swapaxes(b_bar, 1, 2).reshape(groups // gi, gi, SSM_GROUP, n_p)
    b_mat = jnp.concatenate([_block_diag(jnp.real(bt)), _block_diag(jnp.imag(bt))], axis=-1)
    go = S5_OUT_BLOCK // SSM_GROUP
    ct = lambda c: jnp.swapaxes(c.astype(F32), 1, 2).reshape(groups // go, go, n_p, SSM_GROUP)
    return (b_mat.astype(BF16), _block_diag(ct(c_re)).astype(BF16), _block_diag(-ct(c_im)).astype(BF16),
            jnp.real(lam_bar).reshape(-1), jnp.imag(lam_bar).reshape(-1))


def s5_scan(u_fwd, u_bwd, fwd, bwd):
    bsz, ltot, width = u_fwd.shape
    t_len = S5_CHUNK
    nc = ltot // t_len
    n_state = fwd[3].shape[0]
    lre = jnp.concatenate([jnp.broadcast_to(fwd[3], (bsz, n_state)), jnp.broadcast_to(bwd[3], (bsz, n_state))])
    lim = jnp.concatenate([jnp.broadcast_to(fwd[4], (bsz, n_state)), jnp.broadcast_to(bwd[4], (bsz, n_state))])
    full = lambda x: pl.BlockSpec(x.shape, lambda c, nd=x.ndim: (0,) * nd)
    fblk = pl.BlockSpec((bsz, t_len, width), lambda c: (0, c, 0))
    bblk = pl.BlockSpec((bsz, t_len, width), lambda c: (0, nc - 1 - c, 0))
    out = jax.ShapeDtypeStruct((bsz, ltot, width), F32)
    consts = (fwd[0], bwd[0], fwd[1], fwd[2], bwd[1], bwd[2], lre, lim)
    return pl.pallas_call(
        _s5_kernel,
        name="s5_scan",
        grid=(nc,),
        in_specs=[fblk, bblk] + [full(x) for x in consts],
        out_specs=[fblk, bblk],
        out_shape=[out, out],
        scratch_shapes=[pltpu.VMEM((2 * n_state // LANES, t_len * 2 * bsz, LANES), F32),
                        pltpu.VMEM((2 * bsz, 2 * n_state), F32),
                        pltpu.VMEM((2, bsz * t_len, n_state // (width // S5_OUT_BLOCK)), BF16)],
        compiler_params=_params(("arbitrary",)),
    )(u_fwd, u_bwd, *consts)


def _odd_out_kernel(na_ref, yf_ref, yb_ref, d_ref, sdg_ref, dsk_ref, wglu_ref, wna_ref, wssm_ref,
                    h_ref, pg_ref, gt_ref, o_ref):
    y = yf_ref[0] + yb_ref[0] + dsk_ref[...] * d_ref[0].astype(F32)
    y = 0.5 * y * (1.0 + jnp.tanh(math.sqrt(2.0 / math.pi) * (y + 0.044715 * (y * y * y))))
    z = y * jax.nn.sigmoid(_dot(y.astype(BF16), wglu_ref[...]))
    s = (z * sdg_ref[0].astype(F32)).astype(BF16)
    out = _dot(na_ref[0], wna_ref[...]) + _dot(s, wssm_ref[...])
    _postnorm_residual(out, h_ref, pg_ref, gt_ref, o_ref)


def odd_out_proj(na, yf, yb, proj, d_col, dg_col, d_skip, w_glu, w_na, w_ssm, h, post_g, gate, ctx_len, tm):
    bsz, length, dm = h.shape
    na_w, ssm_w = na.shape[2], yf.shape[2]
    off = ctx_len // tm
    tok = lambda w, col=0: pl.BlockSpec((1, tm, w), lambda b, i: (b, i, col // w))
    const = lambda r, c: pl.BlockSpec((r, c), lambda b, i: (0, 0))
    return pl.pallas_call(
        _odd_out_kernel,
        name="odd_out_proj",
        grid=(bsz, length // tm),
        in_specs=[tok(na_w),
                  pl.BlockSpec((1, tm, ssm_w), lambda b, i: (b, i + off, 0)),
                  tok(ssm_w),
                  tok(ssm_w, d_col), tok(ssm_w, dg_col),
                  const(1, ssm_w), const(ssm_w, ssm_w), const(na_w, dm), const(ssm_w, dm),
                  tok(dm), const(1, dm),
                  pl.BlockSpec((1, 1, dm), lambda b, i: (b, 0, 0))],
        out_specs=tok(dm),
        out_shape=jax.ShapeDtypeStruct((bsz, length, dm), F32),
        compiler_params=_params(("parallel", "parallel")),
    )(na, yf, yb, proj, proj, d_skip.reshape(1, ssm_w), w_glu, w_na, w_ssm, h, post_g.reshape(1, dm), gate)


def _small_dft(n):
    k = jnp.arange(n, dtype=jnp.int32)
    ang = ((k[:, None] * k[None, :]) % n).astype(F32) * (2.0 * math.pi / n)
    return jnp.concatenate([jnp.cos(ang), jnp.sin(ang)], axis=1).astype(BF16)


def _conv_fourier_layer(h, pre_g, post_g, shift, scale, gate, w5, w_out_a, w_out_b, conv_w, conv_b,
                        ln_g, ln_b, fourier_g, cs_small, tm, tl, tdft, shared_mod):
    bsz, length, _ = h.shape
    gd = fourier_g.shape[1]
    fold = (lambda t: t.reshape(1, bsz * length, t.shape[2])) if shared_mod else (lambda t: t)
    unfold = (lambda t: t.reshape(bsz, length, t.shape[2])) if shared_mod else (lambda t: t)
    vec = (lambda v: v[:1]) if shared_mod else (lambda v: v)
    outs = even_in_proj(fold(h), pre_g, vec(shift), vec(scale), w5, fourier_g, cs_small, tm)
    a, sga, p, q, sgb = [unfold(t) for t in outs]
    a = conv_branch(a, sga, conv_w, conv_b, ln_g, ln_b, tl)
    cmat, smat = dft_matrices(length)
    f = fourier_branch(cmat, smat, p, q, sgb, 1.0 / math.sqrt(length * gd), tdft, tdft)
    return unfold(even_out_proj(fold(a), fold(f), w_out_a, w_out_b, fold(h), post_g, vec(gate), tm))


def kernel(x, c, ctx, c_ctx, pre_g, post_g, ada_w, ada_b, ab_w_in, ab_w_out, conv_w, conv_b, conv_ln_g,
           conv_ln_b, fourier_g, cd_w_in, cd_w_out, na_rpb, s5_a_re, s5_a_im, s5_log_dt, s5_b_re, s5_b_im,
           s5_c_re, s5_c_im, s5_d, s5_w_glu):
    bsz, length, d = x.shape
    lc = ctx.shape[1]
    depth = ada_w.shape[0]
    assert depth == 2 and length % (2 * GRID_W) == 0 and length // GRID_W >= NA_WIN_ROWS + 2
    assert lc % S5_CHUNK == 0 and length % S5_CHUNK == 0

    cond_rows = jnp.zeros((SUBLANES, d), F32).at[:bsz].set(c).at[bsz].set(c_ctx)
    mods = ada_modulation(cond_rows, ada_w, ada_b)

    def mod_vectors(i):
        xs = [mods[i, :bsz, k * d:(k + 1) * d][:, None, :] for k in range(3)]
        cs = [jnp.broadcast_to(mods[i, bsz, k * d:(k + 1) * d], (bsz, 1, d)) for k in range(3)]
        return xs, cs

    (sh_x, sc_x, gt_x), (sh_c, sc_c, gt_c) = mod_vectors(0)
    w_in = ab_w_in[0]
    conv_width = conv_w.shape[2]
    cw = conv_width
    fw = (w_in.shape[1] - 3 * cw) // 2
    w5 = [w_in[:, s:e].astype(BF16) for s, e in
          ((0, cw), (cw, 2 * cw), (2 * cw, 3 * cw), (3 * cw, 3 * cw + fw), (3 * cw + fw, 3 * cw + 2 * fw))]
    w_out_a = ab_w_out[0, :cw].astype(BF16)
    w_out_b = ab_w_out[0, cw:].astype(BF16)
    cs_small = _small_dft(fourier_g.shape[2])
    layer0 = functools.partial(
        _conv_fourier_layer, pre_g=pre_g[0], post_g=post_g[0], w5=w5, w_out_a=w_out_a, w_out_b=w_out_b,
        conv_w=conv_w[0], conv_b=conv_b[0], ln_g=conv_ln_g[0], ln_b=conv_ln_b[0], fourier_g=fourier_g[0],
        cs_small=cs_small)
    h_x = layer0(x, shift=sh_x, scale=sc_x, gate=gt_x, tm=512, tl=256, tdft=1024, shared_mod=False)
    h_c = layer0(ctx, shift=sh_c, scale=sc_c, gate=gt_c, tm=512, tl=lc, tdft=lc, shared_mod=True)

    (sh_x, sc_x, gt_x), (sh_c, sc_c, _) = mod_vectors(1)
    w_in = cd_w_in[0]
    ssm_w = s5_d.shape[1]
    na_w = (w_in.shape[1] - 2 * ssm_w) // 4
    d_col, g_col, dg_col = 3 * na_w, 3 * na_w + ssm_w, 4 * na_w + ssm_w
    w_all = jnp.concatenate(
        [w_in[:, :na_w] * (HEAD_DIM ** -0.5),
         w_in[:, na_w:3 * na_w], w_in[:, 4 * na_w:4 * na_w + ssm_w],
         w_in[:, 3 * na_w:4 * na_w], w_in[:, 4 * na_w + ssm_w:]], axis=1).astype(BF16)
    assert g_col % ODD_TN == 0 and w_all.shape[1] % ODD_TN == 0
    proj_x = odd_in_proj(h_x, pre_g[1], sh_x, sc_x, w_all, g_col, tm=1024)
    proj_c = odd_in_proj(h_c.reshape(1, bsz * lc, d), pre_g[1], sh_c[:1], sc_c[:1], w_all, g_col,
                         tm=bsz * lc).reshape(bsz, lc, -1)

    bias = na_bias_table(na_rpb[0], length // GRID_W)
    na = neighbourhood_attention(proj_x, proj_c, bias, na_w, g_col)

    dirs = [_s5_direction_params(s5_a_re[0, i], s5_a_im[0, i], s5_log_dt[0, i], s5_b_re[0, i], s5_b_im[0, i],
                                 s5_c_re[0, i], s5_c_im[0, i]) for i in range(2)]
    d_x, d_c = proj_x[:, :, d_col:d_col + ssm_w], proj_c[:, :, d_col:d_col + ssm_w]
    y_f, y_b = s5_scan(jnp.concatenate([d_c, d_x], axis=1), jnp.concatenate([d_x, d_c], axis=1), *dirs)

    return odd_out_proj(na, y_f, y_b, proj_x, d_col, dg_col, s5_d[0], s5_w_glu[0].astype(BF16),
                        cd_w_out[0, :na_w].astype(BF16), cd_w_out[0, na_w:].astype(BF16),
                        h_x, post_g[1], gt_x, lc, tm=256)
```

```python
import functools
import math

import jax
import jax.numpy as jnp
from jax import lax
from jax.experimental import pallas as pl
from jax.experimental.pallas import tpu as pltpu

F32 = jnp.float32
BF16 = jnp.bfloat16

EPS = 1e-6
NEG_INF = -1e30

GRID_W = 64
CONV_K = 31
FOURIER_GROUPS = 4
HEAD_DIM = 128
NA_ROWS = 8
NA_COLS = 16
SSM_GROUP = 16
SSM_STATE = 64

LANES = 128
SUBLANES = 8
VMEM_LIMIT = 56 * 1024 * 1024

NA_PAIR = 2 * GRID_W
NA_WIN_ROWS = NA_ROWS + 2
NA_WIN = NA_WIN_ROWS * GRID_W
NA_VARIANTS = 5


def _params(sem, vmem=VMEM_LIMIT):
    return pltpu.CompilerParams(dimension_semantics=sem, vmem_limit_bytes=vmem)


def _silu(x):
    return x * jax.nn.sigmoid(x)


def _rms(x):
    return x * lax.rsqrt(jnp.mean(x * x, axis=-1, keepdims=True) + EPS)


def _dot(a, b):
    return jnp.dot(a, b, preferred_element_type=F32)


def _ada_kernel(c_ref, w_ref, b_ref, o_ref):
    cond = _silu(c_ref[...])
    o_ref[0] = _dot(cond.astype(BF16), w_ref[0].astype(BF16)) + b_ref[0]


def ada_modulation(cond_rows, ada_w, ada_b, tn=1024):
    depth, d, n = ada_w.shape
    rows = cond_rows.shape[0]
    return pl.pallas_call(
        _ada_kernel,
        name="ada_modulation",
        grid=(depth, n // tn),
        in_specs=[
            pl.BlockSpec((rows, d), lambda i, j: (0, 0)),
            pl.BlockSpec((1, d, tn), lambda i, j: (i, 0, j)),
            pl.BlockSpec((1, 1, tn), lambda i, j: (i, 0, j)),
        ],
        out_specs=pl.BlockSpec((1, rows, tn), lambda i, j: (i, 0, j)),
        out_shape=jax.ShapeDtypeStruct((depth, rows, n), F32),
        compiler_params=_params(("parallel", "parallel")),
    )(cond_rows, ada_w, ada_b.reshape(depth, 1, n))


PRENORM_ROWS = 16


def _prenorm_modulate(h_ref, g_ref, sh_ref, sc_ref, xn_ref):
    gain = g_ref[...] * (1.0 + sc_ref[0])
    shift = sh_ref[0]

    def chunk(r, carry):
        rows = pl.ds(pl.multiple_of(r * PRENORM_ROWS, PRENORM_ROWS), PRENORM_ROWS)
        xn_ref[rows, :] = (_rms(h_ref[0, rows, :]) * gain + shift).astype(BF16)
        return carry

    lax.fori_loop(0, xn_ref.shape[0] // PRENORM_ROWS, chunk, 0, unroll=4)


EVEN_PARTS = 5


def _even_in_kernel(h_ref, g_ref, sh_ref, sc_ref, w_ref, fg_ref, cs_ref,
                    a_ref, sga_ref, p_ref, q_ref, sgb_ref, xn_ref):
    @pl.when(pl.program_id(2) == 0)
    def _():
        _prenorm_modulate(h_ref, g_ref, sh_ref, sc_ref, xn_ref)

    gd = a_ref.shape[2]
    acc = _dot(xn_ref[...], w_ref[...])
    part = lambda k: acc[:, k * gd:(k + 1) * gd]
    a_ref[0] = (part(0) * jax.nn.sigmoid(part(1))).astype(BF16)
    sga_ref[0] = _silu(part(2)).astype(BF16)
    bn = _rms(part(3)) * fg_ref[0]
    pq = _dot(bn.astype(BF16), cs_ref[...])
    p_ref[0] = pq[:, :gd].astype(BF16)
    q_ref[0] = pq[:, gd:].astype(BF16)
    sgb_ref[0] = _silu(part(4)).astype(BF16)


def even_in_proj(h, pre_g, shift, scale, w_cat, fourier_g, cs_small, tm):
    bsz, length, d = h.shape
    gd = fourier_g.shape[1]
    width = FOURIER_GROUPS * gd
    ospec = pl.BlockSpec((1, tm, gd), lambda b, i, n: (b, i, n))
    vec = pl.BlockSpec((1, 1, d), lambda b, i, n: (b, 0, 0))
    out = jax.ShapeDtypeStruct((bsz, length, width), BF16)
    return pl.pallas_call(
        _even_in_kernel,
        name="even_in_proj",
        grid=(bsz, length // tm, FOURIER_GROUPS),
        in_specs=[
            pl.BlockSpec((1, tm, d), lambda b, i, n: (b, i, 0)),
            pl.BlockSpec((1, d), lambda b, i, n: (0, 0)),
            vec, vec,
            pl.BlockSpec((d, EVEN_PARTS * gd), lambda b, i, n: (0, n)),
            pl.BlockSpec((1, 1, gd), lambda b, i, n: (n, 0, 0)),
            pl.BlockSpec((gd, 2 * gd), lambda b, i, n: (0, 0)),
        ],
        out_specs=[ospec] * 5,
        out_shape=[out] * 5,
        scratch_shapes=[pltpu.VMEM((tm, d), BF16)],
        compiler_params=_params(("parallel", "parallel", "arbitrary")),
    )(h, pre_g.reshape(1, d), shift, scale, w_cat, fourier_g.reshape(FOURIER_GROUPS, 1, gd), cs_small)


CONV_HALO = 16
CONV_ROWS = 32


def _conv_kernel(a_ref, prev_ref, next_ref, w_ref, cb_ref, lg_ref, lb_ref, sga_ref, o_ref,
                 ext_ref, sh_ref, acc_ref):
    i = pl.program_id(1)
    last = pl.num_programs(1) - 1
    tl, width = acc_ref.shape
    ext_ref[pl.ds(CONV_HALO, tl), :] = a_ref[0].astype(F32)
    ext_ref[pl.ds(0, CONV_HALO), :] = jnp.where(i > 0, prev_ref[0].astype(F32), 0.0)
    ext_ref[pl.ds(CONV_HALO + tl, CONV_HALO), :] = jnp.where(i < last, next_ref[0].astype(F32), 0.0)
    base = CONV_HALO - CONV_K // 2
    span = sh_ref.shape[1]
    for r in range(1, SUBLANES):
        sh_ref[r - 1] = ext_ref[pl.ds(r, span), :]

    def rows(rb, carry):
        r0 = pl.multiple_of(rb * CONV_ROWS, CONV_ROWS)
        for cb in range(width // LANES):
            cols = pl.ds(cb * LANES, LANES)
            acc = jnp.zeros((CONV_ROWS, LANES), F32)
            for k in range(CONV_K):
                q, r = divmod(base + k, SUBLANES)
                rsel = pl.ds(r0 + q * SUBLANES, CONV_ROWS)
                tap = ext_ref[rsel, cols] if r == 0 else sh_ref[r - 1, rsel, cols]
                acc = acc + tap * w_ref[pl.ds(k, 1), cols]
            acc_ref[pl.ds(r0, CONV_ROWS), cols] = acc + cb_ref[:, cols]
        return carry

    lax.fori_loop(0, tl // CONV_ROWS, rows, 0)
    x = acc_ref[...]
    mu = jnp.mean(x, axis=-1, keepdims=True)
    xc = x - mu
    var = jnp.mean(xc * xc, axis=-1, keepdims=True)
    y = xc * lax.rsqrt(var + EPS) * lg_ref[...] + lb_ref[...]
    o_ref[0] = (_silu(y) * sga_ref[0].astype(F32)).astype(BF16)


def conv_branch(a, sga, conv_w, conv_b, ln_g, ln_b, tl):
    bsz, length, width = a.shape
    hb = tl // CONV_HALO
    nh = length // CONV_HALO
    row = pl.BlockSpec((1, width), lambda b, i: (0, 0))
    main = pl.BlockSpec((1, tl, width), lambda b, i: (b, i, 0))
    return pl.pallas_call(
        _conv_kernel,
        name="conv_branch",
        grid=(bsz, length // tl),
        in_specs=[
            main,
            pl.BlockSpec((1, CONV_HALO, width), lambda b, i: (b, jnp.maximum(i * hb - 1, 0), 0)),
            pl.BlockSpec((1, CONV_HALO, width), lambda b, i: (b, jnp.minimum((i + 1) * hb, nh - 1), 0)),
            pl.BlockSpec((CONV_K, width), lambda b, i: (0, 0)),
            row, row, row,
            main,
        ],
        out_specs=main,
        out_shape=jax.ShapeDtypeStruct((bsz, length, width), BF16),
        scratch_shapes=[pltpu.VMEM((tl + 2 * CONV_HALO, width), F32),
                        pltpu.VMEM((SUBLANES - 1, tl + 2 * CONV_HALO - SUBLANES, width), F32),
                        pltpu.VMEM((tl, width), F32)],
        compiler_params=_params(("parallel", "parallel")),
    )(a, a, a, conv_w, conv_b.reshape(1, width), ln_g.reshape(1, width), ln_b.reshape(1, width), sga)


DFT_SPLIT = 64


def _dft_gen_kernel(ca_ref, sa_ref, cb_ref, sb_ref, c_ref, s_ref):
    ca, sa = ca_ref[0], sa_ref[0]
    cb, sb = cb_ref[...], sb_ref[...]
    c_ref[...] = (ca * cb - sa * sb).astype(BF16)
    s_ref[...] = (-(sa * cb + ca * sb)).astype(BF16)


def dft_matrices(length):
    na = length // DFT_SPLIT
    k = jnp.arange(length, dtype=jnp.int32)[None, :]
    ia = jnp.arange(na, dtype=jnp.int32)[:, None]
    ib = jnp.arange(DFT_SPLIT, dtype=jnp.int32)[:, None]
    w = 2.0 * math.pi / length
    ang_a = ((DFT_SPLIT * ia * k) % length).astype(F32) * w
    ang_b = ((ib * k) % length).astype(F32) * w
    ca, sa = jnp.cos(ang_a).reshape(na, 1, length), jnp.sin(ang_a).reshape(na, 1, length)
    cb, sb = jnp.cos(ang_b), jnp.sin(ang_b)
    tab_a = pl.BlockSpec((1, 1, length), lambda a: (a, 0, 0))
    tab_b = pl.BlockSpec((DFT_SPLIT, length), lambda a: (0, 0))
    out = pl.BlockSpec((DFT_SPLIT, length), lambda a: (a, 0))
    shp = jax.ShapeDtypeStruct((length, length), BF16)
    return pl.pallas_call(
        _dft_gen_kernel,
        name="dft_matrices",
        grid=(na,),
        in_specs=[tab_a, tab_a, tab_b, tab_b],
        out_specs=[out, out],
        out_shape=[shp, shp],
        compiler_params=_params(("parallel",)),
    )(ca, sa, cb, sb)


def _dft_kernel(c_ref, s_ref, p_ref, q_ref, sgb_ref, o_ref, acc_ref, *, scale):
    kk = pl.program_id(2)

    @pl.when(kk == 0)
    def _():
        acc_ref[...] = jnp.zeros_like(acc_ref)

    acc_ref[...] += _dot(c_ref[...], p_ref[0]) + _dot(s_ref[...], q_ref[0])

    @pl.when(kk == pl.num_programs(2) - 1)
    def _():
        o_ref[0] = (acc_ref[...] * scale * sgb_ref[0].astype(F32)).astype(BF16)


def fourier_branch(cmat, smat, p, q, sgb, scale, tm, tk):
    bsz, length, width = p.shape
    return pl.pallas_call(
        functools.partial(_dft_kernel, scale=scale),
        name="fourier_branch",
        grid=(bsz, length // tm, length // tk),
        in_specs=[
            pl.BlockSpec((tm, tk), lambda b, m, k: (m, k)),
            pl.BlockSpec((tm, tk), lambda b, m, k: (m, k)),
            pl.BlockSpec((1, tk, width), lambda b, m, k: (b, k, 0)),
            pl.BlockSpec((1, tk, width), lambda b, m, k: (b, k, 0)),
            pl.BlockSpec((1, tm, width), lambda b, m, k: (b, m, 0)),
        ],
        out_specs=pl.BlockSpec((1, tm, width), lambda b, m, k: (b, m, 0)),
        out_shape=jax.ShapeDtypeStruct((bsz, length, width), BF16),
        scratch_shapes=[pltpu.VMEM((tm, width), F32)],
        compiler_params=_params(("parallel", "parallel", "arbitrary")),
    )(cmat, smat, p, q, sgb)


FFT_MINOR = 64
FFT_COLS = 256


def _fft_kernel(p_ref, q_ref, tw_ref, f2_ref, sgb_ref, o_ref, x_ref, y_ref, z_ref, *, scale):
    length, cw = p_ref.shape[1], p_ref.shape[2]
    ncb = cw // LANES
    n_a = FFT_MINOR
    n_m = length // n_a
    for cb in range(ncb):
        cols = pl.ds(cb * LANES, LANES)
        x_ref[cb] = p_ref[0, :, cols].astype(F32)
        x_ref[ncb + cb] = q_ref[0, :, cols].astype(F32)

    def gathered(src_ref, start, count, stride):
        rows = pl.ds(start, count, stride=stride)
        halves = [jnp.concatenate([src_ref[part * ncb + cb, rows, :] for cb in range(ncb)], axis=1)
                  for part in range(2)]
        return jnp.concatenate(halves, axis=0).astype(BF16)

    for a in range(n_a):
        out = _dot(tw_ref[a], gathered(x_ref, a, n_m, n_a))
        for part in range(2):
            for cb in range(ncb):
                y_ref[part * ncb + cb, pl.ds(a * n_m, n_m), :] = out[
                    part * n_m:(part + 1) * n_m, cb * LANES:(cb + 1) * LANES]

    f2 = f2_ref[...]
    for k2 in range(n_m):
        out = _dot(f2, gathered(y_ref, k2, n_a, n_m))
        for cb in range(ncb):
            z_ref[cb, pl.ds(k2, n_a, stride=n_m), :] = out[:, cb * LANES:(cb + 1) * LANES]

    for cb in range(ncb):
        cols = pl.ds(cb * LANES, LANES)
        o_ref[0, :, cols] = (z_ref[cb] * scale * sgb_ref[0, :, cols].astype(F32)).astype(BF16)


def _fft_tables(length):
    n_a = FFT_MINOR
    n_m = length // n_a
    w = 2.0 * math.pi / length
    a = jnp.arange(n_a, dtype=jnp.int32)[:, None, None]
    k2 = jnp.arange(n_m, dtype=jnp.int32)[None, :, None]
    m = jnp.arange(n_m, dtype=jnp.int32)[None, None, :]
    ang = ((k2 * (a + n_a * m)) % length).astype(F32) * w
    ct, st = jnp.cos(ang), jnp.sin(ang)
    tw = jnp.concatenate([jnp.concatenate([ct, -st], axis=2), jnp.concatenate([-st, -ct], axis=2)], axis=1)
    k1 = jnp.arange(n_a, dtype=jnp.int32)
    ang2 = ((k1[:, None] * k1[None, :]) % n_a).astype(F32) * (2.0 * math.pi / n_a)
    f2 = jnp.concatenate([jnp.cos(ang2), jnp.sin(ang2)], axis=1)
    return tw.astype(BF16), f2.astype(BF16)


def fourier_branch_fft(p, q, sgb, scale):
    bsz, length, width = p.shape
    tw, f2 = _fft_tables(length)
    ncb = FFT_COLS // LANES
    tok = pl.BlockSpec((1, length, FFT_COLS), lambda b, n: (b, 0, n))
    return pl.pallas_call(
        functools.partial(_fft_kernel, scale=scale),
        name="fourier_branch_fft",
        grid=(bsz, width // FFT_COLS),
        in_specs=[tok, tok,
                  pl.BlockSpec(tw.shape, lambda b, n: (0, 0, 0)),
                  pl.BlockSpec(f2.shape, lambda b, n: (0, 0)),
                  tok],
        out_specs=tok,
        out_shape=jax.ShapeDtypeStruct((bsz, length, width), BF16),
        scratch_shapes=[pltpu.VMEM((2 * ncb, length, LANES), F32),
                        pltpu.VMEM((2 * ncb, length, LANES), F32),
                        pltpu.VMEM((ncb, length, LANES), F32)],
        compiler_params=_params(("parallel", "parallel")),
    )(p, q, tw, f2, sgb)


def _postnorm_residual(y, h_ref, pg_ref, gt_ref, o_ref):
    o_ref[0] = h_ref[0] + gt_ref[0] * (_rms(y) * pg_ref[...])


def _even_out_kernel(a_ref, b_ref, wa_ref, wb_ref, h_ref, pg_ref, gt_ref, o_ref):
    y = _dot(a_ref[0], wa_ref[...]) + _dot(b_ref[0], wb_ref[...])
    _postnorm_residual(y, h_ref, pg_ref, gt_ref, o_ref)


def even_out_proj(a, b, wa, wb, h, post_g, gate, tm):
    bsz, length, d = h.shape
    width = a.shape[2]
    half = pl.BlockSpec((1, tm, width), lambda bb, i: (bb, i, 0))
    wspec = pl.BlockSpec((width, d), lambda bb, i: (0, 0))
    full = pl.BlockSpec((1, tm, d), lambda bb, i: (bb, i, 0))
    return pl.pallas_call(
        _even_out_kernel,
        name="even_out_proj",
        grid=(bsz, length // tm),
        in_specs=[half, half, wspec, wspec, full,
                  pl.BlockSpec((1, d), lambda bb, i: (0, 0)),
                  pl.BlockSpec((1, 1, d), lambda bb, i: (bb, 0, 0))],
        out_specs=full,
        out_shape=jax.ShapeDtypeStruct((bsz, length, d), F32),
        compiler_params=_params(("parallel", "parallel")),
    )(a, b, wa, wb, h, post_g.reshape(1, d), gate)


ODD_TN = 1024


def _odd_in_kernel(h_ref, g_ref, sh_ref, sc_ref, w_ref, o_ref, xn_ref, *, first_gate_tile):
    n = pl.program_id(2)

    @pl.when(n == 0)
    def _():
        _prenorm_modulate(h_ref, g_ref, sh_ref, sc_ref, xn_ref)

    acc = _dot(xn_ref[...], w_ref[...])

    @pl.when(n < first_gate_tile)
    def _():
        o_ref[0] = acc.astype(BF16)

    @pl.when(n >= first_gate_tile)
    def _():
        o_ref[0] = _silu(acc).astype(BF16)


def odd_in_proj(h, pre_g, shift, scale, w_all, n_plain, tm):
    bsz, length, d = h.shape
    n_all = w_all.shape[1]
    vec = pl.BlockSpec((1, 1, d), lambda b, i, n: (b, 0, 0))
    return pl.pallas_call(
        functools.partial(_odd_in_kernel, first_gate_tile=n_plain // ODD_TN),
        name="odd_in_proj",
        grid=(bsz, length // tm, n_all // ODD_TN),
        in_specs=[pl.BlockSpec((1, tm, d), lambda b, i, n: (b, i, 0)),
                  pl.BlockSpec((1, d), lambda b, i, n: (0, 0)),
                  vec, vec,
                  pl.BlockSpec((d, ODD_TN), lambda b, i, n: (0, n))],
        out_specs=pl.BlockSpec((1, tm, ODD_TN), lambda b, i, n: (b, i, n)),
        out_shape=jax.ShapeDtypeStruct((bsz, length, n_all), BF16),
        scratch_shapes=[pltpu.VMEM((tm, d), BF16)],
        compiler_params=_params(("parallel", "parallel", "arbitrary")),
    )(h, pre_g.reshape(1, d), shift, scale, w_all)


def _na_pair_geometry(variant, rows):
    r0 = {0: 4, 1: 0, 2: 2, 3: rows - 4, 4: rows - 2}[variant]
    ws = min(max(r0 - NA_ROWS // 2, 0), rows - NA_WIN_ROWS)
    return r0, ws


def _na_bias_kernel(rpb_ref, o_ref, *, rows):
    h = pl.program_id(0)
    n_dr, n_dc = 2 * NA_ROWS - 1, 2 * NA_COLS - 1
    qc = lax.broadcasted_iota(jnp.int32, (GRID_W, LANES), 0)
    lane = lax.broadcasted_iota(jnp.int32, (GRID_W, LANES), 1)
    kc = lane % GRID_W
    hi = lane // GRID_W
    diff = kc - qc
    c_start = jnp.clip(qc - NA_COLS // 2, 0, GRID_W - NA_COLS)
    col_ok = jnp.where(kc >= c_start, 1, 0) * jnp.where(kc < c_start + NA_COLS, 1, 0)
    blocks = []
    for dr in range(n_dr):
        val = jnp.full((GRID_W, LANES), NEG_INF, F32)
        for dc in range(n_dc):
            val = jnp.where(diff == dc - (NA_COLS - 1), rpb_ref[(h * n_dr + dr) * n_dc + dc], val)
        blocks.append(jnp.where(col_ok == 1, val, NEG_INF))
    masked = jnp.full((GRID_W, LANES), NEG_INF, F32)
    for variant in range(NA_VARIANTS):
        r0, ws = _na_pair_geometry(variant, rows)
        for ri in range(2):
            r = r0 + ri
            r_start = min(max(r - NA_ROWS // 2, 0), rows - NA_ROWS)
            for wp in range(NA_WIN_ROWS // 2):
                halves = []
                for a in (ws + 2 * wp, ws + 2 * wp + 1):
                    in_window = r_start <= a < r_start + NA_ROWS
                    halves.append(blocks[a - r + NA_ROWS - 1] if in_window else masked)
                o_ref[0, variant, pl.ds(ri * GRID_W, GRID_W), pl.ds(wp * LANES, LANES)] = jnp.where(
                    hi == 1, halves[1], halves[0])


def na_bias_table(rpb, rows):
    heads = rpb.shape[0]
    return pl.pallas_call(
        functools.partial(_na_bias_kernel, rows=rows),
        name="na_bias_table",
        grid=(heads,),
        in_specs=[pl.BlockSpec(memory_space=pltpu.SMEM)],
        out_specs=pl.BlockSpec((1, NA_VARIANTS, NA_PAIR, NA_WIN), lambda h: (h, 0, 0, 0)),
        out_shape=jax.ShapeDtypeStruct((heads, NA_VARIANTS, NA_PAIR, NA_WIN), F32),
        compiler_params=_params(("parallel",)),
    )(rpb.reshape(-1))


def _na_kernel(q_ref, k_ref, v_ref, kc_ref, vc_ref, bias_ref, sg_ref, o_ref,
               vx_ref, vcx_ref, s0_ref, s1_ref, p0_ref, p1_ref, *, rows):
    npairs = rows // 2
    last = npairs - 1
    nt = (((1,), (1,)), ((), ()))
    vx_ref[:, :HEAD_DIM] = v_ref[0]
    vx_ref[:, HEAD_DIM:] = jnp.ones((vx_ref.shape[0], HEAD_DIM), BF16)
    vcx_ref[:, :HEAD_DIM] = vc_ref[0]
    vcx_ref[:, HEAD_DIM:] = jnp.ones((vcx_ref.shape[0], HEAD_DIM), BF16)

    def window(pr):
        ws = jnp.clip(2 * pr - NA_ROWS // 2, 0, rows - NA_WIN_ROWS)
        return pl.ds(pl.multiple_of(ws * GRID_W, NA_PAIR), NA_WIN)

    def qrows(pr):
        return pl.ds(pl.multiple_of(pr * NA_PAIR, NA_PAIR), NA_PAIR)

    def scores(pr, s_ref):
        variant = jnp.where(pr == 0, 1, jnp.where(pr == 1, 2, jnp.where(
            pr == last - 1, 3, jnp.where(pr == last, 4, 0))))
        q = q_ref[0, qrows(pr), :]
        s_ref[:, :NA_WIN] = lax.dot_general(q, k_ref[0, window(pr), :], nt,
                                            preferred_element_type=F32) + bias_ref[0, variant]
        s_ref[:, NA_WIN:] = lax.dot_general(q, kc_ref[0], nt, preferred_element_type=F32)

    def probs(s_ref, p_ref):
        s = s_ref[...]
        p_ref[...] = jnp.exp(s - jnp.max(s, axis=-1, keepdims=True)).astype(BF16)

    def values(pr, p_ref):
        acc = _dot(p_ref[:, :NA_WIN], vx_ref[window(pr), :]) + _dot(p_ref[:, NA_WIN:], vcx_ref[...])
        o = acc[:, :HEAD_DIM] / acc[:, HEAD_DIM:]
        o_ref[0, qrows(pr), :] = (o * sg_ref[0, qrows(pr), :].astype(F32)).astype(BF16)

    scores(0, s0_ref)
    scores(1, s1_ref)
    probs(s0_ref, p0_ref)

    def two_pairs(i2, carry):
        i = 2 * i2
        values(i, p0_ref)
        probs(s1_ref, p1_ref)
        scores(jnp.minimum(i + 2, last), s0_ref)
        values(i + 1, p1_ref)
        probs(s0_ref, p0_ref)
        scores(jnp.minimum(i + 3, last), s1_ref)
        return carry

    lax.fori_loop(0, npairs // 2, two_pairs, 0)


def neighbourhood_attention(proj, proj_c, bias, na_w, gate_col):
    bsz, length, _ = proj.shape
    lc = proj_c.shape[1]
    heads = na_w // HEAD_DIM
    rows = length // GRID_W
    seq = lambda off: pl.BlockSpec((1, length, HEAD_DIM), lambda h, b: (b, 0, off + h))
    cseq = lambda off: pl.BlockSpec((1, lc, HEAD_DIM), lambda h, b: (b, 0, off + h))
    return pl.pallas_call(
        functools.partial(_na_kernel, rows=rows),
        name="neighbourhood_attention",
        grid=(heads, bsz),
        in_specs=[seq(0), seq(heads), seq(2 * heads), cseq(heads), cseq(2 * heads),
                  pl.BlockSpec((1, NA_VARIANTS, NA_PAIR, NA_WIN), lambda h, b: (h, 0, 0, 0)),
                  seq(gate_col // HEAD_DIM)],
        out_specs=seq(0),
        out_shape=jax.ShapeDtypeStruct((bsz, length, na_w), BF16),
        scratch_shapes=[pltpu.VMEM((length, 2 * HEAD_DIM), BF16), pltpu.VMEM((lc, 2 * HEAD_DIM), BF16),
                        pltpu.VMEM((NA_PAIR, NA_WIN + lc), F32), pltpu.VMEM((NA_PAIR, NA_WIN + lc), F32),
                        pltpu.VMEM((NA_PAIR, NA_WIN + lc), BF16), pltpu.VMEM((NA_PAIR, NA_WIN + lc), BF16)],
        compiler_params=_params(("parallel", "parallel")),
    )(proj, proj, proj, proj_c, proj_c, bias, proj)


S5_CHUNK = 128
S5_IN_BLOCK = LANES
S5_OUT_BLOCK = 256
S5_SCAN_COLS = 1024


def _s5_kernel(ufc_ref, ufx_ref, ubc_ref, ubx_ref, bf_ref, bb_ref, crf_ref, cif_ref, crb_ref, cib_ref,
               lre_ref, lim_ref, yf_ref, yb_ref, buf_ref, state_ref, stage_ref, *, ctx_chunks):
    bsz, t_len, width = ufx_ref.shape
    n_state = lre_ref.shape[1]
    in_ctx = pl.program_id(0) < ctx_chunks
    n_in = width // S5_IN_BLOCK
    blk = n_state // n_in
    rows8 = 2 * bsz

    @pl.when(pl.program_id(0) == 0)
    def _():
        state_ref[...] = jnp.zeros_like(state_ref)

    nlb = n_state // LANES
    per_in = blk // LANES
    ti = lax.broadcasted_iota(jnp.int32, (t_len, t_len), 0)
    tj = lax.broadcasted_iota(jnp.int32, (t_len, t_len), 1)
    rev = jnp.where(ti + tj == t_len - 1, 1.0, 0.0).astype(BF16)
    u_fwd = jnp.where(in_ctx, ufc_ref[...], ufx_ref[...]).reshape(bsz * t_len, width)
    u_nat = jnp.where(in_ctx, ubc_ref[...], ubx_ref[...])
    u_bwd = jnp.concatenate([_dot(rev, u_nat[b]).astype(BF16) for b in range(bsz)], axis=0)
    for di, (u, w_ref) in enumerate(((u_fwd, bf_ref), (u_bwd, bb_ref))):
        for kb in range(n_in):
            res = _dot(u[:, kb * S5_IN_BLOCK:(kb + 1) * S5_IN_BLOCK], w_ref[kb])
            for b in range(bsz):
                rsel = pl.ds(di * bsz + b, t_len, stride=rows8)
                for c in range(2 * per_in):
                    dst = (c // per_in) * nlb + kb * per_in + c % per_in
                    buf_ref[dst, rsel, :] = res[b * t_len:(b + 1) * t_len, c * LANES:(c + 1) * LANES]

    per = S5_SCAN_COLS // LANES
    for cbk in range(nlb // per):
        blocks = [cbk * per + i for i in range(per)]
        lre = [lre_ref[:, pl.ds(c * LANES, LANES)] for c in blocks]
        lim = [lim_ref[:, pl.ds(c * LANES, LANES)] for c in blocks]

        def step(t, carry, blocks=blocks, lre=lre, lim=lim):
            row = pl.ds(pl.multiple_of(t * rows8, rows8), rows8)
            new = []
            for i, c in enumerate(blocks):
                hre, him = carry[2 * i], carry[2 * i + 1]
                nre = lre[i] * hre - lim[i] * him + buf_ref[c, row, :]
                nim = lre[i] * him + lim[i] * hre + buf_ref[nlb + c, row, :]
                buf_ref[c, row, :] = nre
                buf_ref[nlb + c, row, :] = nim
                new += [nre, nim]
            return tuple(new)

        init = tuple(state_ref[:, pl.ds(part * n_state + c * LANES, LANES)] for c in blocks for part in range(2))
        fin = lax.fori_loop(0, t_len, step, init, unroll=2)
        for i, c in enumerate(blocks):
            state_ref[:, pl.ds(c * LANES, LANES)] = fin[2 * i]
            state_ref[:, pl.ds(n_state + c * LANES, LANES)] = fin[2 * i + 1]

    n_out = width // S5_OUT_BLOCK
    per_out = nlb // n_out
    for di, (y_ref, cr_ref, ci_ref) in enumerate(((yf_ref, crf_ref, cif_ref), (yb_ref, crb_ref, cib_ref))):
        for j in range(n_out):
            for part in range(2):
                for b in range(bsz):
                    rsel = pl.ds(di * bsz + b, t_len, stride=rows8)
                    for c in range(per_out):
                        stage_ref[part, pl.ds(b * t_len, t_len), pl.ds(c * LANES, LANES)] = buf_ref[
                            part * nlb + j * per_out + c, rsel, :].astype(BF16)
            y = _dot(stage_ref[0], cr_ref[j]) + _dot(stage_ref[1], ci_ref[j])
            for b in range(bsz):
                yb = y[b * t_len:(b + 1) * t_len]
                if di == 1:
                    hi = yb.astype(BF16)
                    r1 = yb - hi.astype(F32)
                    mid = r1.astype(BF16)
                    lo = (r1 - mid.astype(F32)).astype(BF16)
                    yb = _dot(rev, hi) + _dot(rev, mid) + _dot(rev, lo)
                y_ref[b, :, pl.ds(j * S5_OUT_BLOCK, S5_OUT_BLOCK)] = yb


def _block_diag(x):
    nblk, g, r, c = x.shape
    eye = jnp.eye(g, dtype=x.dtype)
    return jnp.einsum("kgrc,gh->kgrhc", x, eye).reshape(nblk, g * r, g * c)


def _s5_direction_params(a_re, a_im, log_dt, b_re, b_im, c_re, c_im):
    groups, n_p = a_re.shape
    lam = lax.complex(a_re.astype(F32), a_im.astype(F32))
    dt = jnp.exp(log_dt.astype(F32))[:, None]
    lam_bar = jnp.exp(lam * dt)
    b_bar = ((lam_bar - 1.0) / lam)[..., None] * lax.complex(b_re.astype(F32), b_im.astype(F32))
    gi = S5_IN_BLOCK // SSM_GROUP
    bt = jnp.swapaxes(b_bar, 1, 2).reshape(groups // gi, gi, SSM_GROUP, n_p)
    b_mat = jnp.concatenate([_block_diag(jnp.real(bt)), _block_diag(jnp.imag(bt))], axis=-1)
    go = S5_OUT_BLOCK // SSM_GROUP
    ct = lambda c: jnp.swapaxes(c.astype(F32), 1, 2).reshape(groups // go, go, n_p, SSM_GROUP)
    return (b_mat.astype(BF16), _block_diag(ct(c_re)).astype(BF16), _block_diag(-ct(c_im)).astype(BF16),
            jnp.real(lam_bar).reshape(-1), jnp.imag(lam_bar).reshape(-1))


def s5_scan(proj_c, proj_x, d_col, width, fwd, bwd):
    bsz, lc, _ = proj_c.shape
    length = proj_x.shape[1]
    t_len = S5_CHUNK
    ncc, ncx = lc // t_len, length // t_len
    nc = ncc + ncx
    col = d_col // width
    n_state = fwd[3].shape[0]
    lre = jnp.concatenate([jnp.broadcast_to(fwd[3], (bsz, n_state)), jnp.broadcast_to(bwd[3], (bsz, n_state))])
    lim = jnp.concatenate([jnp.broadcast_to(fwd[4], (bsz, n_state)), jnp.broadcast_to(bwd[4], (bsz, n_state))])
    full = lambda x: pl.BlockSpec(x.shape, lambda c, nd=x.ndim: (0,) * nd)
    blk = lambda index: pl.BlockSpec((bsz, t_len, width), index)
    in_blocks = [blk(lambda c: (0, jnp.minimum(c, ncc - 1), col)),
                 blk(lambda c: (0, jnp.maximum(c - ncc, 0), col)),
                 blk(lambda c: (0, jnp.maximum(ncc - 1 - c, 0), col)),
                 blk(lambda c: (0, jnp.minimum(nc - 1 - c, ncx - 1), col))]
    out = jax.ShapeDtypeStruct((bsz, lc + length, width), F32)
    consts = (fwd[0], bwd[0], fwd[1], fwd[2], bwd[1], bwd[2], lre, lim)
    return pl.pallas_call(
        functools.partial(_s5_kernel, ctx_chunks=ncc),
        name="s5_scan",
        grid=(nc,),
        in_specs=in_blocks + [full(x) for x in consts],
        out_specs=[blk(lambda c: (0, c, 0)), blk(lambda c: (0, nc - 1 - c, 0))],
        out_shape=[out, out],
        scratch_shapes=[pltpu.VMEM((2 * n_state // LANES, t_len * 2 * bsz, LANES), F32),
                        pltpu.VMEM((2 * bsz, 2 * n_state), F32),
                        pltpu.VMEM((2, bsz * t_len, n_state // (width // S5_OUT_BLOCK)), BF16)],
        compiler_params=_params(("arbitrary",)),
    )(proj_c, proj_x, proj_c, proj_x, *consts)


def _odd_out_kernel(na_ref, yf_ref, yb_ref, d_ref, sdg_ref, dsk_ref, wglu_ref, wna_ref, wssm_ref,
                    h_ref, pg_ref, gt_ref, o_ref):
    y = yf_ref[0] + yb_ref[0] + dsk_ref[...] * d_ref[0].astype(F32)
    y = 0.5 * y * (1.0 + jnp.tanh(math.sqrt(2.0 / math.pi) * (y + 0.044715 * (y * y * y))))
    z = y * jax.nn.sigmoid(_dot(y.astype(BF16), wglu_ref[...]))
    s = (z * sdg_ref[0].astype(F32)).astype(BF16)
    out = _dot(na_ref[0], wna_ref[...]) + _dot(s, wssm_ref[...])
    _postnorm_residual(out, h_ref, pg_ref, gt_ref, o_ref)


def odd_out_proj(na, yf, yb, proj, d_col, dg_col, d_skip, w_glu, w_na, w_ssm, h, post_g, gate, ctx_len, tm):
    bsz, length, dm = h.shape
    na_w, ssm_w = na.shape[2], yf.shape[2]
    off = ctx_len // tm
    tok = lambda w, col=0: pl.BlockSpec((1, tm, w), lambda b, i: (b, i, col // w))
    const = lambda r, c: pl.BlockSpec((r, c), lambda b, i: (0, 0))
    return pl.pallas_call(
        _odd_out_kernel,
        name="odd_out_proj",
        grid=(bsz, length // tm),
        in_specs=[tok(na_w),
                  pl.BlockSpec((1, tm, ssm_w), lambda b, i: (b, i + off, 0)),
                  tok(ssm_w),
                  tok(ssm_w, d_col), tok(ssm_w, dg_col),
                  const(1, ssm_w), const(ssm_w, ssm_w), const(na_w, dm), const(ssm_w, dm),
                  tok(dm), const(1, dm),
                  pl.BlockSpec((1, 1, dm), lambda b, i: (b, 0, 0))],
        out_specs=tok(dm),
        out_shape=jax.ShapeDtypeStruct((bsz, length, dm), F32),
        compiler_params=_params(("parallel", "parallel")),
    )(na, yf, yb, proj, proj, d_skip.reshape(1, ssm_w), w_glu, w_na, w_ssm, h, post_g.reshape(1, dm), gate)


def _small_dft(n):
    k = jnp.arange(n, dtype=jnp.int32)
    ang = ((k[:, None] * k[None, :]) % n).astype(F32) * (2.0 * math.pi / n)
    return jnp.concatenate([jnp.cos(ang), jnp.sin(ang)], axis=1).astype(BF16)


def _conv_fourier_layer(h, pre_g, post_g, shift, scale, gate, w5, w_out_a, w_out_b, conv_w, conv_b,
                        ln_g, ln_b, fourier_g, cs_small, tm, tl, tdft, shared_mod):
    bsz, length, _ = h.shape
    gd = fourier_g.shape[1]
    fold = (lambda t: t.reshape(1, bsz * length, t.shape[2])) if shared_mod else (lambda t: t)
    unfold = (lambda t: t.reshape(bsz, length, t.shape[2])) if shared_mod else (lambda t: t)
    vec = (lambda v: v[:1]) if shared_mod else (lambda v: v)
    outs = even_in_proj(fold(h), pre_g, vec(shift), vec(scale), w5, fourier_g, cs_small, tm)
    a, sga, p, q, sgb = [unfold(t) for t in outs]
    a = conv_branch(a, sga, conv_w, conv_b, ln_g, ln_b, tl)
    ortho = 1.0 / math.sqrt(length * gd)
    if (length // FFT_MINOR) % SUBLANES == 0:
        f = fourier_branch_fft(p, q, sgb, ortho)
    else:
        cmat, smat = dft_matrices(length)
        f = fourier_branch(cmat, smat, p, q, sgb, ortho, tdft, tdft)
    return unfold(even_out_proj(fold(a), fold(f), w_out_a, w_out_b, fold(h), post_g, vec(gate), min(tm, 512)))


def kernel(x, c, ctx, c_ctx, pre_g, post_g, ada_w, ada_b, ab_w_in, ab_w_out, conv_w, conv_b, conv_ln_g,
           conv_ln_b, fourier_g, cd_w_in, cd_w_out, na_rpb, s5_a_re, s5_a_im, s5_log_dt, s5_b_re, s5_b_im,
           s5_c_re, s5_c_im, s5_d, s5_w_glu):
    bsz, length, d = x.shape
    lc = ctx.shape[1]
    depth = ada_w.shape[0]
    assert depth == 2 and length % (2 * GRID_W) == 0 and length // GRID_W >= NA_WIN_ROWS + 2
    assert lc % S5_CHUNK == 0 and length % S5_CHUNK == 0

    cond_rows = jnp.zeros((SUBLANES, d), F32).at[:bsz].set(c).at[bsz].set(c_ctx)
    mods = ada_modulation(cond_rows, ada_w, ada_b)

    def mod_vectors(i):
        xs = [mods[i, :bsz, k * d:(k + 1) * d][:, None, :] for k in range(3)]
        cs = [jnp.broadcast_to(mods[i, bsz, k * d:(k + 1) * d], (bsz, 1, d)) for k in range(3)]
        return xs, cs

    (sh_x, sc_x, gt_x), (sh_c, sc_c, gt_c) = mod_vectors(0)
    w_in = ab_w_in[0]
    conv_width = conv_w.shape[2]
    cw = conv_width
    fw = (w_in.shape[1] - 3 * cw) // 2
    assert cw == fw
    gd = fourier_g.shape[2]
    w5 = jnp.stack([w_in[:, k * cw:(k + 1) * cw].reshape(d, FOURIER_GROUPS, gd) for k in range(EVEN_PARTS)],
                   axis=2).reshape(d, EVEN_PARTS * cw).astype(BF16)
    w_out_a = ab_w_out[0, :cw].astype(BF16)
    w_out_b = ab_w_out[0, cw:].astype(BF16)
    cs_small = _small_dft(fourier_g.shape[2])
    layer0 = functools.partial(
        _conv_fourier_layer, pre_g=pre_g[0], post_g=post_g[0], w5=w5, w_out_a=w_out_a, w_out_b=w_out_b,
        conv_w=conv_w[0], conv_b=conv_b[0], ln_g=conv_ln_g[0], ln_b=conv_ln_b[0], fourier_g=fourier_g[0],
        cs_small=cs_small)
    h_x = layer0(x, shift=sh_x, scale=sc_x, gate=gt_x, tm=1024, tl=256, tdft=1024, shared_mod=False)
    h_c = layer0(ctx, shift=sh_c, scale=sc_c, gate=gt_c, tm=min(1024, bsz * lc), tl=lc, tdft=lc, shared_mod=True)

    (sh_x, sc_x, gt_x), (sh_c, sc_c, _) = mod_vectors(1)
    w_in = cd_w_in[0]
    ssm_w = s5_d.shape[1]
    na_w = (w_in.shape[1] - 2 * ssm_w) // 4
    d_col, g_col, dg_col = 3 * na_w, 3 * na_w + ssm_w, 4 * na_w + ssm_w
    w_all = jnp.concatenate(
        [w_in[:, :na_w] * (HEAD_DIM ** -0.5),
         w_in[:, na_w:3 * na_w], w_in[:, 4 * na_w:4 * na_w + ssm_w],
         w_in[:, 3 * na_w:4 * na_w], w_in[:, 4 * na_w + ssm_w:]], axis=1).astype(BF16)
    assert g_col % ODD_TN == 0 and w_all.shape[1] % ODD_TN == 0
    proj_x = odd_in_proj(h_x, pre_g[1], sh_x, sc_x, w_all, g_col, tm=1024)
    proj_c = odd_in_proj(h_c.reshape(1, bsz * lc, d), pre_g[1], sh_c[:1], sc_c[:1], w_all, g_col,
                         tm=bsz * lc).reshape(bsz, lc, -1)

    bias = na_bias_table(na_rpb[0], length // GRID_W)
    na = neighbourhood_attention(proj_x, proj_c, bias, na_w, g_col)

    dirs = [_s5_direction_params(s5_a_re[0, i], s5_a_im[0, i], s5_log_dt[0, i], s5_b_re[0, i], s5_b_im[0, i],
                                 s5_c_re[0, i], s5_c_im[0, i]) for i in range(2)]
    y_f, y_b = s5_scan(proj_c, proj_x, d_col, ssm_w, *dirs)

    return odd_out_proj(na, y_f, y_b, proj_x, d_col, dg_col, s5_d[0], s5_w_glu[0].astype(BF16),
                        cd_w_out[0, :na_w].astype(BF16), cd_w_out[0, na_w:].astype(BF16),
                        h_x, post_g[1], gt_x, lc, tm=256)
```

```python
import functools
import math

import jax
import jax.numpy as jnp
from jax import lax
from jax.experimental import pallas as pl
from jax.experimental.pallas import tpu as pltpu

F32 = jnp.float32
BF16 = jnp.bfloat16

EPS = 1e-6
NEG_INF = -1e30

GRID_W = 64
CONV_K = 31
FOURIER_GROUPS = 4
HEAD_DIM = 128
NA_ROWS = 8
NA_COLS = 16
SSM_GROUP = 16
SSM_STATE = 64

LANES = 128
SUBLANES = 8
VMEM_LIMIT = 56 * 1024 * 1024

NA_PAIR = 2 * GRID_W
NA_WIN_ROWS = NA_ROWS + 2
NA_WIN = NA_WIN_ROWS * GRID_W
NA_VARIANTS = 5


def _params(sem, vmem=VMEM_LIMIT):
    return pltpu.CompilerParams(dimension_semantics=sem, vmem_limit_bytes=vmem)


def _silu(x):
    return x * jax.nn.sigmoid(x)


def _rms(x):
    return x * lax.rsqrt(jnp.mean(x * x, axis=-1, keepdims=True) + EPS)


def _dot(a, b):
    return jnp.dot(a, b, preferred_element_type=F32)


def _ada_kernel(c_ref, w_ref, b_ref, o_ref):
    cond = _silu(c_ref[...])
    o_ref[0] = _dot(cond.astype(BF16), w_ref[0].astype(BF16)) + b_ref[0]


def ada_modulation(cond_rows, ada_w, ada_b, tn=1024):
    depth, d, n = ada_w.shape
    rows = cond_rows.shape[0]
    return pl.pallas_call(
        _ada_kernel,
        name="ada_modulation",
        grid=(depth, n // tn),
        in_specs=[
            pl.BlockSpec((rows, d), lambda i, j: (0, 0)),
            pl.BlockSpec((1, d, tn), lambda i, j: (i, 0, j)),
            pl.BlockSpec((1, 1, tn), lambda i, j: (i, 0, j)),
        ],
        out_specs=pl.BlockSpec((1, rows, tn), lambda i, j: (i, 0, j)),
        out_shape=jax.ShapeDtypeStruct((depth, rows, n), F32),
        compiler_params=_params(("parallel", "parallel")),
    )(cond_rows, ada_w, ada_b.reshape(depth, 1, n))


PRENORM_ROWS = 16


def _prenorm_modulate(h_ref, g_ref, sh_ref, sc_ref, xn_ref):
    gain = g_ref[...] * (1.0 + sc_ref[0])
    shift = sh_ref[0]

    def chunk(r, carry):
        rows = pl.ds(pl.multiple_of(r * PRENORM_ROWS, PRENORM_ROWS), PRENORM_ROWS)
        xn_ref[rows, :] = (_rms(h_ref[0, rows, :]) * gain + shift).astype(BF16)
        return carry

    lax.fori_loop(0, xn_ref.shape[0] // PRENORM_ROWS, chunk, 0, unroll=4)


EVEN_PARTS = 5


def _even_in_kernel(h_ref, g_ref, sh_ref, sc_ref, w0_ref, w1_ref, w2_ref, w3_ref, w4_ref, fg_ref, cs_ref,
                    a_ref, sga_ref, p_ref, q_ref, sgb_ref, xn_ref):
    @pl.when(pl.program_id(2) == 0)
    def _():
        _prenorm_modulate(h_ref, g_ref, sh_ref, sc_ref, xn_ref)

    gd = a_ref.shape[2]
    w = jnp.concatenate([w0_ref[...], w1_ref[...], w2_ref[...], w3_ref[...], w4_ref[...]], axis=1)
    acc = _dot(xn_ref[...], w)
    part = lambda k: acc[:, k * gd:(k + 1) * gd]
    a_ref[0] = (part(0) * jax.nn.sigmoid(part(1))).astype(BF16)
    sga_ref[0] = _silu(part(2)).astype(BF16)
    bn = _rms(part(3)) * fg_ref[0]
    pq = _dot(bn.astype(BF16), cs_ref[...])
    p_ref[0] = pq[:, :gd].astype(BF16)
    q_ref[0] = pq[:, gd:].astype(BF16)
    sgb_ref[0] = _silu(part(4)).astype(BF16)


def even_in_proj(h, pre_g, shift, scale, w_in, fourier_g, cs_small, tm):
    bsz, length, d = h.shape
    gd = fourier_g.shape[1]
    width = FOURIER_GROUPS * gd
    wspecs = [pl.BlockSpec((d, gd), lambda b, i, n, k=k: (0, k * FOURIER_GROUPS + n)) for k in range(EVEN_PARTS)]
    ospec = pl.BlockSpec((1, tm, gd), lambda b, i, n: (b, i, n))
    vec = pl.BlockSpec((1, 1, d), lambda b, i, n: (b, 0, 0))
    out = jax.ShapeDtypeStruct((bsz, length, width), BF16)
    return pl.pallas_call(
        _even_in_kernel,
        name="even_in_proj",
        grid=(bsz, length // tm, FOURIER_GROUPS),
        in_specs=[
            pl.BlockSpec((1, tm, d), lambda b, i, n: (b, i, 0)),
            pl.BlockSpec((1, d), lambda b, i, n: (0, 0)),
            vec, vec,
            *wspecs,
            pl.BlockSpec((1, 1, gd), lambda b, i, n: (n, 0, 0)),
            pl.BlockSpec((gd, 2 * gd), lambda b, i, n: (0, 0)),
        ],
        out_specs=[ospec] * 5,
        out_shape=[out] * 5,
        scratch_shapes=[pltpu.VMEM((tm, d), BF16)],
        compiler_params=_params(("parallel", "parallel", "arbitrary")),
    )(h, pre_g.reshape(1, d), shift, scale, *([w_in] * EVEN_PARTS),
      fourier_g.reshape(FOURIER_GROUPS, 1, gd), cs_small)


CONV_HALO = 16
CONV_ROWS = 32


def _conv_kernel(a_ref, prev_ref, next_ref, w_ref, cb_ref, lg_ref, lb_ref, sga_ref, o_ref,
                 ext_ref, sh_ref, acc_ref):
    i = pl.program_id(1)
    last = pl.num_programs(1) - 1
    tl, width = acc_ref.shape
    ext_ref[pl.ds(CONV_HALO, tl), :] = a_ref[0].astype(F32)
    ext_ref[pl.ds(0, CONV_HALO), :] = jnp.where(i > 0, prev_ref[0].astype(F32), 0.0)
    ext_ref[pl.ds(CONV_HALO + tl, CONV_HALO), :] = jnp.where(i < last, next_ref[0].astype(F32), 0.0)
    base = CONV_HALO - CONV_K // 2
    span = sh_ref.shape[1]
    for r in range(1, SUBLANES):
        sh_ref[r - 1] = ext_ref[pl.ds(r, span), :]

    for cb in range(width // LANES):
        cols = pl.ds(cb * LANES, LANES)
        taps = [jnp.broadcast_to(w_ref[pl.ds(k, 1), cols], (SUBLANES, LANES)) for k in range(CONV_K)]
        bias = jnp.broadcast_to(cb_ref[:, cols], (SUBLANES, LANES))

        def rows(rb, carry, cols=cols, taps=taps, bias=bias):
            r0 = pl.multiple_of(rb * CONV_ROWS, CONV_ROWS)
            accs = [bias] * (CONV_ROWS // SUBLANES)
            for k in range(CONV_K):
                q, r = divmod(base + k, SUBLANES)
                src = ext_ref if r == 0 else sh_ref.at[r - 1]
                for j in range(len(accs)):
                    accs[j] = accs[j] + src[pl.ds(r0 + (q + j) * SUBLANES, SUBLANES), cols] * taps[k]
            for j, acc in enumerate(accs):
                acc_ref[pl.ds(r0 + j * SUBLANES, SUBLANES), cols] = acc
            return carry

        lax.fori_loop(0, tl // CONV_ROWS, rows, 0, unroll=2)
    x = acc_ref[...]
    mu = jnp.mean(x, axis=-1, keepdims=True)
    xc = x - mu
    var = jnp.mean(xc * xc, axis=-1, keepdims=True)
    y = xc * lax.rsqrt(var + EPS) * lg_ref[...] + lb_ref[...]
    o_ref[0] = (_silu(y) * sga_ref[0].astype(F32)).astype(BF16)


def conv_branch(a, sga, conv_w, conv_b, ln_g, ln_b, tl):
    bsz, length, width = a.shape
    hb = tl // CONV_HALO
    nh = length // CONV_HALO
    row = pl.BlockSpec((1, width), lambda b, i: (0, 0))
    main = pl.BlockSpec((1, tl, width), lambda b, i: (b, i, 0))
    return pl.pallas_call(
        _conv_kernel,
        name="conv_branch",
        grid=(bsz, length // tl),
        in_specs=[
            main,
            pl.BlockSpec((1, CONV_HALO, width), lambda b, i: (b, jnp.maximum(i * hb - 1, 0), 0)),
            pl.BlockSpec((1, CONV_HALO, width), lambda b, i: (b, jnp.minimum((i + 1) * hb, nh - 1), 0)),
            pl.BlockSpec((CONV_K, width), lambda b, i: (0, 0)),
            row, row, row,
            main,
        ],
        out_specs=main,
        out_shape=jax.ShapeDtypeStruct((bsz, length, width), BF16),
        scratch_shapes=[pltpu.VMEM((tl + 2 * CONV_HALO, width), F32),
                        pltpu.VMEM((SUBLANES - 1, tl + 2 * CONV_HALO - SUBLANES, width), F32),
                        pltpu.VMEM((tl, width), F32)],
        compiler_params=_params(("parallel", "parallel")),
    )(a, a, a, conv_w, conv_b.reshape(1, width), ln_g.reshape(1, width), ln_b.reshape(1, width), sga)


DFT_SPLIT = 64


def _dft_gen_kernel(ca_ref, sa_ref, cb_ref, sb_ref, c_ref, s_ref):
    ca, sa = ca_ref[0], sa_ref[0]
    cb, sb = cb_ref[...], sb_ref[...]
    c_ref[...] = (ca * cb - sa * sb).astype(BF16)
    s_ref[...] = (-(sa * cb + ca * sb)).astype(BF16)


def dft_matrices(length):
    na = length // DFT_SPLIT
    k = jnp.arange(length, dtype=jnp.int32)[None, :]
    ia = jnp.arange(na, dtype=jnp.int32)[:, None]
    ib = jnp.arange(DFT_SPLIT, dtype=jnp.int32)[:, None]
    w = 2.0 * math.pi / length
    ang_a = ((DFT_SPLIT * ia * k) % length).astype(F32) * w
    ang_b = ((ib * k) % length).astype(F32) * w
    ca, sa = jnp.cos(ang_a).reshape(na, 1, length), jnp.sin(ang_a).reshape(na, 1, length)
    cb, sb = jnp.cos(ang_b), jnp.sin(ang_b)
    tab_a = pl.BlockSpec((1, 1, length), lambda a: (a, 0, 0))
    tab_b = pl.BlockSpec((DFT_SPLIT, length), lambda a: (0, 0))
    out = pl.BlockSpec((DFT_SPLIT, length), lambda a: (a, 0))
    shp = jax.ShapeDtypeStruct((length, length), BF16)
    return pl.pallas_call(
        _dft_gen_kernel,
        name="dft_matrices",
        grid=(na,),
        in_specs=[tab_a, tab_a, tab_b, tab_b],
        out_specs=[out, out],
        out_shape=[shp, shp],
        compiler_params=_params(("parallel",)),
    )(ca, sa, cb, sb)


def _dft_kernel(c_ref, s_ref, p_ref, q_ref, sgb_ref, o_ref, acc_ref, *, scale):
    kk = pl.program_id(2)

    @pl.when(kk == 0)
    def _():
        acc_ref[...] = jnp.zeros_like(acc_ref)

    acc_ref[...] += _dot(c_ref[...], p_ref[0]) + _dot(s_ref[...], q_ref[0])

    @pl.when(kk == pl.num_programs(2) - 1)
    def _():
        o_ref[0] = (acc_ref[...] * scale * sgb_ref[0].astype(F32)).astype(BF16)


def fourier_branch(cmat, smat, p, q, sgb, scale, tm, tk):
    bsz, length, width = p.shape
    return pl.pallas_call(
        functools.partial(_dft_kernel, scale=scale),
        name="fourier_branch",
        grid=(bsz, length // tm, length // tk),
        in_specs=[
            pl.BlockSpec((tm, tk), lambda b, m, k: (m, k)),
            pl.BlockSpec((tm, tk), lambda b, m, k: (m, k)),
            pl.BlockSpec((1, tk, width), lambda b, m, k: (b, k, 0)),
            pl.BlockSpec((1, tk, width), lambda b, m, k: (b, k, 0)),
            pl.BlockSpec((1, tm, width), lambda b, m, k: (b, m, 0)),
        ],
        out_specs=pl.BlockSpec((1, tm, width), lambda b, m, k: (b, m, 0)),
        out_shape=jax.ShapeDtypeStruct((bsz, length, width), BF16),
        scratch_shapes=[pltpu.VMEM((tm, width), F32)],
        compiler_params=_params(("parallel", "parallel", "arbitrary")),
    )(cmat, smat, p, q, sgb)


FFT_MINOR = 64
FFT_COLS = 256


def _fft_kernel(p_ref, q_ref, k1a_ref, k1b_ref, twr_ref, twi_ref, hre_ref, him_ref, sgb_ref, o_ref,
                x_ref, y_ref, z_ref, *, scale):
    length, cw = p_ref.shape[1], p_ref.shape[2]
    ncb = cw // LANES
    n_m = length // FFT_MINOR
    n_a1 = FFT_MINOR // SUBLANES
    n_kb = n_m // SUBLANES
    slab = n_m * SUBLANES
    piece = SUBLANES * SUBLANES
    for cb in range(ncb):
        cols = pl.ds(cb * LANES, LANES)
        x_ref[cb] = p_ref[0, :, cols].astype(F32).reshape(n_m, FFT_MINOR, LANES)
        x_ref[ncb + cb] = q_ref[0, :, cols].astype(F32).reshape(n_m, FFT_MINOR, LANES)

    def slab_of(part, a1):
        tiles = [x_ref[part * ncb + cb, :, pl.ds(a1 * SUBLANES, SUBLANES), :].reshape(slab, LANES)
                 for cb in range(ncb)]
        return jnp.concatenate(tiles, axis=1).astype(BF16)

    for a1 in range(n_a1):
        y = _dot(k1a_ref[...], slab_of(0, a1)) + _dot(k1b_ref[...], slab_of(1, a1))
        tr, ti = twr_ref[a1], twi_ref[a1]
        for cb in range(ncb):
            yre = y[:slab, cb * LANES:(cb + 1) * LANES]
            yim = y[slab:, cb * LANES:(cb + 1) * LANES]
            zre = (tr * yre - ti * yim).astype(BF16)
            zim = (tr * yim + ti * yre).astype(BF16)
            for kb in range(n_kb):
                dst = (pl.ds(a1 * piece, piece), pl.ds(cb * LANES, LANES))
                y_ref[(kb, 0) + dst] = zre[kb * piece:(kb + 1) * piece]
                y_ref[(kb, 1) + dst] = zim[kb * piece:(kb + 1) * piece]

    for kb in range(n_kb):
        out = _dot(hre_ref[...], y_ref[kb, 0]) + _dot(him_ref[...], y_ref[kb, 1])
        z_ref[:, kb] = out.reshape(FFT_MINOR, SUBLANES, cw)

    o_ref[0] = (z_ref[...].reshape(length, cw) * scale * sgb_ref[0].astype(F32)).astype(BF16)


def _fft_tables(length):
    n_a, s = FFT_MINOR, SUBLANES
    n_m = length // n_a
    n_a1 = n_a // s
    eye = jnp.eye(s, dtype=F32)
    idx = lambda n: jnp.arange(n, dtype=jnp.int32)
    ang_m = ((idx(n_m)[:, None] * idx(n_m)[None, :]) % n_m).astype(F32) * (2.0 * math.pi / n_m)
    cm, sm = jnp.kron(jnp.cos(ang_m), eye), jnp.kron(jnp.sin(ang_m), eye)
    k1a = jnp.concatenate([cm, -sm], axis=0)
    k1b = jnp.concatenate([-sm, -cm], axis=0)
    a_full = (s * idx(n_a1)[:, None, None] + idx(s)[None, None, :])
    ang_t = ((idx(n_m)[None, :, None] * a_full) % length).astype(F32) * (2.0 * math.pi / length)
    rep = lambda t: jnp.broadcast_to(t.reshape(n_a1, n_m * s, 1), (n_a1, n_m * s, LANES))
    twr, twi = rep(jnp.cos(ang_t)), rep(-jnp.sin(ang_t))
    ang_a = ((idx(n_a)[:, None, None] * a_full.reshape(1, n_a1, s)) % n_a).astype(F32) * (2.0 * math.pi / n_a)
    spread = lambda t: jnp.einsum("kas,pq->kpaqs", t, eye).reshape(n_a * s, n_a1 * s * s)
    hre, him = spread(jnp.cos(ang_a)), spread(jnp.sin(ang_a))
    return [t.astype(BF16) for t in (k1a, k1b)] + [twr, twi] + [t.astype(BF16) for t in (hre, him)]


def fourier_branch_fft(p, q, sgb, scale):
    bsz, length, width = p.shape
    tables = _fft_tables(length)
    n_m = length // FFT_MINOR
    tok = pl.BlockSpec((1, length, FFT_COLS), lambda b, n: (b, 0, n))
    const = lambda t: pl.BlockSpec(t.shape, lambda b, n, nd=t.ndim: (0,) * nd)
    return pl.pallas_call(
        functools.partial(_fft_kernel, scale=scale),
        name="fourier_branch_fft",
        grid=(bsz, width // FFT_COLS),
        in_specs=[tok, tok] + [const(t) for t in tables] + [tok],
        out_specs=tok,
        out_shape=jax.ShapeDtypeStruct((bsz, length, width), BF16),
        scratch_shapes=[pltpu.VMEM((2 * FFT_COLS // LANES, n_m, FFT_MINOR, LANES), F32),
                        pltpu.VMEM((n_m // SUBLANES, 2, FFT_MINOR * SUBLANES, FFT_COLS), BF16),
                        pltpu.VMEM((FFT_MINOR, n_m // SUBLANES, SUBLANES, FFT_COLS), F32)],
        compiler_params=_params(("parallel", "parallel")),
    )(p, q, *tables, sgb)


def _postnorm_residual(y, h_ref, pg_ref, gt_ref, o_ref):
    o_ref[0] = h_ref[0] + gt_ref[0] * (_rms(y) * pg_ref[...])


def _even_out_kernel(a_ref, b_ref, wa_ref, wb_ref, h_ref, pg_ref, gt_ref, o_ref):
    y = _dot(a_ref[0], wa_ref[...]) + _dot(b_ref[0], wb_ref[...])
    _postnorm_residual(y, h_ref, pg_ref, gt_ref, o_ref)


def even_out_proj(a, b, wa, wb, h, post_g, gate, tm):
    bsz, length, d = h.shape
    width = a.shape[2]
    half = pl.BlockSpec((1, tm, width), lambda bb, i: (bb, i, 0))
    wspec = pl.BlockSpec((width, d), lambda bb, i: (0, 0))
    full = pl.BlockSpec((1, tm, d), lambda bb, i: (bb, i, 0))
    return pl.pallas_call(
        _even_out_kernel,
        name="even_out_proj",
        grid=(bsz, length // tm),
        in_specs=[half, half, wspec, wspec, full,
                  pl.BlockSpec((1, d), lambda bb, i: (0, 0)),
                  pl.BlockSpec((1, 1, d), lambda bb, i: (bb, 0, 0))],
        out_specs=full,
        out_shape=jax.ShapeDtypeStruct((bsz, length, d), F32),
        compiler_params=_params(("parallel", "parallel")),
    )(a, b, wa, wb, h, post_g.reshape(1, d), gate)


ODD_TN = 1024


def _odd_in_kernel(h_ref, g_ref, sh_ref, sc_ref, w_ref, o_ref, xn_ref, *, first_gate_tile):
    n = pl.program_id(2)

    @pl.when(n == 0)
    def _():
        _prenorm_modulate(h_ref, g_ref, sh_ref, sc_ref, xn_ref)

    acc = _dot(xn_ref[...], w_ref[...])

    @pl.when(n < first_gate_tile)
    def _():
        o_ref[0] = acc.astype(BF16)

    @pl.when(n >= first_gate_tile)
    def _():
        o_ref[0] = _silu(acc).astype(BF16)


def odd_in_proj(h, pre_g, shift, scale, w_all, n_plain, tm):
    bsz, length, d = h.shape
    n_all = w_all.shape[1]
    vec = pl.BlockSpec((1, 1, d), lambda b, i, n: (b, 0, 0))
    return pl.pallas_call(
        functools.partial(_odd_in_kernel, first_gate_tile=n_plain // ODD_TN),
        name="odd_in_proj",
        grid=(bsz, length // tm, n_all // ODD_TN),
        in_specs=[pl.BlockSpec((1, tm, d), lambda b, i, n: (b, i, 0)),
                  pl.BlockSpec((1, d), lambda b, i, n: (0, 0)),
                  vec, vec,
                  pl.BlockSpec((d, ODD_TN), lambda b, i, n: (0, n))],
        out_specs=pl.BlockSpec((1, tm, ODD_TN), lambda b, i, n: (b, i, n)),
        out_shape=jax.ShapeDtypeStruct((bsz, length, n_all), BF16),
        scratch_shapes=[pltpu.VMEM((tm, d), BF16)],
        compiler_params=_params(("parallel", "parallel", "arbitrary")),
    )(h, pre_g.reshape(1, d), shift, scale, w_all)


def _na_pair_geometry(variant, rows):
    r0 = {0: 4, 1: 0, 2: 2, 3: rows - 4, 4: rows - 2}[variant]
    ws = min(max(r0 - NA_ROWS // 2, 0), rows - NA_WIN_ROWS)
    return r0, ws


def _na_bias_kernel(rpb_ref, o_ref, *, rows):
    h = pl.program_id(0)
    n_dr, n_dc = 2 * NA_ROWS - 1, 2 * NA_COLS - 1
    qc = lax.broadcasted_iota(jnp.int32, (GRID_W, LANES), 0)
    lane = lax.broadcasted_iota(jnp.int32, (GRID_W, LANES), 1)
    kc = lane % GRID_W
    hi = lane // GRID_W
    diff = kc - qc
    c_start = jnp.clip(qc - NA_COLS // 2, 0, GRID_W - NA_COLS)
    col_ok = jnp.where(kc >= c_start, 1, 0) * jnp.where(kc < c_start + NA_COLS, 1, 0)
    blocks = []
    for dr in range(n_dr):
        val = jnp.full((GRID_W, LANES), NEG_INF, F32)
        for dc in range(n_dc):
            val = jnp.where(diff == dc - (NA_COLS - 1), rpb_ref[(h * n_dr + dr) * n_dc + dc], val)
        blocks.append(jnp.where(col_ok == 1, val, NEG_INF))
    masked = jnp.full((GRID_W, LANES), NEG_INF, F32)
    for variant in range(NA_VARIANTS):
        r0, ws = _na_pair_geometry(variant, rows)
        for ri in range(2):
            r = r0 + ri
            r_start = min(max(r - NA_ROWS // 2, 0), rows - NA_ROWS)
            for wp in range(NA_WIN_ROWS // 2):
                halves = []
                for a in (ws + 2 * wp, ws + 2 * wp + 1):
                    in_window = r_start <= a < r_start + NA_ROWS
                    halves.append(blocks[a - r + NA_ROWS - 1] if in_window else masked)
                o_ref[0, variant, pl.ds(ri * GRID_W, GRID_W), pl.ds(wp * LANES, LANES)] = jnp.where(
                    hi == 1, halves[1], halves[0])


def na_bias_table(rpb, rows):
    heads = rpb.shape[0]
    return pl.pallas_call(
        functools.partial(_na_bias_kernel, rows=rows),
        name="na_bias_table",
        grid=(heads,),
        in_specs=[pl.BlockSpec(memory_space=pltpu.SMEM)],
        out_specs=pl.BlockSpec((1, NA_VARIANTS, NA_PAIR, NA_WIN), lambda h: (h, 0, 0, 0)),
        out_shape=jax.ShapeDtypeStruct((heads, NA_VARIANTS, NA_PAIR, NA_WIN), F32),
        compiler_params=_params(("parallel",)),
    )(rpb.reshape(-1))


def _na_kernel(q_ref, k_ref, v_ref, kc_ref, vc_ref, bias_ref, sg_ref, o_ref,
               vx_ref, vcx_ref, s0_ref, s1_ref, p0_ref, p1_ref, *, rows):
    npairs = rows // 2
    last = npairs - 1
    nt = (((1,), (1,)), ((), ()))
    vx_ref[:, :HEAD_DIM] = v_ref[0]
    vx_ref[:, HEAD_DIM:] = jnp.ones((vx_ref.shape[0], HEAD_DIM), BF16)
    vcx_ref[:, :HEAD_DIM] = vc_ref[0]
    vcx_ref[:, HEAD_DIM:] = jnp.ones((vcx_ref.shape[0], HEAD_DIM), BF16)

    def window(pr):
        ws = jnp.clip(2 * pr - NA_ROWS // 2, 0, rows - NA_WIN_ROWS)
        return pl.ds(pl.multiple_of(ws * GRID_W, NA_PAIR), NA_WIN)

    def qrows(pr):
        return pl.ds(pl.multiple_of(pr * NA_PAIR, NA_PAIR), NA_PAIR)

    def scores(pr, s_ref):
        variant = jnp.where(pr == 0, 1, jnp.where(pr == 1, 2, jnp.where(
            pr == last - 1, 3, jnp.where(pr == last, 4, 0))))
        q = q_ref[0, qrows(pr), :]
        s_ref[:, :NA_WIN] = lax.dot_general(q, k_ref[0, window(pr), :], nt,
                                            preferred_element_type=F32) + bias_ref[0, variant]
        s_ref[:, NA_WIN:] = lax.dot_general(q, kc_ref[0], nt, preferred_element_type=F32)

    def probs(s_ref, p_ref):
        s = s_ref[...]
        p_ref[...] = jnp.exp(s - jnp.max(s, axis=-1, keepdims=True)).astype(BF16)

    def values(pr, p_ref):
        acc = _dot(p_ref[:, :NA_WIN], vx_ref[window(pr), :]) + _dot(p_ref[:, NA_WIN:], vcx_ref[...])
        o = acc[:, :HEAD_DIM] / acc[:, HEAD_DIM:]
        o_ref[0, qrows(pr), :] = (o * sg_ref[0, qrows(pr), :].astype(F32)).astype(BF16)

    scores(0, s0_ref)
    scores(1, s1_ref)
    probs(s0_ref, p0_ref)

    def two_pairs(i2, carry):
        i = 2 * i2
        values(i, p0_ref)
        probs(s1_ref, p1_ref)
        scores(jnp.minimum(i + 2, last), s0_ref)
        values(i + 1, p1_ref)
        probs(s0_ref, p0_ref)
        scores(jnp.minimum(i + 3, last), s1_ref)
        return carry

    lax.fori_loop(0, npairs // 2, two_pairs, 0)


def neighbourhood_attention(proj, proj_c, bias, na_w, gate_col):
    bsz, length, _ = proj.shape
    lc = proj_c.shape[1]
    heads = na_w // HEAD_DIM
    rows = length // GRID_W
    seq = lambda off: pl.BlockSpec((1, length, HEAD_DIM), lambda h, b: (b, 0, off + h))
    cseq = lambda off: pl.BlockSpec((1, lc, HEAD_DIM), lambda h, b: (b, 0, off + h))
    return pl.pallas_call(
        functools.partial(_na_kernel, rows=rows),
        name="neighbourhood_attention",
        grid=(heads, bsz),
        in_specs=[seq(0), seq(heads), seq(2 * heads), cseq(heads), cseq(2 * heads),
                  pl.BlockSpec((1, NA_VARIANTS, NA_PAIR, NA_WIN), lambda h, b: (h, 0, 0, 0)),
                  seq(gate_col // HEAD_DIM)],
        out_specs=seq(0),
        out_shape=jax.ShapeDtypeStruct((bsz, length, na_w), BF16),
        scratch_shapes=[pltpu.VMEM((length, 2 * HEAD_DIM), BF16), pltpu.VMEM((lc, 2 * HEAD_DIM), BF16),
                        pltpu.VMEM((NA_PAIR, NA_WIN + lc), F32), pltpu.VMEM((NA_PAIR, NA_WIN + lc), F32),
                        pltpu.VMEM((NA_PAIR, NA_WIN + lc), BF16), pltpu.VMEM((NA_PAIR, NA_WIN + lc), BF16)],
        compiler_params=_params(("parallel", "parallel")),
    )(proj, proj, proj, proj_c, proj_c, bias, proj)


S5_CHUNK = 128
S5_IN_BLOCK = LANES
S5_OUT_BLOCK = 256
S5_SCAN_COLS = 1024


def _s5_kernel(ufc_ref, ufx_ref, ubc_ref, ubx_ref, bf_ref, bb_ref, crf_ref, cif_ref, crb_ref, cib_ref,
               lre_ref, lim_ref, yf_ref, yb_ref, buf_ref, state_ref, stage_ref, *, ctx_chunks):
    bsz, t_len, width = ufx_ref.shape
    n_state = lre_ref.shape[1]
    in_ctx = pl.program_id(0) < ctx_chunks
    n_in = width // S5_IN_BLOCK
    blk = n_state // n_in
    rows8 = 2 * bsz

    @pl.when(pl.program_id(0) == 0)
    def _():
        state_ref[...] = jnp.zeros_like(state_ref)

    nlb = n_state // LANES
    per_in = blk // LANES
    ti = lax.broadcasted_iota(jnp.int32, (t_len, t_len), 0)
    tj = lax.broadcasted_iota(jnp.int32, (t_len, t_len), 1)
    rev = jnp.where(ti + tj == t_len - 1, 1.0, 0.0).astype(BF16)
    u_fwd = jnp.where(in_ctx, ufc_ref[...], ufx_ref[...]).reshape(bsz * t_len, width)
    u_nat = jnp.where(in_ctx, ubc_ref[...], ubx_ref[...])
    u_bwd = jnp.concatenate([_dot(rev, u_nat[b]).astype(BF16) for b in range(bsz)], axis=0)
    for di, (u, w_ref) in enumerate(((u_fwd, bf_ref), (u_bwd, bb_ref))):
        for kb in range(n_in):
            res = _dot(u[:, kb * S5_IN_BLOCK:(kb + 1) * S5_IN_BLOCK], w_ref[kb])
            for b in range(bsz):
                rsel = pl.ds(di * bsz + b, t_len, stride=rows8)
                for c in range(2 * per_in):
                    dst = (c // per_in) * nlb + kb * per_in + c % per_in
                    buf_ref[dst, rsel, :] = res[b * t_len:(b + 1) * t_len, c * LANES:(c + 1) * LANES]

    per = S5_SCAN_COLS // LANES
    for cbk in range(nlb // per):
        blocks = [cbk * per + i for i in range(per)]
        lre = [lre_ref[:, pl.ds(c * LANES, LANES)] for c in blocks]
        lim = [lim_ref[:, pl.ds(c * LANES, LANES)] for c in blocks]

        def step(t, carry, blocks=blocks, lre=lre, lim=lim):
            row = pl.ds(pl.multiple_of(t * rows8, rows8), rows8)
            new = []
            for i, c in enumerate(blocks):
                hre, him = carry[2 * i], carry[2 * i + 1]
                nre = lre[i] * hre - lim[i] * him + buf_ref[c, row, :]
                nim = lre[i] * him + lim[i] * hre + buf_ref[nlb + c, row, :]
                buf_ref[c, row, :] = nre
                buf_ref[nlb + c, row, :] = nim
                new += [nre, nim]
            return tuple(new)

        init = tuple(state_ref[:, pl.ds(part * n_state + c * LANES, LANES)] for c in blocks for part in range(2))
        fin = lax.fori_loop(0, t_len, step, init, unroll=2)
        for i, c in enumerate(blocks):
            state_ref[:, pl.ds(c * LANES, LANES)] = fin[2 * i]
            state_ref[:, pl.ds(n_state + c * LANES, LANES)] = fin[2 * i + 1]

    n_out = width // S5_OUT_BLOCK
    per_out = nlb // n_out
    for di, (y_ref, cr_ref, ci_ref) in enumerate(((yf_ref, crf_ref, cif_ref), (yb_ref, crb_ref, cib_ref))):
        for j in range(n_out):
            for part in range(2):
                for b in range(bsz):
                    rsel = pl.ds(di * bsz + b, t_len, stride=rows8)
                    for c in range(per_out):
                        stage_ref[part, pl.ds(b * t_len, t_len), pl.ds(c * LANES, LANES)] = buf_ref[
                            part * nlb + j * per_out + c, rsel, :].astype(BF16)
            y = _dot(stage_ref[0], cr_ref[j]) + _dot(stage_ref[1], ci_ref[j])
            for b in range(bsz):
                yb = y[b * t_len:(b + 1) * t_len]
                if di == 1:
                    hi = yb.astype(BF16)
                    r1 = yb - hi.astype(F32)
                    mid = r1.astype(BF16)
                    lo = (r1 - mid.astype(F32)).astype(BF16)
                    yb = _dot(rev, hi) + _dot(rev, mid) + _dot(rev, lo)
                y_ref[b, :, pl.ds(j * S5_OUT_BLOCK, S5_OUT_BLOCK)] = yb


def _block_diag(x):
    nblk, g, r, c = x.shape
    eye = jnp.eye(g, dtype=x.dtype)
    return jnp.einsum("kgrc,gh->kgrhc", x, eye).reshape(nblk, g * r, g * c)


def _s5_direction_params(a_re, a_im, log_dt, b_re, b_im, c_re, c_im):
    groups, n_p = a_re.shape
    lam = lax.complex(a_re.astype(F32), a_im.astype(F32))
    dt = jnp.exp(log_dt.astype(F32))[:, None]
    lam_bar = jnp.exp(lam * dt)
    b_bar = ((lam_bar - 1.0) / lam)[..., None] * lax.complex(b_re.astype(F32), b_im.astype(F32))
    gi = S5_IN_BLOCK // SSM_GROUP
    bt = jnp.swapaxes(b_bar, 1, 2).reshape(groups // gi, gi, SSM_GROUP, n_p)
    b_mat = jnp.concatenate([_block_diag(jnp.real(bt)), _block_diag(jnp.imag(bt))], axis=-1)
    go = S5_OUT_BLOCK // SSM_GROUP
    ct = lambda c: jnp.swapaxes(c.astype(F32), 1, 2).reshape(groups // go, go, n_p, SSM_GROUP)
    return (b_mat.astype(BF16), _block_diag(ct(c_re)).astype(BF16), _block_diag(-ct(c_im)).astype(BF16),
            jnp.real(lam_bar).reshape(-1), jnp.imag(lam_bar).reshape(-1))


def s5_scan(proj_c, proj_x, d_col, width, fwd, bwd):
    bsz, lc, _ = proj_c.shape
    length = proj_x.shape[1]
    t_len = S5_CHUNK
    ncc, ncx = lc // t_len, length // t_len
    nc = ncc + ncx
    col = d_col // width
    n_state = fwd[3].shape[0]
    lre = jnp.concatenate([jnp.broadcast_to(fwd[3], (bsz, n_state)), jnp.broadcast_to(bwd[3], (bsz, n_state))])
    lim = jnp.concatenate([jnp.broadcast_to(fwd[4], (bsz, n_state)), jnp.broadcast_to(bwd[4], (bsz, n_state))])
    full = lambda x: pl.BlockSpec(x.shape, lambda c, nd=x.ndim: (0,) * nd)
    blk = lambda index: pl.BlockSpec((bsz, t_len, width), index)
    in_blocks = [blk(lambda c: (0, jnp.minimum(c, ncc - 1), col)),
                 blk(lambda c: (0, jnp.maximum(c - ncc, 0), col)),
                 blk(lambda c: (0, jnp.maximum(ncc - 1 - c, 0), col)),
                 blk(lambda c: (0, jnp.minimum(nc - 1 - c, ncx - 1), col))]
    out = jax.ShapeDtypeStruct((bsz, lc + length, width), F32)
    consts = (fwd[0], bwd[0], fwd[1], fwd[2], bwd[1], bwd[2], lre, lim)
    return pl.pallas_call(
        functools.partial(_s5_kernel, ctx_chunks=ncc),
        name="s5_scan",
        grid=(nc,),
        in_specs=in_blocks + [full(x) for x in consts],
        out_specs=[blk(lambda c: (0, c, 0)), blk(lambda c: (0, nc - 1 - c, 0))],
        out_shape=[out, out],
        scratch_shapes=[pltpu.VMEM((2 * n_state // LANES, t_len * 2 * bsz, LANES), F32),
                        pltpu.VMEM((2 * bsz, 2 * n_state), F32),
                        pltpu.VMEM((2, bsz * t_len, n_state // (width // S5_OUT_BLOCK)), BF16)],
        compiler_params=_params(("arbitrary",)),
    )(proj_c, proj_x, proj_c, proj_x, *consts)


def _odd_out_kernel(na_ref, yf_ref, yb_ref, d_ref, sdg_ref, dsk_ref, wglu_ref, wna_ref, wssm_ref,
                    h_ref, pg_ref, gt_ref, o_ref):
    y = yf_ref[0] + yb_ref[0] + dsk_ref[...] * d_ref[0].astype(F32)
    y = 0.5 * y * (1.0 + jnp.tanh(math.sqrt(2.0 / math.pi) * (y + 0.044715 * (y * y * y))))
    z = y * jax.nn.sigmoid(_dot(y.astype(BF16), wglu_ref[...]))
    s = (z * sdg_ref[0].astype(F32)).astype(BF16)
    out = _dot(na_ref[0], wna_ref[...]) + _dot(s, wssm_ref[...])
    _postnorm_residual(out, h_ref, pg_ref, gt_ref, o_ref)


def odd_out_proj(na, yf, yb, proj, d_col, dg_col, d_skip, w_glu, w_na, w_ssm, h, post_g, gate, ctx_len, tm):
    bsz, length, dm = h.shape
    na_w, ssm_w = na.shape[2], yf.shape[2]
    off = ctx_len // tm
    tok = lambda w, col=0: pl.BlockSpec((1, tm, w), lambda b, i: (b, i, col // w))
    const = lambda r, c: pl.BlockSpec((r, c), lambda b, i: (0, 0))
    return pl.pallas_call(
        _odd_out_kernel,
        name="odd_out_proj",
        grid=(bsz, length // tm),
        in_specs=[tok(na_w),
                  pl.BlockSpec((1, tm, ssm_w), lambda b, i: (b, i + off, 0)),
                  tok(ssm_w),
                  tok(ssm_w, d_col), tok(ssm_w, dg_col),
                  const(1, ssm_w), const(ssm_w, ssm_w), const(na_w, dm), const(ssm_w, dm),
                  tok(dm), const(1, dm),
                  pl.BlockSpec((1, 1, dm), lambda b, i: (b, 0, 0))],
        out_specs=tok(dm),
        out_shape=jax.ShapeDtypeStruct((bsz, length, dm), F32),
        compiler_params=_params(("parallel", "parallel")),
    )(na, yf, yb, proj, proj, d_skip.reshape(1, ssm_w), w_glu, w_na, w_ssm, h, post_g.reshape(1, dm), gate)


def _small_dft(n):
    k = jnp.arange(n, dtype=jnp.int32)
    ang = ((k[:, None] * k[None, :]) % n).astype(F32) * (2.0 * math.pi / n)
    return jnp.concatenate([jnp.cos(ang), jnp.sin(ang)], axis=1).astype(BF16)


def _conv_fourier_layer(h, pre_g, post_g, shift, scale, gate, w5, w_out_a, w_out_b, conv_w, conv_b,
                        ln_g, ln_b, fourier_g, cs_small, tm, tl, tdft, shared_mod):
    bsz, length, _ = h.shape
    gd = fourier_g.shape[1]
    fold = (lambda t: t.reshape(1, bsz * length, t.shape[2])) if shared_mod else (lambda t: t)
    unfold = (lambda t: t.reshape(bsz, length, t.shape[2])) if shared_mod else (lambda t: t)
    vec = (lambda v: v[:1]) if shared_mod else (lambda v: v)
    outs = even_in_proj(fold(h), pre_g, vec(shift), vec(scale), w5, fourier_g, cs_small, tm)
    a, sga, p, q, sgb = [unfold(t) for t in outs]
    a = conv_branch(a, sga, conv_w, conv_b, ln_g, ln_b, tl)
    ortho = 1.0 / math.sqrt(length * gd)
    if (length // FFT_MINOR) % SUBLANES == 0:
        f = fourier_branch_fft(p, q, sgb, ortho)
    else:
        cmat, smat = dft_matrices(length)
        f = fourier_branch(cmat, smat, p, q, sgb, ortho, tdft, tdft)
    return unfold(even_out_proj(fold(a), fold(f), w_out_a, w_out_b, fold(h), post_g, vec(gate), min(tm, 512)))


def kernel(x, c, ctx, c_ctx, pre_g, post_g, ada_w, ada_b, ab_w_in, ab_w_out, conv_w, conv_b, conv_ln_g,
           conv_ln_b, fourier_g, cd_w_in, cd_w_out, na_rpb, s5_a_re, s5_a_im, s5_log_dt, s5_b_re, s5_b_im,
           s5_c_re, s5_c_im, s5_d, s5_w_glu):
    bsz, length, d = x.shape
    lc = ctx.shape[1]
    depth = ada_w.shape[0]
    assert depth == 2 and length % (2 * GRID_W) == 0 and length // GRID_W >= NA_WIN_ROWS + 2
    assert lc % S5_CHUNK == 0 and length % S5_CHUNK == 0

    cond_rows = jnp.zeros((SUBLANES, d), F32).at[:bsz].set(c).at[bsz].set(c_ctx)
    mods = ada_modulation(cond_rows, ada_w, ada_b)

    def mod_vectors(i):
        xs = [mods[i, :bsz, k * d:(k + 1) * d][:, None, :] for k in range(3)]
        cs = [jnp.broadcast_to(mods[i, bsz, k * d:(k + 1) * d], (bsz, 1, d)) for k in range(3)]
        return xs, cs

    (sh_x, sc_x, gt_x), (sh_c, sc_c, gt_c) = mod_vectors(0)
    w_in = ab_w_in[0]
    conv_width = conv_w.shape[2]
    cw = conv_width
    fw = (w_in.shape[1] - 3 * cw) // 2
    assert cw == fw
    w5 = w_in.astype(BF16)
    w_out_a = ab_w_out[0, :cw].astype(BF16)
    w_out_b = ab_w_out[0, cw:].astype(BF16)
    cs_small = _small_dft(fourier_g.shape[2])
    layer0 = functools.partial(
        _conv_fourier_layer, pre_g=pre_g[0], post_g=post_g[0], w5=w5, w_out_a=w_out_a, w_out_b=w_out_b,
        conv_w=conv_w[0], conv_b=conv_b[0], ln_g=conv_ln_g[0], ln_b=conv_ln_b[0], fourier_g=fourier_g[0],
        cs_small=cs_small)
    h_x = layer0(x, shift=sh_x, scale=sc_x, gate=gt_x, tm=1024, tl=256, tdft=1024, shared_mod=False)
    h_c = layer0(ctx, shift=sh_c, scale=sc_c, gate=gt_c, tm=min(1024, bsz * lc), tl=lc, tdft=lc, shared_mod=True)

    (sh_x, sc_x, gt_x), (sh_c, sc_c, _) = mod_vectors(1)
    w_in = cd_w_in[0]
    ssm_w = s5_d.shape[1]
    na_w = (w_in.shape[1] - 2 * ssm_w) // 4
    d_col, g_col, dg_col = 3 * na_w, 3 * na_w + ssm_w, 4 * na_w + ssm_w
    w_all = jnp.concatenate(
        [w_in[:, :na_w] * (HEAD_DIM ** -0.5),
         w_in[:, na_w:3 * na_w], w_in[:, 4 * na_w:4 * na_w + ssm_w],
         w_in[:, 3 * na_w:4 * na_w], w_in[:, 4 * na_w + ssm_w:]], axis=1).astype(BF16)
    assert g_col % ODD_TN == 0 and w_all.shape[1] % ODD_TN == 0
    proj_x = odd_in_proj(h_x, pre_g[1], sh_x, sc_x, w_all, g_col, tm=1024)
    proj_c = odd_in_proj(h_c.reshape(1, bsz * lc, d), pre_g[1], sh_c[:1], sc_c[:1], w_all, g_col,
                         tm=bsz * lc).reshape(bsz, lc, -1)

    bias = na_bias_table(na_rpb[0], length // GRID_W)
    na = neighbourhood_attention(proj_x, proj_c, bias, na_w, g_col)

    dirs = [_s5_direction_params(s5_a_re[0, i], s5_a_im[0, i], s5_log_dt[0, i], s5_b_re[0, i], s5_b_im[0, i],
                                 s5_c_re[0, i], s5_c_im[0, i]) for i in range(2)]
    y_f, y_b = s5_scan(proj_c, proj_x, d_col, ssm_w, *dirs)

    return odd_out_proj(na, y_f, y_b, proj_x, d_col, dg_col, s5_d[0], s5_w_glu[0].astype(BF16),
                        cd_w_out[0, :na_w].astype(BF16), cd_w_out[0, na_w:].astype(BF16),
                        h_x, post_g[1], gt_x, lc, tm=256)
```

```python
import functools
import math

import jax
import jax.numpy as jnp
import numpy as np
from jax import lax
from jax.experimental import pallas as pl
from jax.experimental.pallas import tpu as pltpu

F32 = jnp.float32
BF16 = jnp.bfloat16

EPS = 1e-6
NEG_INF = -1e30

GRID_W = 64
CONV_K = 31
FOURIER_GROUPS = 4
HEAD_DIM = 128
NA_ROWS = 8
NA_COLS = 16
SSM_GROUP = 16
SSM_STATE = 64

LANES = 128
SUBLANES = 8
VMEM_LIMIT = 56 * 1024 * 1024

NA_PAIR = 2 * GRID_W
NA_WIN_ROWS = NA_ROWS + 2
NA_WIN = NA_WIN_ROWS * GRID_W
NA_VARIANTS = 5


def _params(sem, vmem=VMEM_LIMIT):
    return pltpu.CompilerParams(dimension_semantics=sem, vmem_limit_bytes=vmem)


def _silu(x):
    return x * jax.nn.sigmoid(x)


def _rms(x):
    return x * lax.rsqrt(jnp.mean(x * x, axis=-1, keepdims=True) + EPS)


def _dot(a, b):
    return jnp.dot(a, b, preferred_element_type=F32)


def _ada_kernel(c_ref, w_ref, b_ref, o_ref):
    cond = _silu(c_ref[...])
    o_ref[0] = _dot(cond.astype(BF16), w_ref[0].astype(BF16)) + b_ref[0]


def ada_modulation(cond_rows, ada_w, ada_b, tn=1024):
    depth, d, n = ada_w.shape
    rows = cond_rows.shape[0]
    return pl.pallas_call(
        _ada_kernel,
        name="ada_modulation",
        grid=(depth, n // tn),
        in_specs=[
            pl.BlockSpec((rows, d), lambda i, j: (0, 0)),
            pl.BlockSpec((1, d, tn), lambda i, j: (i, 0, j)),
            pl.BlockSpec((1, 1, tn), lambda i, j: (i, 0, j)),
        ],
        out_specs=pl.BlockSpec((1, rows, tn), lambda i, j: (i, 0, j)),
        out_shape=jax.ShapeDtypeStruct((depth, rows, n), F32),
        compiler_params=_params(("parallel", "parallel")),
    )(cond_rows, ada_w, ada_b.reshape(depth, 1, n))


PRENORM_ROWS = 16


def _prenorm_modulate(h_ref, g_ref, sh_ref, sc_ref, xn_ref):
    gain = g_ref[...] * (1.0 + sc_ref[0])
    shift = sh_ref[0]

    def chunk(r, carry):
        rows = pl.ds(pl.multiple_of(r * PRENORM_ROWS, PRENORM_ROWS), PRENORM_ROWS)
        xn_ref[rows, :] = (_rms(h_ref[0, rows, :]) * gain + shift).astype(BF16)
        return carry

    lax.fori_loop(0, xn_ref.shape[0] // PRENORM_ROWS, chunk, 0, unroll=4)


EVEN_PARTS = 5


def _even_in_kernel(h_ref, g_ref, sh_ref, sc_ref, w0_ref, w1_ref, w2_ref, w3_ref, w4_ref, fg_ref, cs_ref,
                    a_ref, sga_ref, p_ref, q_ref, sgb_ref, xn_ref):
    @pl.when(pl.program_id(2) == 0)
    def _():
        _prenorm_modulate(h_ref, g_ref, sh_ref, sc_ref, xn_ref)

    gd = a_ref.shape[2]
    w = jnp.concatenate([w0_ref[...], w1_ref[...], w2_ref[...], w3_ref[...], w4_ref[...]], axis=1)
    acc = _dot(xn_ref[...], w)
    part = lambda k: acc[:, k * gd:(k + 1) * gd]
    a_ref[0] = (part(0) * jax.nn.sigmoid(part(1))).astype(BF16)
    sga_ref[0] = _silu(part(2)).astype(BF16)
    bn = _rms(part(3)) * fg_ref[0]
    pq = _dot(bn.astype(BF16), cs_ref[...])
    p_ref[0] = pq[:, :gd].astype(BF16)
    q_ref[0] = pq[:, gd:].astype(BF16)
    sgb_ref[0] = _silu(part(4)).astype(BF16)


def even_in_proj(h, pre_g, shift, scale, w_in, fourier_g, cs_small, tm):
    bsz, length, d = h.shape
    gd = fourier_g.shape[1]
    width = FOURIER_GROUPS * gd
    wspecs = [pl.BlockSpec((d, gd), lambda b, i, n, k=k: (0, k * FOURIER_GROUPS + n)) for k in range(EVEN_PARTS)]
    ospec = pl.BlockSpec((1, tm, gd), lambda b, i, n: (b, i, n))
    vec = pl.BlockSpec((1, 1, d), lambda b, i, n: (b, 0, 0))
    out = jax.ShapeDtypeStruct((bsz, length, width), BF16)
    return pl.pallas_call(
        _even_in_kernel,
        name="even_in_proj",
        grid=(bsz, length // tm, FOURIER_GROUPS),
        in_specs=[
            pl.BlockSpec((1, tm, d), lambda b, i, n: (b, i, 0)),
            pl.BlockSpec((1, d), lambda b, i, n: (0, 0)),
            vec, vec,
            *wspecs,
            pl.BlockSpec((1, 1, gd), lambda b, i, n: (n, 0, 0)),
            pl.BlockSpec((gd, 2 * gd), lambda b, i, n: (0, 0)),
        ],
        out_specs=[ospec] * 5,
        out_shape=[out] * 5,
        scratch_shapes=[pltpu.VMEM((tm, d), BF16)],
        compiler_params=_params(("parallel", "parallel", "arbitrary")),
    )(h, pre_g.reshape(1, d), shift, scale, *([w_in] * EVEN_PARTS),
      fourier_g.reshape(FOURIER_GROUPS, 1, gd), cs_small)


CONV_HALO = 16
CONV_ROWS = 32


def _conv_kernel(a_ref, prev_ref, next_ref, w_ref, cb_ref, lg_ref, lb_ref, sga_ref, o_ref,
                 ext_ref, sh_ref, acc_ref):
    i = pl.program_id(1)
    last = pl.num_programs(1) - 1
    tl, width = acc_ref.shape
    ext_ref[pl.ds(CONV_HALO, tl), :] = a_ref[0].astype(F32)
    ext_ref[pl.ds(0, CONV_HALO), :] = jnp.where(i > 0, prev_ref[0].astype(F32), 0.0)
    ext_ref[pl.ds(CONV_HALO + tl, CONV_HALO), :] = jnp.where(i < last, next_ref[0].astype(F32), 0.0)
    base = CONV_HALO - CONV_K // 2
    span = sh_ref.shape[1]
    for r in range(1, SUBLANES):
        sh_ref[r - 1] = ext_ref[pl.ds(r, span), :]

    for cb in range(width // LANES):
        cols = pl.ds(cb * LANES, LANES)
        taps = [jnp.broadcast_to(w_ref[pl.ds(k, 1), cols], (SUBLANES, LANES)) for k in range(CONV_K)]
        bias = jnp.broadcast_to(cb_ref[:, cols], (SUBLANES, LANES))

        def rows(rb, carry, cols=cols, taps=taps, bias=bias):
            r0 = pl.multiple_of(rb * CONV_ROWS, CONV_ROWS)
            accs = [bias] * (CONV_ROWS // SUBLANES)
            for k in range(CONV_K):
                q, r = divmod(base + k, SUBLANES)
                src = ext_ref if r == 0 else sh_ref.at[r - 1]
                for j in range(len(accs)):
                    accs[j] = accs[j] + src[pl.ds(r0 + (q + j) * SUBLANES, SUBLANES), cols] * taps[k]
            for j, acc in enumerate(accs):
                acc_ref[pl.ds(r0 + j * SUBLANES, SUBLANES), cols] = acc
            return carry

        lax.fori_loop(0, tl // CONV_ROWS, rows, 0, unroll=2)
    x = acc_ref[...]
    mu = jnp.mean(x, axis=-1, keepdims=True)
    xc = x - mu
    var = jnp.mean(xc * xc, axis=-1, keepdims=True)
    y = xc * lax.rsqrt(var + EPS) * lg_ref[...] + lb_ref[...]
    o_ref[0] = (_silu(y) * sga_ref[0].astype(F32)).astype(BF16)


def conv_branch(a, sga, conv_w, conv_b, ln_g, ln_b, tl):
    bsz, length, width = a.shape
    hb = tl // CONV_HALO
    nh = length // CONV_HALO
    row = pl.BlockSpec((1, width), lambda b, i: (0, 0))
    main = pl.BlockSpec((1, tl, width), lambda b, i: (b, i, 0))
    return pl.pallas_call(
        _conv_kernel,
        name="conv_branch",
        grid=(bsz, length // tl),
        in_specs=[
            main,
            pl.BlockSpec((1, CONV_HALO, width), lambda b, i: (b, jnp.maximum(i * hb - 1, 0), 0)),
            pl.BlockSpec((1, CONV_HALO, width), lambda b, i: (b, jnp.minimum((i + 1) * hb, nh - 1), 0)),
            pl.BlockSpec((CONV_K, width), lambda b, i: (0, 0)),
            row, row, row,
            main,
        ],
        out_specs=main,
        out_shape=jax.ShapeDtypeStruct((bsz, length, width), BF16),
        scratch_shapes=[pltpu.VMEM((tl + 2 * CONV_HALO, width), F32),
                        pltpu.VMEM((SUBLANES - 1, tl + 2 * CONV_HALO - SUBLANES, width), F32),
                        pltpu.VMEM((tl, width), F32)],
        compiler_params=_params(("parallel", "parallel")),
    )(a, a, a, conv_w, conv_b.reshape(1, width), ln_g.reshape(1, width), ln_b.reshape(1, width), sga)


DFT_SPLIT = 64


def _dft_gen_kernel(ca_ref, sa_ref, cb_ref, sb_ref, c_ref, s_ref):
    ca, sa = ca_ref[0], sa_ref[0]
    cb, sb = cb_ref[...], sb_ref[...]
    c_ref[...] = (ca * cb - sa * sb).astype(BF16)
    s_ref[...] = (-(sa * cb + ca * sb)).astype(BF16)


def dft_matrices(length):
    na = length // DFT_SPLIT
    k = np.arange(length)[None, :]
    w = 2.0 * math.pi / length
    ang_a = ((DFT_SPLIT * np.arange(na)[:, None] * k) % length) * w
    ang_b = ((np.arange(DFT_SPLIT)[:, None] * k) % length) * w
    ca, sa = (jnp.asarray(f(ang_a).reshape(na, 1, length), F32) for f in (np.cos, np.sin))
    cb, sb = (jnp.asarray(f(ang_b), F32) for f in (np.cos, np.sin))
    tab_a = pl.BlockSpec((1, 1, length), lambda a: (a, 0, 0))
    tab_b = pl.BlockSpec((DFT_SPLIT, length), lambda a: (0, 0))
    out = pl.BlockSpec((DFT_SPLIT, length), lambda a: (a, 0))
    shp = jax.ShapeDtypeStruct((length, length), BF16)
    return pl.pallas_call(
        _dft_gen_kernel,
        name="dft_matrices",
        grid=(na,),
        in_specs=[tab_a, tab_a, tab_b, tab_b],
        out_specs=[out, out],
        out_shape=[shp, shp],
        compiler_params=_params(("parallel",)),
    )(ca, sa, cb, sb)


def _dft_kernel(c_ref, s_ref, p_ref, q_ref, sgb_ref, o_ref, acc_ref, *, scale):
    kk = pl.program_id(2)

    @pl.when(kk == 0)
    def _():
        acc_ref[...] = jnp.zeros_like(acc_ref)

    acc_ref[...] += _dot(c_ref[...], p_ref[0]) + _dot(s_ref[...], q_ref[0])

    @pl.when(kk == pl.num_programs(2) - 1)
    def _():
        o_ref[0] = (acc_ref[...] * scale * sgb_ref[0].astype(F32)).astype(BF16)


def fourier_branch(cmat, smat, p, q, sgb, scale, tm, tk):
    bsz, length, width = p.shape
    return pl.pallas_call(
        functools.partial(_dft_kernel, scale=scale),
        name="fourier_branch",
        grid=(bsz, length // tm, length // tk),
        in_specs=[
            pl.BlockSpec((tm, tk), lambda b, m, k: (m, k)),
            pl.BlockSpec((tm, tk), lambda b, m, k: (m, k)),
            pl.BlockSpec((1, tk, width), lambda b, m, k: (b, k, 0)),
            pl.BlockSpec((1, tk, width), lambda b, m, k: (b, k, 0)),
            pl.BlockSpec((1, tm, width), lambda b, m, k: (b, m, 0)),
        ],
        out_specs=pl.BlockSpec((1, tm, width), lambda b, m, k: (b, m, 0)),
        out_shape=jax.ShapeDtypeStruct((bsz, length, width), BF16),
        scratch_shapes=[pltpu.VMEM((tm, width), F32)],
        compiler_params=_params(("parallel", "parallel", "arbitrary")),
    )(cmat, smat, p, q, sgb)


FFT_MINOR = 64
FFT_COLS = 256


def _fft_kernel(p_ref, q_ref, k1a_ref, k1b_ref, twr_ref, twi_ref, hre_ref, him_ref, sgb_ref, o_ref,
                x_ref, y_ref, z_ref, *, scale):
    length, cw = p_ref.shape[1], p_ref.shape[2]
    ncb = cw // LANES
    n_m = length // FFT_MINOR
    n_a1 = FFT_MINOR // SUBLANES
    n_kb = n_m // SUBLANES
    slab = n_m * SUBLANES
    piece = SUBLANES * SUBLANES
    for cb in range(ncb):
        cols = pl.ds(cb * LANES, LANES)
        x_ref[cb] = p_ref[0, :, cols].astype(F32).reshape(n_m, FFT_MINOR, LANES)
        x_ref[ncb + cb] = q_ref[0, :, cols].astype(F32).reshape(n_m, FFT_MINOR, LANES)

    def slab_of(part, a1):
        tiles = [x_ref[part * ncb + cb, :, pl.ds(a1 * SUBLANES, SUBLANES), :].reshape(slab, LANES)
                 for cb in range(ncb)]
        return jnp.concatenate(tiles, axis=1).astype(BF16)

    for a1 in range(n_a1):
        y = _dot(k1a_ref[...], slab_of(0, a1)) + _dot(k1b_ref[...], slab_of(1, a1))
        tr, ti = twr_ref[a1], twi_ref[a1]
        for cb in range(ncb):
            yre = y[:slab, cb * LANES:(cb + 1) * LANES]
            yim = y[slab:, cb * LANES:(cb + 1) * LANES]
            zre = (tr * yre - ti * yim).astype(BF16)
            zim = (tr * yim + ti * yre).astype(BF16)
            for kb in range(n_kb):
                dst = (pl.ds(a1 * piece, piece), pl.ds(cb * LANES, LANES))
                y_ref[(kb, 0) + dst] = zre[kb * piece:(kb + 1) * piece]
                y_ref[(kb, 1) + dst] = zim[kb * piece:(kb + 1) * piece]

    for kb in range(n_kb):
        out = _dot(hre_ref[...], y_ref[kb, 0]) + _dot(him_ref[...], y_ref[kb, 1])
        z_ref[:, kb] = out.reshape(FFT_MINOR, SUBLANES, cw)

    o_ref[0] = (z_ref[...].reshape(length, cw) * scale * sgb_ref[0].astype(F32)).astype(BF16)


def _fft_tables(length):
    n_a, s = FFT_MINOR, SUBLANES
    n_m = length // n_a
    n_a1 = n_a // s
    eye = np.eye(s)
    ang_m = (np.outer(np.arange(n_m), np.arange(n_m)) % n_m) * (2.0 * math.pi / n_m)
    cm, sm = np.kron(np.cos(ang_m), eye), np.kron(np.sin(ang_m), eye)
    k1a = np.concatenate([cm, -sm], axis=0)
    k1b = np.concatenate([-sm, -cm], axis=0)
    a_full = s * np.arange(n_a1)[:, None, None] + np.arange(s)[None, None, :]
    ang_t = ((np.arange(n_m)[None, :, None] * a_full) % length) * (2.0 * math.pi / length)
    rep = lambda t: np.broadcast_to(t.reshape(n_a1, n_m * s, 1), (n_a1, n_m * s, LANES))
    twr, twi = rep(np.cos(ang_t)), rep(-np.sin(ang_t))
    ang_a = ((np.arange(n_a)[:, None, None] * a_full.reshape(1, n_a1, s)) % n_a) * (2.0 * math.pi / n_a)
    spread = lambda t: np.einsum("kas,pq->kpaqs", t, eye).reshape(n_a * s, n_a1 * s * s)
    hre, him = spread(np.cos(ang_a)), spread(np.sin(ang_a))
    return ([jnp.asarray(t, BF16) for t in (k1a, k1b)] + [jnp.asarray(t, F32) for t in (twr, twi)]
            + [jnp.asarray(t, BF16) for t in (hre, him)])


def fourier_branch_fft(p, q, sgb, scale):
    bsz, length, width = p.shape
    tables = _fft_tables(length)
    n_m = length // FFT_MINOR
    tok = pl.BlockSpec((1, length, FFT_COLS), lambda b, n: (b, 0, n))
    const = lambda t: pl.BlockSpec(t.shape, lambda b, n, nd=t.ndim: (0,) * nd)
    return pl.pallas_call(
        functools.partial(_fft_kernel, scale=scale),
        name="fourier_branch_fft",
        grid=(bsz, width // FFT_COLS),
        in_specs=[tok, tok] + [const(t) for t in tables] + [tok],
        out_specs=tok,
        out_shape=jax.ShapeDtypeStruct((bsz, length, width), BF16),
        scratch_shapes=[pltpu.VMEM((2 * FFT_COLS // LANES, n_m, FFT_MINOR, LANES), F32),
                        pltpu.VMEM((n_m // SUBLANES, 2, FFT_MINOR * SUBLANES, FFT_COLS), BF16),
                        pltpu.VMEM((FFT_MINOR, n_m // SUBLANES, SUBLANES, FFT_COLS), F32)],
        compiler_params=_params(("parallel", "parallel")),
    )(p, q, *tables, sgb)


def _postnorm_residual(y, h_ref, pg_ref, gt_ref, o_ref):
    o_ref[0] = h_ref[0] + gt_ref[0] * (_rms(y) * pg_ref[...])


def _even_out_kernel(a_ref, b_ref, wa_ref, wb_ref, h_ref, pg_ref, gt_ref, o_ref):
    y = _dot(a_ref[0], wa_ref[...]) + _dot(b_ref[0], wb_ref[...])
    _postnorm_residual(y, h_ref, pg_ref, gt_ref, o_ref)


def even_out_proj(a, b, wa, wb, h, post_g, gate, tm):
    bsz, length, d = h.shape
    width = a.shape[2]
    half = pl.BlockSpec((1, tm, width), lambda bb, i: (bb, i, 0))
    wspec = pl.BlockSpec((width, d), lambda bb, i: (0, 0))
    full = pl.BlockSpec((1, tm, d), lambda bb, i: (bb, i, 0))
    return pl.pallas_call(
        _even_out_kernel,
        name="even_out_proj",
        grid=(bsz, length // tm),
        in_specs=[half, half, wspec, wspec, full,
                  pl.BlockSpec((1, d), lambda bb, i: (0, 0)),
                  pl.BlockSpec((1, 1, d), lambda bb, i: (bb, 0, 0))],
        out_specs=full,
        out_shape=jax.ShapeDtypeStruct((bsz, length, d), F32),
        compiler_params=_params(("parallel", "parallel")),
    )(a, b, wa, wb, h, post_g.reshape(1, d), gate)


ODD_TN = 1024


ODD_PARTS = 2


def _odd_in_kernel(h_ref, g_ref, sh_ref, sc_ref, w_ref, o_ref, xn_ref, *, gate_parts):
    n = pl.program_id(2)

    @pl.when(n == 0)
    def _():
        _prenorm_modulate(h_ref, g_ref, sh_ref, sc_ref, xn_ref)

    xn = xn_ref[...]
    pw = ODD_TN // ODD_PARTS
    for j in range(ODD_PARTS):
        cols = pl.ds(j * pw, pw)
        acc = _dot(xn, w_ref[:, cols])
        tiles = [g // ODD_PARTS for g in gate_parts if g % ODD_PARTS == j]
        if tiles:
            is_gate = functools.reduce(jnp.logical_or, [n == t for t in tiles])
            acc = jnp.where(is_gate, _silu(acc), acc)
        o_ref[0, :, cols] = acc.astype(BF16)


def odd_in_proj(h, pre_g, shift, scale, w_all, gate_ranges, tm):
    bsz, length, d = h.shape
    n_all = w_all.shape[1]
    pw = ODD_TN // ODD_PARTS
    assert n_all % ODD_TN == 0 and all(s % pw == 0 and e % pw == 0 for s, e in gate_ranges)
    gate_parts = tuple(p for s, e in gate_ranges for p in range(s // pw, e // pw))
    vec = pl.BlockSpec((1, 1, d), lambda b, i, n: (b, 0, 0))
    return pl.pallas_call(
        functools.partial(_odd_in_kernel, gate_parts=gate_parts),
        name="odd_in_proj",
        grid=(bsz, length // tm, n_all // ODD_TN),
        in_specs=[pl.BlockSpec((1, tm, d), lambda b, i, n: (b, i, 0)),
                  pl.BlockSpec((1, d), lambda b, i, n: (0, 0)),
                  vec, vec,
                  pl.BlockSpec((d, ODD_TN), lambda b, i, n: (0, n))],
        out_specs=pl.BlockSpec((1, tm, ODD_TN), lambda b, i, n: (b, i, n)),
        out_shape=jax.ShapeDtypeStruct((bsz, length, n_all), BF16),
        scratch_shapes=[pltpu.VMEM((tm, d), BF16)],
        compiler_params=_params(("parallel", "parallel", "arbitrary")),
    )(h, pre_g.reshape(1, d), shift, scale, w_all)


def _na_pair_geometry(variant, rows):
    r0 = {0: 4, 1: 0, 2: 2, 3: rows - 4, 4: rows - 2}[variant]
    ws = min(max(r0 - NA_ROWS // 2, 0), rows - NA_WIN_ROWS)
    return r0, ws


def _na_bias_kernel(rpb_ref, o_ref, *, rows):
    h = pl.program_id(0)
    n_dr, n_dc = 2 * NA_ROWS - 1, 2 * NA_COLS - 1
    qc = lax.broadcasted_iota(jnp.int32, (GRID_W, LANES), 0)
    lane = lax.broadcasted_iota(jnp.int32, (GRID_W, LANES), 1)
    kc = lane % GRID_W
    hi = lane // GRID_W
    diff = kc - qc
    c_start = jnp.clip(qc - NA_COLS // 2, 0, GRID_W - NA_COLS)
    col_ok = jnp.where(kc >= c_start, 1, 0) * jnp.where(kc < c_start + NA_COLS, 1, 0)
    blocks = []
    for dr in range(n_dr):
        val = jnp.full((GRID_W, LANES), NEG_INF, F32)
        for dc in range(n_dc):
            val = jnp.where(diff == dc - (NA_COLS - 1), rpb_ref[(h * n_dr + dr) * n_dc + dc], val)
        blocks.append(jnp.where(col_ok == 1, val, NEG_INF))
    masked = jnp.full((GRID_W, LANES), NEG_INF, F32)
    for variant in range(NA_VARIANTS):
        r0, ws = _na_pair_geometry(variant, rows)
        for ri in range(2):
            r = r0 + ri
            r_start = min(max(r - NA_ROWS // 2, 0), rows - NA_ROWS)
            for wp in range(NA_WIN_ROWS // 2):
                halves = []
                for a in (ws + 2 * wp, ws + 2 * wp + 1):
                    in_window = r_start <= a < r_start + NA_ROWS
                    halves.append(blocks[a - r + NA_ROWS - 1] if in_window else masked)
                o_ref[0, variant, pl.ds(ri * GRID_W, GRID_W), pl.ds(wp * LANES, LANES)] = jnp.where(
                    hi == 1, halves[1], halves[0])


def na_bias_table(rpb, rows):
    heads = rpb.shape[0]
    return pl.pallas_call(
        functools.partial(_na_bias_kernel, rows=rows),
        name="na_bias_table",
        grid=(heads,),
        in_specs=[pl.BlockSpec(memory_space=pltpu.SMEM)],
        out_specs=pl.BlockSpec((1, NA_VARIANTS, NA_PAIR, NA_WIN), lambda h: (h, 0, 0, 0)),
        out_shape=jax.ShapeDtypeStruct((heads, NA_VARIANTS, NA_PAIR, NA_WIN), F32),
        compiler_params=_params(("parallel",)),
    )(rpb.reshape(-1))


def _na_kernel(q_ref, k_ref, v_ref, kc_ref, vc_ref, bias_ref, sg_ref, o_ref,
               vx_ref, vcx_ref, s0_ref, s1_ref, p0_ref, p1_ref, *, rows):
    npairs = rows // 2
    last = npairs - 1
    nt = (((1,), (1,)), ((), ()))
    vx_ref[:, :HEAD_DIM] = v_ref[0]
    vx_ref[:, HEAD_DIM:] = jnp.ones((vx_ref.shape[0], HEAD_DIM), BF16)
    vcx_ref[:, :HEAD_DIM] = vc_ref[0]
    vcx_ref[:, HEAD_DIM:] = jnp.ones((vcx_ref.shape[0], HEAD_DIM), BF16)

    def window(pr):
        ws = jnp.clip(2 * pr - NA_ROWS // 2, 0, rows - NA_WIN_ROWS)
        return pl.ds(pl.multiple_of(ws * GRID_W, NA_PAIR), NA_WIN)

    def qrows(pr):
        return pl.ds(pl.multiple_of(pr * NA_PAIR, NA_PAIR), NA_PAIR)

    def scores(pr, s_ref):
        variant = jnp.where(pr == 0, 1, jnp.where(pr == 1, 2, jnp.where(
            pr == last - 1, 3, jnp.where(pr == last, 4, 0))))
        q = q_ref[0, qrows(pr), :]
        s_ref[:, :NA_WIN] = lax.dot_general(q, k_ref[0, window(pr), :], nt,
                                            preferred_element_type=F32) + bias_ref[0, variant]
        s_ref[:, NA_WIN:] = lax.dot_general(q, kc_ref[0], nt, preferred_element_type=F32)

    def probs(s_ref, p_ref):
        s = s_ref[...]
        p_ref[...] = jnp.exp(s - jnp.max(s, axis=-1, keepdims=True)).astype(BF16)

    def values(pr, p_ref):
        acc = _dot(p_ref[:, :NA_WIN], vx_ref[window(pr), :]) + _dot(p_ref[:, NA_WIN:], vcx_ref[...])
        o = acc[:, :HEAD_DIM] / acc[:, HEAD_DIM:]
        o_ref[0, qrows(pr), :] = (o * sg_ref[0, qrows(pr), :].astype(F32)).astype(BF16)

    scores(0, s0_ref)
    scores(1, s1_ref)
    probs(s0_ref, p0_ref)

    def two_pairs(i2, carry):
        i = 2 * i2
        values(i, p0_ref)
        probs(s1_ref, p1_ref)
        scores(jnp.minimum(i + 2, last), s0_ref)
        values(i + 1, p1_ref)
        probs(s0_ref, p0_ref)
        scores(jnp.minimum(i + 3, last), s1_ref)
        return carry

    lax.fori_loop(0, npairs // 2, two_pairs, 0)


def neighbourhood_attention(proj, proj_c, bias, na_w, gate_col):
    bsz, length, _ = proj.shape
    lc = proj_c.shape[1]
    heads = na_w // HEAD_DIM
    rows = length // GRID_W
    seq = lambda off: pl.BlockSpec((1, length, HEAD_DIM), lambda h, b: (b, 0, off + h))
    cseq = lambda off: pl.BlockSpec((1, lc, HEAD_DIM), lambda h, b: (b, 0, off + h))
    return pl.pallas_call(
        functools.partial(_na_kernel, rows=rows),
        name="neighbourhood_attention",
        grid=(heads, bsz),
        in_specs=[seq(0), seq(heads), seq(2 * heads), cseq(heads), cseq(2 * heads),
                  pl.BlockSpec((1, NA_VARIANTS, NA_PAIR, NA_WIN), lambda h, b: (h, 0, 0, 0)),
                  seq(gate_col // HEAD_DIM)],
        out_specs=seq(0),
        out_shape=jax.ShapeDtypeStruct((bsz, length, na_w), BF16),
        scratch_shapes=[pltpu.VMEM((length, 2 * HEAD_DIM), BF16), pltpu.VMEM((lc, 2 * HEAD_DIM), BF16),
                        pltpu.VMEM((NA_PAIR, NA_WIN + lc), F32), pltpu.VMEM((NA_PAIR, NA_WIN + lc), F32),
                        pltpu.VMEM((NA_PAIR, NA_WIN + lc), BF16), pltpu.VMEM((NA_PAIR, NA_WIN + lc), BF16)],
        compiler_params=_params(("parallel", "parallel")),
    )(proj, proj, proj, proj_c, proj_c, bias, proj)


S5_CHUNK = 128
S5_IN_BLOCK = LANES
S5_OUT_BLOCK = 256
S5_SCAN_COLS = 1024


def _s5_kernel(ufc_ref, ufx_ref, ubc_ref, ubx_ref, bf_ref, bb_ref, crf_ref, cif_ref, crb_ref, cib_ref,
               lre_ref, lim_ref, yf_ref, yb_ref, buf_ref, state_ref, stage_ref, *, ctx_chunks):
    bsz, t_len, width = ufx_ref.shape
    n_state = lre_ref.shape[1]
    in_ctx = pl.program_id(0) < ctx_chunks
    n_in = width // S5_IN_BLOCK
    blk = n_state // n_in
    rows8 = 2 * bsz

    @pl.when(pl.program_id(0) == 0)
    def _():
        state_ref[...] = jnp.zeros_like(state_ref)

    nlb = n_state // LANES
    per_in = blk // LANES
    ti = lax.broadcasted_iota(jnp.int32, (t_len, t_len), 0)
    tj = lax.broadcasted_iota(jnp.int32, (t_len, t_len), 1)
    rev = jnp.where(ti + tj == t_len - 1, 1.0, 0.0).astype(BF16)
    u_fwd = jnp.where(in_ctx, ufc_ref[...], ufx_ref[...]).reshape(bsz * t_len, width)
    u_nat = jnp.where(in_ctx, ubc_ref[...], ubx_ref[...])
    u_bwd = jnp.concatenate([_dot(rev, u_nat[b]).astype(BF16) for b in range(bsz)], axis=0)
    for di, (u, w_ref) in enumerate(((u_fwd, bf_ref), (u_bwd, bb_ref))):
        for kb in range(n_in):
            res = _dot(u[:, kb * S5_IN_BLOCK:(kb + 1) * S5_IN_BLOCK], w_ref[kb])
            for b in range(bsz):
                rsel = pl.ds(di * bsz + b, t_len, stride=rows8)
                for c in range(2 * per_in):
                    dst = (c // per_in) * nlb + kb * per_in + c % per_in
                    buf_ref[dst, rsel, :] = res[b * t_len:(b + 1) * t_len, c * LANES:(c + 1) * LANES]

    per = S5_SCAN_COLS // LANES
    for cbk in range(nlb // per):
        blocks = [cbk * per + i for i in range(per)]
        lre = [lre_ref[:, pl.ds(c * LANES, LANES)] for c in blocks]
        lim = [lim_ref[:, pl.ds(c * LANES, LANES)] for c in blocks]

        def step(t, carry, blocks=blocks, lre=lre, lim=lim):
            row = pl.ds(pl.multiple_of(t * rows8, rows8), rows8)
            new = []
            for i, c in enumerate(blocks):
                hre, him = carry[2 * i], carry[2 * i + 1]
                nre = lre[i] * hre - lim[i] * him + buf_ref[c, row, :]
                nim = lre[i] * him + lim[i] * hre + buf_ref[nlb + c, row, :]
                buf_ref[c, row, :] = nre
                buf_ref[nlb + c, row, :] = nim
                new += [nre, nim]
            return tuple(new)

        init = tuple(state_ref[:, pl.ds(part * n_state + c * LANES, LANES)] for c in blocks for part in range(2))
        fin = lax.fori_loop(0, t_len, step, init, unroll=2)
        for i, c in enumerate(blocks):
            state_ref[:, pl.ds(c * LANES, LANES)] = fin[2 * i]
            state_ref[:, pl.ds(n_state + c * LANES, LANES)] = fin[2 * i + 1]

    n_out = width // S5_OUT_BLOCK
    per_out = nlb // n_out
    for di, (y_ref, cr_ref, ci_ref) in enumerate(((yf_ref, crf_ref, cif_ref), (yb_ref, crb_ref, cib_ref))):
        for j in range(n_out):
            for part in range(2):
                for b in range(bsz):
                    rsel = pl.ds(di * bsz + b, t_len, stride=rows8)
                    for c in range(per_out):
                        stage_ref[part, pl.ds(b * t_len, t_len), pl.ds(c * LANES, LANES)] = buf_ref[
                            part * nlb + j * per_out + c, rsel, :].astype(BF16)
            y = _dot(stage_ref[0], cr_ref[j]) + _dot(stage_ref[1], ci_ref[j])
            for b in range(bsz):
                yb = y[b * t_len:(b + 1) * t_len]
                if di == 1:
                    hi = yb.astype(BF16)
                    r1 = yb - hi.astype(F32)
                    mid = r1.astype(BF16)
                    lo = (r1 - mid.astype(F32)).astype(BF16)
                    yb = _dot(rev, hi) + _dot(rev, mid) + _dot(rev, lo)
                y_ref[b, :, pl.ds(j * S5_OUT_BLOCK, S5_OUT_BLOCK)] = yb


def _block_diag(x):
    nblk, g, r, c = x.shape
    eye = jnp.eye(g, dtype=x.dtype)
    return jnp.einsum("kgrc,gh->kgrhc", x, eye).reshape(nblk, g * r, g * c)


def _s5_direction_params(a_re, a_im, log_dt, b_re, b_im, c_re, c_im):
    groups, n_p = a_re.shape
    lam = lax.complex(a_re.astype(F32), a_im.astype(F32))
    dt = jnp.exp(log_dt.astype(F32))[:, None]
    lam_bar = jnp.exp(lam * dt)
    b_bar = ((lam_bar - 1.0) / lam)[..., None] * lax.complex(b_re.astype(F32), b_im.astype(F32))
    gi = S5_IN_BLOCK // SSM_GROUP
    bt = jnp.swapaxes(b_bar, 1, 2).reshape(groups // gi, gi, SSM_GROUP, n_p)
    b_mat = jnp.concatenate([_block_diag(jnp.real(bt)), _block_diag(jnp.imag(bt))], axis=-1)
    go = S5_OUT_BLOCK // SSM_GROUP
    ct = lambda c: jnp.swapaxes(c.astype(F32), 1, 2).reshape(groups // go, go, n_p, SSM_GROUP)
    return (b_mat.astype(BF16), _block_diag(ct(c_re)).astype(BF16), _block_diag(-ct(c_im)).astype(BF16),
            jnp.real(lam_bar).reshape(-1), jnp.imag(lam_bar).reshape(-1))


def s5_scan(proj_c, proj_x, d_col, width, fwd, bwd):
    bsz, lc, _ = proj_c.shape
    length = proj_x.shape[1]
    t_len = S5_CHUNK
    ncc, ncx = lc // t_len, length // t_len
    nc = ncc + ncx
    col = d_col // width
    n_state = fwd[3].shape[0]
    lre = jnp.concatenate([jnp.broadcast_to(fwd[3], (bsz, n_state)), jnp.broadcast_to(bwd[3], (bsz, n_state))])
    lim = jnp.concatenate([jnp.broadcast_to(fwd[4], (bsz, n_state)), jnp.broadcast_to(bwd[4], (bsz, n_state))])
    full = lambda x: pl.BlockSpec(x.shape, lambda c, nd=x.ndim: (0,) * nd)
    blk = lambda index: pl.BlockSpec((bsz, t_len, width), index)
    in_blocks = [blk(lambda c: (0, jnp.minimum(c, ncc - 1), col)),
                 blk(lambda c: (0, jnp.maximum(c - ncc, 0), col)),
                 blk(lambda c: (0, jnp.maximum(ncc - 1 - c, 0), col)),
                 blk(lambda c: (0, jnp.minimum(nc - 1 - c, ncx - 1), col))]
    out = jax.ShapeDtypeStruct((bsz, length, width), F32)
    consts = (fwd[0], bwd[0], fwd[1], fwd[2], bwd[1], bwd[2], lre, lim)
    return pl.pallas_call(
        functools.partial(_s5_kernel, ctx_chunks=ncc),
        name="s5_scan",
        grid=(nc,),
        in_specs=in_blocks + [full(x) for x in consts],
        out_specs=[blk(lambda c: (0, jnp.maximum(c - ncc, 0), 0)),
                   blk(lambda c: (0, jnp.minimum(nc - 1 - c, ncx - 1), 0))],
        out_shape=[out, out],
        scratch_shapes=[pltpu.VMEM((2 * n_state // LANES, t_len * 2 * bsz, LANES), F32),
                        pltpu.VMEM((2 * bsz, 2 * n_state), F32),
                        pltpu.VMEM((2, bsz * t_len, n_state // (width // S5_OUT_BLOCK)), BF16)],
        compiler_params=_params(("arbitrary",)),
    )(proj_c, proj_x, proj_c, proj_x, *consts)


def _odd_out_kernel(na_ref, yf_ref, yb_ref, d_ref, sdg_ref, dsk_ref, wglu_ref, wna_ref, wssm_ref,
                    h_ref, pg_ref, gt_ref, o_ref):
    y = yf_ref[0] + yb_ref[0] + dsk_ref[...] * d_ref[0].astype(F32)
    y = 0.5 * y * (1.0 + jnp.tanh(math.sqrt(2.0 / math.pi) * (y + 0.044715 * (y * y * y))))
    z = y * jax.nn.sigmoid(_dot(y.astype(BF16), wglu_ref[...]))
    s = (z * sdg_ref[0].astype(F32)).astype(BF16)
    out = _dot(na_ref[0], wna_ref[...]) + _dot(s, wssm_ref[...])
    _postnorm_residual(out, h_ref, pg_ref, gt_ref, o_ref)


def odd_out_proj(na, yf, yb, proj, d_col, dg_col, d_skip, w_glu, w_na, w_ssm, h, post_g, gate, tm):
    bsz, length, dm = h.shape
    na_w, ssm_w = na.shape[2], yf.shape[2]
    tok = lambda w, col=0: pl.BlockSpec((1, tm, w), lambda b, i: (b, i, col // w))
    const = lambda r, c: pl.BlockSpec((r, c), lambda b, i: (0, 0))
    return pl.pallas_call(
        _odd_out_kernel,
        name="odd_out_proj",
        grid=(bsz, length // tm),
        in_specs=[tok(na_w), tok(ssm_w), tok(ssm_w),
                  tok(ssm_w, d_col), tok(ssm_w, dg_col),
                  const(1, ssm_w), const(ssm_w, ssm_w), const(na_w, dm), const(ssm_w, dm),
                  tok(dm), const(1, dm),
                  pl.BlockSpec((1, 1, dm), lambda b, i: (b, 0, 0))],
        out_specs=tok(dm),
        out_shape=jax.ShapeDtypeStruct((bsz, length, dm), F32),
        compiler_params=_params(("parallel", "parallel")),
    )(na, yf, yb, proj, proj, d_skip.reshape(1, ssm_w), w_glu, w_na, w_ssm, h, post_g.reshape(1, dm), gate)


def _small_dft(n):
    ang = (np.outer(np.arange(n), np.arange(n)) % n) * (2.0 * math.pi / n)
    return jnp.asarray(np.concatenate([np.cos(ang), np.sin(ang)], axis=1), BF16)


def _conv_fourier_layer(h, pre_g, post_g, shift, scale, gate, w5, w_out_a, w_out_b, conv_w, conv_b,
                        ln_g, ln_b, fourier_g, cs_small, tm, tl, tdft, shared_mod):
    bsz, length, _ = h.shape
    gd = fourier_g.shape[1]
    fold = (lambda t: t.reshape(1, bsz * length, t.shape[2])) if shared_mod else (lambda t: t)
    unfold = (lambda t: t.reshape(bsz, length, t.shape[2])) if shared_mod else (lambda t: t)
    vec = (lambda v: v[:1]) if shared_mod else (lambda v: v)
    outs = even_in_proj(fold(h), pre_g, vec(shift), vec(scale), w5, fourier_g, cs_small, tm)
    a, sga, p, q, sgb = [unfold(t) for t in outs]
    a = conv_branch(a, sga, conv_w, conv_b, ln_g, ln_b, tl)
    ortho = 1.0 / math.sqrt(length * gd)
    if (length // FFT_MINOR) % SUBLANES == 0:
        f = fourier_branch_fft(p, q, sgb, ortho)
    else:
        cmat, smat = dft_matrices(length)
        f = fourier_branch(cmat, smat, p, q, sgb, ortho, tdft, tdft)
    return unfold(even_out_proj(fold(a), fold(f), w_out_a, w_out_b, fold(h), post_g, vec(gate), min(tm, 512)))


def kernel(x, c, ctx, c_ctx, pre_g, post_g, ada_w, ada_b, ab_w_in, ab_w_out, conv_w, conv_b, conv_ln_g,
           conv_ln_b, fourier_g, cd_w_in, cd_w_out, na_rpb, s5_a_re, s5_a_im, s5_log_dt, s5_b_re, s5_b_im,
           s5_c_re, s5_c_im, s5_d, s5_w_glu):
    bsz, length, d = x.shape
    lc = ctx.shape[1]
    depth = ada_w.shape[0]
    assert depth == 2 and length % (2 * GRID_W) == 0 and length // GRID_W >= NA_WIN_ROWS + 2
    assert lc % S5_CHUNK == 0 and length % S5_CHUNK == 0

    cond_rows = jnp.zeros((SUBLANES, d), F32).at[:bsz].set(c).at[bsz].set(c_ctx)
    mods = ada_modulation(cond_rows, ada_w, ada_b)

    def mod_vectors(i):
        xs = [mods[i, :bsz, k * d:(k + 1) * d][:, None, :] for k in range(3)]
        cs = [jnp.broadcast_to(mods[i, bsz, k * d:(k + 1) * d], (bsz, 1, d)) for k in range(3)]
        return xs, cs

    (sh_x, sc_x, gt_x), (sh_c, sc_c, gt_c) = mod_vectors(0)
    w_in = ab_w_in[0]
    conv_width = conv_w.shape[2]
    cw = conv_width
    fw = (w_in.shape[1] - 3 * cw) // 2
    assert cw == fw
    w5 = w_in.astype(BF16)
    w_out_a = ab_w_out[0, :cw].astype(BF16)
    w_out_b = ab_w_out[0, cw:].astype(BF16)
    cs_small = _small_dft(fourier_g.shape[2])
    layer0 = functools.partial(
        _conv_fourier_layer, pre_g=pre_g[0], post_g=post_g[0], w5=w5, w_out_a=w_out_a, w_out_b=w_out_b,
        conv_w=conv_w[0], conv_b=conv_b[0], ln_g=conv_ln_g[0], ln_b=conv_ln_b[0], fourier_g=fourier_g[0],
        cs_small=cs_small)
    h_x = layer0(x, shift=sh_x, scale=sc_x, gate=gt_x, tm=1024, tl=256, tdft=1024, shared_mod=False)
    h_c = layer0(ctx, shift=sh_c, scale=sc_c, gate=gt_c, tm=min(1024, bsz * lc), tl=lc, tdft=lc, shared_mod=True)

    (sh_x, sc_x, gt_x), (sh_c, sc_c, _) = mod_vectors(1)
    w_in = cd_w_in[0]
    ssm_w = s5_d.shape[1]
    na_w = (w_in.shape[1] - 2 * ssm_w) // 4
    g_col, d_col, dg_col = 3 * na_w, 4 * na_w, 4 * na_w + ssm_w
    gates = ((g_col, d_col), (dg_col, dg_col + ssm_w))
    col_scale = jnp.where(jnp.arange(w_in.shape[1]) < na_w, HEAD_DIM ** -0.5, 1.0).astype(F32)
    w_all = (w_in * col_scale).astype(BF16)
    proj_x = odd_in_proj(h_x, pre_g[1], sh_x, sc_x, w_all, gates, tm=1024)
    proj_c = odd_in_proj(h_c.reshape(1, bsz * lc, d), pre_g[1], sh_c[:1], sc_c[:1], w_all, gates,
                         tm=bsz * lc).reshape(bsz, lc, -1)

    bias = na_bias_table(na_rpb[0], length // GRID_W)
    na = neighbourhood_attention(proj_x, proj_c, bias, na_w, g_col)

    dirs = [_s5_direction_params(s5_a_re[0, i], s5_a_im[0, i], s5_log_dt[0, i], s5_b_re[0, i], s5_b_im[0, i],
                                 s5_c_re[0, i], s5_c_im[0, i]) for i in range(2)]
    y_f, y_b = s5_scan(proj_c, proj_x, d_col, ssm_w, *dirs)

    return odd_out_proj(na, y_f, y_b, proj_x, d_col, dg_col, s5_d[0], s5_w_glu[0].astype(BF16),
                        cd_w_out[0, :na_w].astype(BF16), cd_w_out[0, na_w:].astype(BF16),
                        h_x, post_g[1], gt_x, tm=512)
```

```python
import functools
import math

import jax
import jax.numpy as jnp
import numpy as np
from jax import lax
from jax.experimental import pallas as pl
from jax.experimental.pallas import tpu as pltpu

F32 = jnp.float32
BF16 = jnp.bfloat16

EPS = 1e-6
NEG_INF = -1e30

GRID_W = 64
CONV_K = 31
FOURIER_GROUPS = 4
HEAD_DIM = 128
NA_ROWS = 8
NA_COLS = 16
SSM_GROUP = 16
SSM_STATE = 64

LANES = 128
SUBLANES = 8
VMEM_LIMIT = 56 * 1024 * 1024

NA_PAIR = 2 * GRID_W
NA_WIN_ROWS = NA_ROWS + 2
NA_WIN = NA_WIN_ROWS * GRID_W
NA_VARIANTS = 5


def _params(sem, vmem=VMEM_LIMIT):
    return pltpu.CompilerParams(dimension_semantics=sem, vmem_limit_bytes=vmem)


def _silu(x):
    return x * jax.nn.sigmoid(x)


def _rms(x):
    return x * lax.rsqrt(jnp.mean(x * x, axis=-1, keepdims=True) + EPS)


def _dot(a, b):
    return jnp.dot(a, b, preferred_element_type=F32)


def _ada_kernel(c_ref, w_ref, b_ref, o_ref):
    cond = _silu(c_ref[...])
    o_ref[0] = _dot(cond.astype(BF16), w_ref[0].astype(BF16)) + b_ref[0]


def ada_modulation(cond_rows, ada_w, ada_b, tn=1024):
    depth, d, n = ada_w.shape
    rows = cond_rows.shape[0]
    return pl.pallas_call(
        _ada_kernel,
        name="ada_modulation",
        grid=(depth, n // tn),
        in_specs=[
            pl.BlockSpec((rows, d), lambda i, j: (0, 0)),
            pl.BlockSpec((1, d, tn), lambda i, j: (i, 0, j)),
            pl.BlockSpec((1, 1, tn), lambda i, j: (i, 0, j)),
        ],
        out_specs=pl.BlockSpec((1, rows, tn), lambda i, j: (i, 0, j)),
        out_shape=jax.ShapeDtypeStruct((depth, rows, n), F32),
        compiler_params=_params(("parallel", "parallel")),
    )(cond_rows, ada_w, ada_b.reshape(depth, 1, n))


PRENORM_ROWS = 16


def _prenorm_modulate(h_ref, g_ref, sh_ref, sc_ref, xn_ref):
    gain = g_ref[...] * (1.0 + sc_ref[0])
    shift = sh_ref[0]

    def chunk(r, carry):
        rows = pl.ds(pl.multiple_of(r * PRENORM_ROWS, PRENORM_ROWS), PRENORM_ROWS)
        xn_ref[rows, :] = (_rms(h_ref[0, rows, :]) * gain + shift).astype(BF16)
        return carry

    lax.fori_loop(0, xn_ref.shape[0] // PRENORM_ROWS, chunk, 0, unroll=4)


EVEN_PARTS = 5


def _even_in_kernel(h_ref, g_ref, sh_ref, sc_ref, w0_ref, w1_ref, w2_ref, w3_ref, w4_ref, fg_ref, cs_ref,
                    a_ref, sga_ref, p_ref, q_ref, sgb_ref, xn_ref):
    @pl.when(pl.program_id(2) == 0)
    def _():
        _prenorm_modulate(h_ref, g_ref, sh_ref, sc_ref, xn_ref)

    gd = a_ref.shape[2]
    w = jnp.concatenate([w0_ref[...], w1_ref[...], w2_ref[...], w3_ref[...], w4_ref[...]], axis=1)
    acc = _dot(xn_ref[...], w)
    part = lambda k: acc[:, k * gd:(k + 1) * gd]
    a_ref[0] = (part(0) * jax.nn.sigmoid(part(1))).astype(BF16)
    sga_ref[0] = _silu(part(2)).astype(BF16)
    bn = _rms(part(3)) * fg_ref[0]
    pq = _dot(bn.astype(BF16), cs_ref[...])
    p_ref[0] = pq[:, :gd].astype(BF16)
    q_ref[0] = pq[:, gd:].astype(BF16)
    sgb_ref[0] = _silu(part(4)).astype(BF16)


def even_in_proj(h, pre_g, shift, scale, w_in, fourier_g, cs_small, tm):
    bsz, length, d = h.shape
    gd = fourier_g.shape[1]
    width = FOURIER_GROUPS * gd
    wspecs = [pl.BlockSpec((d, gd), lambda b, i, n, k=k: (0, k * FOURIER_GROUPS + n)) for k in range(EVEN_PARTS)]
    ospec = pl.BlockSpec((1, tm, gd), lambda b, i, n: (b, i, n))
    vec = pl.BlockSpec((1, 1, d), lambda b, i, n: (b, 0, 0))
    out = jax.ShapeDtypeStruct((bsz, length, width), BF16)
    return pl.pallas_call(
        _even_in_kernel,
        name="even_in_proj",
        grid=(bsz, length // tm, FOURIER_GROUPS),
        in_specs=[
            pl.BlockSpec((1, tm, d), lambda b, i, n: (b, i, 0)),
            pl.BlockSpec((1, d), lambda b, i, n: (0, 0)),
            vec, vec,
            *wspecs,
            pl.BlockSpec((1, 1, gd), lambda b, i, n: (n, 0, 0)),
            pl.BlockSpec((gd, 2 * gd), lambda b, i, n: (0, 0)),
        ],
        out_specs=[ospec] * 5,
        out_shape=[out] * 5,
        scratch_shapes=[pltpu.VMEM((tm, d), BF16)],
        compiler_params=_params(("parallel", "parallel", "arbitrary")),
    )(h, pre_g.reshape(1, d), shift, scale, *([w_in] * EVEN_PARTS),
      fourier_g.reshape(FOURIER_GROUPS, 1, gd), cs_small)


CONV_HALO = 16
CONV_ROWS = 32


def _conv_kernel(a_ref, prev_ref, next_ref, w_ref, cb_ref, lg_ref, lb_ref, sga_ref, o_ref,
                 ext_ref, sh_ref, acc_ref):
    i = pl.program_id(1)
    last = pl.num_programs(1) - 1
    tl, width = acc_ref.shape
    ext_ref[pl.ds(CONV_HALO, tl), :] = a_ref[0].astype(F32)
    ext_ref[pl.ds(0, CONV_HALO), :] = jnp.where(i > 0, prev_ref[0].astype(F32), 0.0)
    ext_ref[pl.ds(CONV_HALO + tl, CONV_HALO), :] = jnp.where(i < last, next_ref[0].astype(F32), 0.0)
    base = CONV_HALO - CONV_K // 2
    span = sh_ref.shape[1]
    for r in range(1, SUBLANES):
        sh_ref[r - 1] = ext_ref[pl.ds(r, span), :]

    for cb in range(width // LANES):
        cols = pl.ds(cb * LANES, LANES)
        taps = [jnp.broadcast_to(w_ref[pl.ds(k, 1), cols], (SUBLANES, LANES)) for k in range(CONV_K)]
        bias = jnp.broadcast_to(cb_ref[:, cols], (SUBLANES, LANES))

        def rows(rb, carry, cols=cols, taps=taps, bias=bias):
            r0 = pl.multiple_of(rb * CONV_ROWS, CONV_ROWS)
            accs = [bias] * (CONV_ROWS // SUBLANES)
            for k in range(CONV_K):
                q, r = divmod(base + k, SUBLANES)
                src = ext_ref if r == 0 else sh_ref.at[r - 1]
                for j in range(len(accs)):
                    accs[j] = accs[j] + src[pl.ds(r0 + (q + j) * SUBLANES, SUBLANES), cols] * taps[k]
            for j, acc in enumerate(accs):
                acc_ref[pl.ds(r0 + j * SUBLANES, SUBLANES), cols] = acc
            return carry

        lax.fori_loop(0, tl // CONV_ROWS, rows, 0, unroll=2)
    x = acc_ref[...]
    mu = jnp.mean(x, axis=-1, keepdims=True)
    xc = x - mu
    var = jnp.mean(xc * xc, axis=-1, keepdims=True)
    y = xc * lax.rsqrt(var + EPS) * lg_ref[...] + lb_ref[...]
    o_ref[0] = (_silu(y) * sga_ref[0].astype(F32)).astype(BF16)


def conv_branch(a, sga, conv_w, conv_b, ln_g, ln_b, tl):
    bsz, length, width = a.shape
    hb = tl // CONV_HALO
    nh = length // CONV_HALO
    row = pl.BlockSpec((1, width), lambda b, i: (0, 0))
    main = pl.BlockSpec((1, tl, width), lambda b, i: (b, i, 0))
    return pl.pallas_call(
        _conv_kernel,
        name="conv_branch",
        grid=(bsz, length // tl),
        in_specs=[
            main,
            pl.BlockSpec((1, CONV_HALO, width), lambda b, i: (b, jnp.maximum(i * hb - 1, 0), 0)),
            pl.BlockSpec((1, CONV_HALO, width), lambda b, i: (b, jnp.minimum((i + 1) * hb, nh - 1), 0)),
            pl.BlockSpec((CONV_K, width), lambda b, i: (0, 0)),
            row, row, row,
            main,
        ],
        out_specs=main,
        out_shape=jax.ShapeDtypeStruct((bsz, length, width), BF16),
        scratch_shapes=[pltpu.VMEM((tl + 2 * CONV_HALO, width), F32),
                        pltpu.VMEM((SUBLANES - 1, tl + 2 * CONV_HALO - SUBLANES, width), F32),
                        pltpu.VMEM((tl, width), F32)],
        compiler_params=_params(("parallel", "parallel")),
    )(a, a, a, conv_w, conv_b.reshape(1, width), ln_g.reshape(1, width), ln_b.reshape(1, width), sga)


DFT_SPLIT = 64


def _dft_gen_kernel(ca_ref, sa_ref, cb_ref, sb_ref, c_ref, s_ref):
    ca, sa = ca_ref[0], sa_ref[0]
    cb, sb = cb_ref[...], sb_ref[...]
    c_ref[...] = (ca * cb - sa * sb).astype(BF16)
    s_ref[...] = (-(sa * cb + ca * sb)).astype(BF16)


def dft_matrices(length):
    na = length // DFT_SPLIT
    k = np.arange(length)[None, :]
    w = 2.0 * math.pi / length
    ang_a = ((DFT_SPLIT * np.arange(na)[:, None] * k) % length) * w
    ang_b = ((np.arange(DFT_SPLIT)[:, None] * k) % length) * w
    ca, sa = (jnp.asarray(f(ang_a).reshape(na, 1, length), F32) for f in (np.cos, np.sin))
    cb, sb = (jnp.asarray(f(ang_b), F32) for f in (np.cos, np.sin))
    tab_a = pl.BlockSpec((1, 1, length), lambda a: (a, 0, 0))
    tab_b = pl.BlockSpec((DFT_SPLIT, length), lambda a: (0, 0))
    out = pl.BlockSpec((DFT_SPLIT, length), lambda a: (a, 0))
    shp = jax.ShapeDtypeStruct((length, length), BF16)
    return pl.pallas_call(
        _dft_gen_kernel,
        name="dft_matrices",
        grid=(na,),
        in_specs=[tab_a, tab_a, tab_b, tab_b],
        out_specs=[out, out],
        out_shape=[shp, shp],
        compiler_params=_params(("parallel",)),
    )(ca, sa, cb, sb)


def _dft_kernel(c_ref, s_ref, p_ref, q_ref, sgb_ref, o_ref, acc_ref, *, scale):
    kk = pl.program_id(2)

    @pl.when(kk == 0)
    def _():
        acc_ref[...] = jnp.zeros_like(acc_ref)

    acc_ref[...] += _dot(c_ref[...], p_ref[0]) + _dot(s_ref[...], q_ref[0])

    @pl.when(kk == pl.num_programs(2) - 1)
    def _():
        o_ref[0] = (acc_ref[...] * scale * sgb_ref[0].astype(F32)).astype(BF16)


def fourier_branch(cmat, smat, p, q, sgb, scale, tm, tk):
    bsz, length, width = p.shape
    return pl.pallas_call(
        functools.partial(_dft_kernel, scale=scale),
        name="fourier_branch",
        grid=(bsz, length // tm, length // tk),
        in_specs=[
            pl.BlockSpec((tm, tk), lambda b, m, k: (m, k)),
            pl.BlockSpec((tm, tk), lambda b, m, k: (m, k)),
            pl.BlockSpec((1, tk, width), lambda b, m, k: (b, k, 0)),
            pl.BlockSpec((1, tk, width), lambda b, m, k: (b, k, 0)),
            pl.BlockSpec((1, tm, width), lambda b, m, k: (b, m, 0)),
        ],
        out_specs=pl.BlockSpec((1, tm, width), lambda b, m, k: (b, m, 0)),
        out_shape=jax.ShapeDtypeStruct((bsz, length, width), BF16),
        scratch_shapes=[pltpu.VMEM((tm, width), F32)],
        compiler_params=_params(("parallel", "parallel", "arbitrary")),
    )(cmat, smat, p, q, sgb)


FFT_MINOR = 64
FFT_COLS = 256


def _fft_kernel(p_ref, q_ref, k1a_ref, k1b_ref, twr_ref, twi_ref, hre_ref, him_ref, sgb_ref, o_ref,
                x_ref, y_ref, z_ref, *, scale):
    length, cw = p_ref.shape[1], p_ref.shape[2]
    ncb = cw // LANES
    n_m = length // FFT_MINOR
    n_a1 = FFT_MINOR // SUBLANES
    n_kb = n_m // SUBLANES
    slab = n_m * SUBLANES
    piece = SUBLANES * SUBLANES
    for cb in range(ncb):
        cols = pl.ds(cb * LANES, LANES)
        x_ref[cb] = p_ref[0, :, cols].astype(F32).reshape(n_m, FFT_MINOR, LANES)
        x_ref[ncb + cb] = q_ref[0, :, cols].astype(F32).reshape(n_m, FFT_MINOR, LANES)

    def slab_of(part, a1):
        tiles = [x_ref[part * ncb + cb, :, pl.ds(a1 * SUBLANES, SUBLANES), :].reshape(slab, LANES)
                 for cb in range(ncb)]
        return jnp.concatenate(tiles, axis=1).astype(BF16)

    for a1 in range(n_a1):
        y = _dot(k1a_ref[...], slab_of(0, a1)) + _dot(k1b_ref[...], slab_of(1, a1))
        tr, ti = twr_ref[a1], twi_ref[a1]
        for cb in range(ncb):
            yre = y[:slab, cb * LANES:(cb + 1) * LANES]
            yim = y[slab:, cb * LANES:(cb + 1) * LANES]
            zre = (tr * yre - ti * yim).astype(BF16)
            zim = (tr * yim + ti * yre).astype(BF16)
            for kb in range(n_kb):
                dst = (pl.ds(a1 * piece, piece), pl.ds(cb * LANES, LANES))
                y_ref[(kb, 0) + dst] = zre[kb * piece:(kb + 1) * piece]
                y_ref[(kb, 1) + dst] = zim[kb * piece:(kb + 1) * piece]

    for kb in range(n_kb):
        out = _dot(hre_ref[...], y_ref[kb, 0]) + _dot(him_ref[...], y_ref[kb, 1])
        z_ref[:, kb] = out.reshape(FFT_MINOR, SUBLANES, cw)

    o_ref[0] = (z_ref[...].reshape(length, cw) * scale * sgb_ref[0].astype(F32)).astype(BF16)


def _fft_tables(length):
    n_a, s = FFT_MINOR, SUBLANES
    n_m = length // n_a
    n_a1 = n_a // s
    eye = np.eye(s)
    ang_m = (np.outer(np.arange(n_m), np.arange(n_m)) % n_m) * (2.0 * math.pi / n_m)
    cm, sm = np.kron(np.cos(ang_m), eye), np.kron(np.sin(ang_m), eye)
    k1a = np.concatenate([cm, -sm], axis=0)
    k1b = np.concatenate([-sm, -cm], axis=0)
    a_full = s * np.arange(n_a1)[:, None, None] + np.arange(s)[None, None, :]
    ang_t = ((np.arange(n_m)[None, :, None] * a_full) % length) * (2.0 * math.pi / length)
    rep = lambda t: np.broadcast_to(t.reshape(n_a1, n_m * s, 1), (n_a1, n_m * s, LANES))
    twr, twi = rep(np.cos(ang_t)), rep(-np.sin(ang_t))
    ang_a = ((np.arange(n_a)[:, None, None] * a_full.reshape(1, n_a1, s)) % n_a) * (2.0 * math.pi / n_a)
    spread = lambda t: np.einsum("kas,pq->kpaqs", t, eye).reshape(n_a * s, n_a1 * s * s)
    hre, him = spread(np.cos(ang_a)), spread(np.sin(ang_a))
    return ([jnp.asarray(t, BF16) for t in (k1a, k1b)] + [jnp.asarray(t, F32) for t in (twr, twi)]
            + [jnp.asarray(t, BF16) for t in (hre, him)])


def fourier_branch_fft(p, q, sgb, scale):
    bsz, length, width = p.shape
    tables = _fft_tables(length)
    n_m = length // FFT_MINOR
    tok = pl.BlockSpec((1, length, FFT_COLS), lambda b, n: (b, 0, n))
    const = lambda t: pl.BlockSpec(t.shape, lambda b, n, nd=t.ndim: (0,) * nd)
    return pl.pallas_call(
        functools.partial(_fft_kernel, scale=scale),
        name="fourier_branch_fft",
        grid=(bsz, width // FFT_COLS),
        in_specs=[tok, tok] + [const(t) for t in tables] + [tok],
        out_specs=tok,
        out_shape=jax.ShapeDtypeStruct((bsz, length, width), BF16),
        scratch_shapes=[pltpu.VMEM((2 * FFT_COLS // LANES, n_m, FFT_MINOR, LANES), F32),
                        pltpu.VMEM((n_m // SUBLANES, 2, FFT_MINOR * SUBLANES, FFT_COLS), BF16),
                        pltpu.VMEM((FFT_MINOR, n_m // SUBLANES, SUBLANES, FFT_COLS), F32)],
        compiler_params=_params(("parallel", "parallel")),
    )(p, q, *tables, sgb)


def _postnorm_residual(y, h_ref, pg_ref, gt_ref, o_ref):
    o_ref[0] = h_ref[0] + gt_ref[0] * (_rms(y) * pg_ref[...])


def _even_out_kernel(a_ref, b_ref, wa_ref, wb_ref, h_ref, pg_ref, gt_ref, o_ref):
    y = _dot(a_ref[0], wa_ref[...]) + _dot(b_ref[0], wb_ref[...])
    _postnorm_residual(y, h_ref, pg_ref, gt_ref, o_ref)


def even_out_proj(a, b, wa, wb, h, post_g, gate, tm):
    bsz, length, d = h.shape
    width = a.shape[2]
    half = pl.BlockSpec((1, tm, width), lambda bb, i: (bb, i, 0))
    wspec = pl.BlockSpec((width, d), lambda bb, i: (0, 0))
    full = pl.BlockSpec((1, tm, d), lambda bb, i: (bb, i, 0))
    return pl.pallas_call(
        _even_out_kernel,
        name="even_out_proj",
        grid=(bsz, length // tm),
        in_specs=[half, half, wspec, wspec, full,
                  pl.BlockSpec((1, d), lambda bb, i: (0, 0)),
                  pl.BlockSpec((1, 1, d), lambda bb, i: (bb, 0, 0))],
        out_specs=full,
        out_shape=jax.ShapeDtypeStruct((bsz, length, d), F32),
        compiler_params=_params(("parallel", "parallel")),
    )(a, b, wa, wb, h, post_g.reshape(1, d), gate)


ODD_TN = 1024


ODD_PARTS = 2


def _odd_in_kernel(h_ref, g_ref, sh_ref, sc_ref, w_ref, o_ref, xn_ref, *, gate_parts):
    n = pl.program_id(2)

    @pl.when(n == 0)
    def _():
        _prenorm_modulate(h_ref, g_ref, sh_ref, sc_ref, xn_ref)

    xn = xn_ref[...]
    pw = ODD_TN // ODD_PARTS
    for j in range(ODD_PARTS):
        cols = pl.ds(j * pw, pw)
        acc = _dot(xn, w_ref[:, cols])
        tiles = [g // ODD_PARTS for g in gate_parts if g % ODD_PARTS == j]
        if tiles:
            is_gate = functools.reduce(jnp.logical_or, [n == t for t in tiles])
            acc = jnp.where(is_gate, _silu(acc), acc)
        o_ref[0, :, cols] = acc.astype(BF16)


def odd_in_proj(h, pre_g, shift, scale, w_all, gate_ranges, tm):
    bsz, length, d = h.shape
    n_all = w_all.shape[1]
    pw = ODD_TN // ODD_PARTS
    assert n_all % ODD_TN == 0 and all(s % pw == 0 and e % pw == 0 for s, e in gate_ranges)
    gate_parts = tuple(p for s, e in gate_ranges for p in range(s // pw, e // pw))
    vec = pl.BlockSpec((1, 1, d), lambda b, i, n: (b, 0, 0))
    return pl.pallas_call(
        functools.partial(_odd_in_kernel, gate_parts=gate_parts),
        name="odd_in_proj",
        grid=(bsz, length // tm, n_all // ODD_TN),
        in_specs=[pl.BlockSpec((1, tm, d), lambda b, i, n: (b, i, 0)),
                  pl.BlockSpec((1, d), lambda b, i, n: (0, 0)),
                  vec, vec,
                  pl.BlockSpec((d, ODD_TN), lambda b, i, n: (0, n))],
        out_specs=pl.BlockSpec((1, tm, ODD_TN), lambda b, i, n: (b, i, n)),
        out_shape=jax.ShapeDtypeStruct((bsz, length, n_all), BF16),
        scratch_shapes=[pltpu.VMEM((tm, d), BF16)],
        compiler_params=_params(("parallel", "parallel", "arbitrary")),
    )(h, pre_g.reshape(1, d), shift, scale, w_all)


def _na_pair_geometry(variant, rows):
    r0 = {0: 4, 1: 0, 2: 2, 3: rows - 4, 4: rows - 2}[variant]
    ws = min(max(r0 - NA_ROWS // 2, 0), rows - NA_WIN_ROWS)
    return r0, ws


def _na_bias_kernel(rpb_ref, o_ref, *, rows):
    h = pl.program_id(0)
    n_dr, n_dc = 2 * NA_ROWS - 1, 2 * NA_COLS - 1
    qc = lax.broadcasted_iota(jnp.int32, (GRID_W, LANES), 0)
    lane = lax.broadcasted_iota(jnp.int32, (GRID_W, LANES), 1)
    kc = lane % GRID_W
    hi = lane // GRID_W
    diff = kc - qc
    c_start = jnp.clip(qc - NA_COLS // 2, 0, GRID_W - NA_COLS)
    col_ok = jnp.where(kc >= c_start, 1, 0) * jnp.where(kc < c_start + NA_COLS, 1, 0)
    blocks = []
    for dr in range(n_dr):
        val = jnp.full((GRID_W, LANES), NEG_INF, F32)
        for dc in range(n_dc):
            val = jnp.where(diff == dc - (NA_COLS - 1), rpb_ref[(h * n_dr + dr) * n_dc + dc], val)
        blocks.append(jnp.where(col_ok == 1, val, NEG_INF))
    masked = jnp.full((GRID_W, LANES), NEG_INF, F32)
    for variant in range(NA_VARIANTS):
        r0, ws = _na_pair_geometry(variant, rows)
        for ri in range(2):
            r = r0 + ri
            r_start = min(max(r - NA_ROWS // 2, 0), rows - NA_ROWS)
            for wp in range(NA_WIN_ROWS // 2):
                halves = []
                for a in (ws + 2 * wp, ws + 2 * wp + 1):
                    in_window = r_start <= a < r_start + NA_ROWS
                    halves.append(blocks[a - r + NA_ROWS - 1] if in_window else masked)
                o_ref[0, variant, pl.ds(ri * GRID_W, GRID_W), pl.ds(wp * LANES, LANES)] = jnp.where(
                    hi == 1, halves[1], halves[0])


def na_bias_table(rpb, rows):
    heads = rpb.shape[0]
    return pl.pallas_call(
        functools.partial(_na_bias_kernel, rows=rows),
        name="na_bias_table",
        grid=(heads,),
        in_specs=[pl.BlockSpec(memory_space=pltpu.SMEM)],
        out_specs=pl.BlockSpec((1, NA_VARIANTS, NA_PAIR, NA_WIN), lambda h: (h, 0, 0, 0)),
        out_shape=jax.ShapeDtypeStruct((heads, NA_VARIANTS, NA_PAIR, NA_WIN), F32),
        compiler_params=_params(("parallel",)),
    )(rpb.reshape(-1))


def _na_kernel(q_ref, k_ref, v_ref, kc_ref, vc_ref, bias_ref, sg_ref, o_ref,
               vx_ref, vcx_ref, s0_ref, s1_ref, p0_ref, p1_ref, *, rows):
    npairs = rows // 2
    last = npairs - 1
    nt = (((1,), (1,)), ((), ()))
    vx_ref[:, :HEAD_DIM] = v_ref[0]
    vx_ref[:, HEAD_DIM:] = jnp.ones((vx_ref.shape[0], HEAD_DIM), BF16)
    vcx_ref[:, :HEAD_DIM] = vc_ref[0]
    vcx_ref[:, HEAD_DIM:] = jnp.ones((vcx_ref.shape[0], HEAD_DIM), BF16)

    def window(pr):
        ws = min(max(2 * pr - NA_ROWS // 2, 0), rows - NA_WIN_ROWS)
        return pl.ds(ws * GRID_W, NA_WIN)

    def qrows(pr):
        return pl.ds(pr * NA_PAIR, NA_PAIR)

    def scores(pr, s_ref):
        variant = {0: 1, 1: 2, last - 1: 3, last: 4}.get(pr, 0)
        q = q_ref[0, qrows(pr), :]
        s_ref[:, :NA_WIN] = lax.dot_general(q, k_ref[0, window(pr), :], nt,
                                            preferred_element_type=F32) + bias_ref[0, variant]
        s_ref[:, NA_WIN:] = lax.dot_general(q, kc_ref[0], nt, preferred_element_type=F32)

    def probs(s_ref, p_ref):
        s = s_ref[...]
        p_ref[...] = jnp.exp(s - jnp.max(s, axis=-1, keepdims=True)).astype(BF16)

    def values(pr, p_ref):
        acc = _dot(p_ref[:, :NA_WIN], vx_ref[window(pr), :]) + _dot(p_ref[:, NA_WIN:], vcx_ref[...])
        o = acc[:, :HEAD_DIM] / acc[:, HEAD_DIM:]
        o_ref[0, qrows(pr), :] = (o * sg_ref[0, qrows(pr), :].astype(F32)).astype(BF16)

    bufs = ((s0_ref, p0_ref), (s1_ref, p1_ref))
    scores(0, s0_ref)
    scores(1, s1_ref)
    probs(s0_ref, p0_ref)
    for i in range(npairs):
        s_cur, p_cur = bufs[i % 2]
        s_nxt, p_nxt = bufs[(i + 1) % 2]
        values(i, p_cur)
        if i + 1 < npairs:
            probs(s_nxt, p_nxt)
        if i + 2 < npairs:
            scores(i + 2, s_cur)


def neighbourhood_attention(proj, proj_c, bias, na_w, gate_col):
    bsz, length, _ = proj.shape
    lc = proj_c.shape[1]
    heads = na_w // HEAD_DIM
    rows = length // GRID_W
    seq = lambda off: pl.BlockSpec((1, length, HEAD_DIM), lambda h, b: (b, 0, off + h))
    cseq = lambda off: pl.BlockSpec((1, lc, HEAD_DIM), lambda h, b: (b, 0, off + h))
    return pl.pallas_call(
        functools.partial(_na_kernel, rows=rows),
        name="neighbourhood_attention",
        grid=(heads, bsz),
        in_specs=[seq(0), seq(heads), seq(2 * heads), cseq(heads), cseq(2 * heads),
                  pl.BlockSpec((1, NA_VARIANTS, NA_PAIR, NA_WIN), lambda h, b: (h, 0, 0, 0)),
                  seq(gate_col // HEAD_DIM)],
        out_specs=seq(0),
        out_shape=jax.ShapeDtypeStruct((bsz, length, na_w), BF16),
        scratch_shapes=[pltpu.VMEM((length, 2 * HEAD_DIM), BF16), pltpu.VMEM((lc, 2 * HEAD_DIM), BF16),
                        pltpu.VMEM((NA_PAIR, NA_WIN + lc), F32), pltpu.VMEM((NA_PAIR, NA_WIN + lc), F32),
                        pltpu.VMEM((NA_PAIR, NA_WIN + lc), BF16), pltpu.VMEM((NA_PAIR, NA_WIN + lc), BF16)],
        compiler_params=_params(("parallel", "parallel")),
    )(proj, proj, proj, proj_c, proj_c, bias, proj)


S5_CHUNK = 128
S5_IN_BLOCK = LANES
S5_OUT_BLOCK = 256
S5_SCAN_COLS = 1024


def _s5_kernel(ufc_ref, ufx_ref, ubc_ref, ubx_ref, bf_ref, bb_ref, crf_ref, cif_ref, crb_ref, cib_ref,
               lre_ref, lim_ref, yf_ref, yb_ref, buf_ref, state_ref, stage_ref, *, ctx_chunks):
    bsz, t_len, width = ufx_ref.shape
    n_state = lre_ref.shape[1]
    in_ctx = pl.program_id(0) < ctx_chunks
    n_in = width // S5_IN_BLOCK
    blk = n_state // n_in
    rows8 = 2 * bsz

    @pl.when(pl.program_id(0) == 0)
    def _():
        state_ref[...] = jnp.zeros_like(state_ref)

    nlb = n_state // LANES
    per_in = blk // LANES
    ti = lax.broadcasted_iota(jnp.int32, (t_len, t_len), 0)
    tj = lax.broadcasted_iota(jnp.int32, (t_len, t_len), 1)
    rev = jnp.where(ti + tj == t_len - 1, 1.0, 0.0).astype(BF16)
    u_fwd = jnp.where(in_ctx, ufc_ref[...], ufx_ref[...]).reshape(bsz * t_len, width)
    u_nat = jnp.where(in_ctx, ubc_ref[...], ubx_ref[...])
    u_bwd = jnp.concatenate([_dot(rev, u_nat[b]).astype(BF16) for b in range(bsz)], axis=0)
    for di, (u, w_ref) in enumerate(((u_fwd, bf_ref), (u_bwd, bb_ref))):
        for kb in range(n_in):
            res = _dot(u[:, kb * S5_IN_BLOCK:(kb + 1) * S5_IN_BLOCK], w_ref[kb])
            for b in range(bsz):
                rsel = pl.ds(di * bsz + b, t_len, stride=rows8)
                for c in range(2 * per_in):
                    dst = (c // per_in) * nlb + kb * per_in + c % per_in
                    buf_ref[dst, rsel, :] = res[b * t_len:(b + 1) * t_len, c * LANES:(c + 1) * LANES]

    per = S5_SCAN_COLS // LANES
    for cbk in range(nlb // per):
        blocks = [cbk * per + i for i in range(per)]
        lre = [lre_ref[:, pl.ds(c * LANES, LANES)] for c in blocks]
        lim = [lim_ref[:, pl.ds(c * LANES, LANES)] for c in blocks]

        def step(t, carry, blocks=blocks, lre=lre, lim=lim):
            row = pl.ds(pl.multiple_of(t * rows8, rows8), rows8)
            new = []
            for i, c in enumerate(blocks):
                hre, him = carry[2 * i], carry[2 * i + 1]
                nre = lre[i] * hre - lim[i] * him + buf_ref[c, row, :]
                nim = lre[i] * him + lim[i] * hre + buf_ref[nlb + c, row, :]
                buf_ref[c, row, :] = nre
                buf_ref[nlb + c, row, :] = nim
                new += [nre, nim]
            return tuple(new)

        init = tuple(state_ref[:, pl.ds(part * n_state + c * LANES, LANES)] for c in blocks for part in range(2))
        fin = lax.fori_loop(0, t_len, step, init, unroll=2)
        for i, c in enumerate(blocks):
            state_ref[:, pl.ds(c * LANES, LANES)] = fin[2 * i]
            state_ref[:, pl.ds(n_state + c * LANES, LANES)] = fin[2 * i + 1]

    n_out = width // S5_OUT_BLOCK
    per_out = nlb // n_out
    for di, (y_ref, cr_ref, ci_ref) in enumerate(((yf_ref, crf_ref, cif_ref), (yb_ref, crb_ref, cib_ref))):
        for j in range(n_out):
            for part in range(2):
                for b in range(bsz):
                    rsel = pl.ds(di * bsz + b, t_len, stride=rows8)
                    for c in range(per_out):
                        stage_ref[part, pl.ds(b * t_len, t_len), pl.ds(c * LANES, LANES)] = buf_ref[
                            part * nlb + j * per_out + c, rsel, :].astype(BF16)
            y = _dot(stage_ref[0], cr_ref[j]) + _dot(stage_ref[1], ci_ref[j])
            for b in range(bsz):
                yb = y[b * t_len:(b + 1) * t_len]
                if di == 1:
                    hi = yb.astype(BF16)
                    r1 = yb - hi.astype(F32)
                    mid = r1.astype(BF16)
                    lo = (r1 - mid.astype(F32)).astype(BF16)
                    yb = _dot(rev, hi) + _dot(rev, mid) + _dot(rev, lo)
                y_ref[b, :, pl.ds(j * S5_OUT_BLOCK, S5_OUT_BLOCK)] = yb


def _block_diag(x):
    nblk, g, r, c = x.shape
    eye = jnp.eye(g, dtype=x.dtype)
    return jnp.einsum("kgrc,gh->kgrhc", x, eye).reshape(nblk, g * r, g * c)


def _s5_direction_params(a_re, a_im, log_dt, b_re, b_im, c_re, c_im):
    groups, n_p = a_re.shape
    lam = lax.complex(a_re.astype(F32), a_im.astype(F32))
    dt = jnp.exp(log_dt.astype(F32))[:, None]
    lam_bar = jnp.exp(lam * dt)
    b_bar = ((lam_bar - 1.0) / lam)[..., None] * lax.complex(b_re.astype(F32), b_im.astype(F32))
    gi = S5_IN_BLOCK // SSM_GROUP
    bt = jnp.swapaxes(b_bar, 1, 2).reshape(groups // gi, gi, SSM_GROUP, n_p)
    b_mat = jnp.concatenate([_block_diag(jnp.real(bt)), _block_diag(jnp.imag(bt))], axis=-1)
    go = S5_OUT_BLOCK // SSM_GROUP
    ct = lambda c: jnp.swapaxes(c.astype(F32), 1, 2).reshape(groups // go, go, n_p, SSM_GROUP)
    return (b_mat.astype(BF16), _block_diag(ct(c_re)).astype(BF16), _block_diag(-ct(c_im)).astype(BF16),
            jnp.real(lam_bar).reshape(-1), jnp.imag(lam_bar).reshape(-1))


def s5_scan(proj_c, proj_x, d_col, width, fwd, bwd):
    bsz, lc, _ = proj_c.shape
    length = proj_x.shape[1]
    t_len = S5_CHUNK
    ncc, ncx = lc // t_len, length // t_len
    nc = ncc + ncx
    col = d_col // width
    n_state = fwd[3].shape[0]
    lre = jnp.concatenate([jnp.broadcast_to(fwd[3], (bsz, n_state)), jnp.broadcast_to(bwd[3], (bsz, n_state))])
    lim = jnp.concatenate([jnp.broadcast_to(fwd[4], (bsz, n_state)), jnp.broadcast_to(bwd[4], (bsz, n_state))])
    full = lambda x: pl.BlockSpec(x.shape, lambda c, nd=x.ndim: (0,) * nd)
    blk = lambda index: pl.BlockSpec((bsz, t_len, width), index)
    in_blocks = [blk(lambda c: (0, jnp.minimum(c, ncc - 1), col)),
                 blk(lambda c: (0, jnp.maximum(c - ncc, 0), col)),
                 blk(lambda c: (0, jnp.maximum(ncc - 1 - c, 0), col)),
                 blk(lambda c: (0, jnp.minimum(nc - 1 - c, ncx - 1), col))]
    out = jax.ShapeDtypeStruct((bsz, length, width), F32)
    consts = (fwd[0], bwd[0], fwd[1], fwd[2], bwd[1], bwd[2], lre, lim)
    return pl.pallas_call(
        functools.partial(_s5_kernel, ctx_chunks=ncc),
        name="s5_scan",
        grid=(nc,),
        in_specs=in_blocks + [full(x) for x in consts],
        out_specs=[blk(lambda c: (0, jnp.maximum(c - ncc, 0), 0)),
                   blk(lambda c: (0, jnp.minimum(nc - 1 - c, ncx - 1), 0))],
        out_shape=[out, out],
        scratch_shapes=[pltpu.VMEM((2 * n_state // LANES, t_len * 2 * bsz, LANES), F32),
                        pltpu.VMEM((2 * bsz, 2 * n_state), F32),
                        pltpu.VMEM((2, bsz * t_len, n_state // (width // S5_OUT_BLOCK)), BF16)],
        compiler_params=_params(("arbitrary",)),
    )(proj_c, proj_x, proj_c, proj_x, *consts)


def _odd_out_kernel(na_ref, yf_ref, yb_ref, d_ref, sdg_ref, dsk_ref, wglu_ref, wna_ref, wssm_ref,
                    h_ref, pg_ref, gt_ref, o_ref):
    y = yf_ref[0] + yb_ref[0] + dsk_ref[...] * d_ref[0].astype(F32)
    y = 0.5 * y * (1.0 + jnp.tanh(math.sqrt(2.0 / math.pi) * (y + 0.044715 * (y * y * y))))
    z = y * jax.nn.sigmoid(_dot(y.astype(BF16), wglu_ref[...]))
    s = (z * sdg_ref[0].astype(F32)).astype(BF16)
    out = _dot(na_ref[0], wna_ref[...]) + _dot(s, wssm_ref[...])
    _postnorm_residual(out, h_ref, pg_ref, gt_ref, o_ref)


def odd_out_proj(na, yf, yb, proj, d_col, dg_col, d_skip, w_glu, w_na, w_ssm, h, post_g, gate, tm):
    bsz, length, dm = h.shape
    na_w, ssm_w = na.shape[2], yf.shape[2]
    tok = lambda w, col=0: pl.BlockSpec((1, tm, w), lambda b, i: (b, i, col // w))
    const = lambda r, c: pl.BlockSpec((r, c), lambda b, i: (0, 0))
    return pl.pallas_call(
        _odd_out_kernel,
        name="odd_out_proj",
        grid=(bsz, length // tm),
        in_specs=[tok(na_w), tok(ssm_w), tok(ssm_w),
                  tok(ssm_w, d_col), tok(ssm_w, dg_col),
                  const(1, ssm_w), const(ssm_w, ssm_w), const(na_w, dm), const(ssm_w, dm),
                  tok(dm), const(1, dm),
                  pl.BlockSpec((1, 1, dm), lambda b, i: (b, 0, 0))],
        out_specs=tok(dm),
        out_shape=jax.ShapeDtypeStruct((bsz, length, dm), F32),
        compiler_params=_params(("parallel", "parallel")),
    )(na, yf, yb, proj, proj, d_skip.reshape(1, ssm_w), w_glu, w_na, w_ssm, h, post_g.reshape(1, dm), gate)


def _small_dft(n):
    ang = (np.outer(np.arange(n), np.arange(n)) % n) * (2.0 * math.pi / n)
    return jnp.asarray(np.concatenate([np.cos(ang), np.sin(ang)], axis=1), BF16)


def _conv_fourier_layer(h, pre_g, post_g, shift, scale, gate, w5, w_out_a, w_out_b, conv_w, conv_b,
                        ln_g, ln_b, fourier_g, cs_small, tm, tl, tdft, shared_mod):
    bsz, length, _ = h.shape
    gd = fourier_g.shape[1]
    fold = (lambda t: t.reshape(1, bsz * length, t.shape[2])) if shared_mod else (lambda t: t)
    unfold = (lambda t: t.reshape(bsz, length, t.shape[2])) if shared_mod else (lambda t: t)
    vec = (lambda v: v[:1]) if shared_mod else (lambda v: v)
    outs = even_in_proj(fold(h), pre_g, vec(shift), vec(scale), w5, fourier_g, cs_small, tm)
    a, sga, p, q, sgb = [unfold(t) for t in outs]
    a = conv_branch(a, sga, conv_w, conv_b, ln_g, ln_b, tl)
    ortho = 1.0 / math.sqrt(length * gd)
    if (length // FFT_MINOR) % SUBLANES == 0:
        f = fourier_branch_fft(p, q, sgb, ortho)
    else:
        cmat, smat = dft_matrices(length)
        f = fourier_branch(cmat, smat, p, q, sgb, ortho, tdft, tdft)
    return unfold(even_out_proj(fold(a), fold(f), w_out_a, w_out_b, fold(h), post_g, vec(gate), min(tm, 512)))


def kernel(x, c, ctx, c_ctx, pre_g, post_g, ada_w, ada_b, ab_w_in, ab_w_out, conv_w, conv_b, conv_ln_g,
           conv_ln_b, fourier_g, cd_w_in, cd_w_out, na_rpb, s5_a_re, s5_a_im, s5_log_dt, s5_b_re, s5_b_im,
           s5_c_re, s5_c_im, s5_d, s5_w_glu):
    bsz, length, d = x.shape
    lc = ctx.shape[1]
    depth = ada_w.shape[0]
    assert depth == 2 and length % (2 * GRID_W) == 0 and length // GRID_W >= NA_WIN_ROWS + 2
    assert lc % S5_CHUNK == 0 and length % S5_CHUNK == 0

    cond_rows = jnp.zeros((SUBLANES, d), F32).at[:bsz].set(c).at[bsz].set(c_ctx)
    mods = ada_modulation(cond_rows, ada_w, ada_b)

    def mod_vectors(i):
        xs = [mods[i, :bsz, k * d:(k + 1) * d][:, None, :] for k in range(3)]
        cs = [jnp.broadcast_to(mods[i, bsz, k * d:(k + 1) * d], (bsz, 1, d)) for k in range(3)]
        return xs, cs

    (sh_x, sc_x, gt_x), (sh_c, sc_c, gt_c) = mod_vectors(0)
    w_in = ab_w_in[0]
    conv_width = conv_w.shape[2]
    cw = conv_width
    fw = (w_in.shape[1] - 3 * cw) // 2
    assert cw == fw
    w5 = w_in.astype(BF16)
    w_out_a = ab_w_out[0, :cw].astype(BF16)
    w_out_b = ab_w_out[0, cw:].astype(BF16)
    cs_small = _small_dft(fourier_g.shape[2])
    layer0 = functools.partial(
        _conv_fourier_layer, pre_g=pre_g[0], post_g=post_g[0], w5=w5, w_out_a=w_out_a, w_out_b=w_out_b,
        conv_w=conv_w[0], conv_b=conv_b[0], ln_g=conv_ln_g[0], ln_b=conv_ln_b[0], fourier_g=fourier_g[0],
        cs_small=cs_small)
    h_x = layer0(x, shift=sh_x, scale=sc_x, gate=gt_x, tm=1024, tl=256, tdft=1024, shared_mod=False)
    h_c = layer0(ctx, shift=sh_c, scale=sc_c, gate=gt_c, tm=min(1024, bsz * lc), tl=lc, tdft=lc, shared_mod=True)

    (sh_x, sc_x, gt_x), (sh_c, sc_c, _) = mod_vectors(1)
    w_in = cd_w_in[0]
    ssm_w = s5_d.shape[1]
    na_w = (w_in.shape[1] - 2 * ssm_w) // 4
    g_col, d_col, dg_col = 3 * na_w, 4 * na_w, 4 * na_w + ssm_w
    gates = ((g_col, d_col), (dg_col, dg_col + ssm_w))
    col_scale = jnp.where(jnp.arange(w_in.shape[1]) < na_w, HEAD_DIM ** -0.5, 1.0).astype(F32)
    w_all = (w_in * col_scale).astype(BF16)
    proj_x = odd_in_proj(h_x, pre_g[1], sh_x, sc_x, w_all, gates, tm=1024)
    proj_c = odd_in_proj(h_c.reshape(1, bsz * lc, d), pre_g[1], sh_c[:1], sc_c[:1], w_all, gates,
                         tm=bsz * lc).reshape(bsz, lc, -1)

    bias = na_bias_table(na_rpb[0], length // GRID_W)
    na = neighbourhood_attention(proj_x, proj_c, bias, na_w, g_col)

    dirs = [_s5_direction_params(s5_a_re[0, i], s5_a_im[0, i], s5_log_dt[0, i], s5_b_re[0, i], s5_b_im[0, i],
                                 s5_c_re[0, i], s5_c_im[0, i]) for i in range(2)]
    y_f, y_b = s5_scan(proj_c, proj_x, d_col, ssm_w, *dirs)

    return odd_out_proj(na, y_f, y_b, proj_x, d_col, dg_col, s5_d[0], s5_w_glu[0].astype(BF16),
                        cd_w_out[0, :na_w].astype(BF16), cd_w_out[0, na_w:].astype(BF16),
                        h_x, post_g[1], gt_x, tm=512)
```

```python
import functools
import math

import jax
import jax.numpy as jnp
import numpy as np
from jax import lax
from jax.experimental import pallas as pl
from jax.experimental.pallas import tpu as pltpu

F32 = jnp.float32
BF16 = jnp.bfloat16

EPS = 1e-6
NEG_INF = -1e30

GRID_W = 64
CONV_K = 31
FOURIER_GROUPS = 4
HEAD_DIM = 128
NA_ROWS = 8
NA_COLS = 16
SSM_GROUP = 16
SSM_STATE = 64

LANES = 128
SUBLANES = 8
VMEM_LIMIT = 56 * 1024 * 1024

NA_PAIR = 2 * GRID_W
NA_WIN_ROWS = NA_ROWS + 2
NA_WIN = NA_WIN_ROWS * GRID_W
NA_VARIANTS = 5


def _params(sem, vmem=VMEM_LIMIT):
    return pltpu.CompilerParams(dimension_semantics=sem, vmem_limit_bytes=vmem)


def _silu(x):
    return x * jax.nn.sigmoid(x)


def _rms(x):
    return x * lax.rsqrt(jnp.mean(x * x, axis=-1, keepdims=True) + EPS)


def _dot(a, b):
    return jnp.dot(a, b, preferred_element_type=F32)


def _ada_kernel(c_ref, w_ref, b_ref, o_ref):
    cond = _silu(c_ref[...])
    o_ref[0] = _dot(cond.astype(BF16), w_ref[0].astype(BF16)) + b_ref[0]


def ada_modulation(cond_rows, ada_w, ada_b, tn=1024):
    depth, d, n = ada_w.shape
    rows = cond_rows.shape[0]
    return pl.pallas_call(
        _ada_kernel,
        name="ada_modulation",
        grid=(depth, n // tn),
        in_specs=[
            pl.BlockSpec((rows, d), lambda i, j: (0, 0)),
            pl.BlockSpec((1, d, tn), lambda i, j: (i, 0, j)),
            pl.BlockSpec((1, 1, tn), lambda i, j: (i, 0, j)),
        ],
        out_specs=pl.BlockSpec((1, rows, tn), lambda i, j: (i, 0, j)),
        out_shape=jax.ShapeDtypeStruct((depth, rows, n), F32),
        compiler_params=_params(("parallel", "parallel")),
    )(cond_rows, ada_w, ada_b.reshape(depth, 1, n))


PRENORM_ROWS = 16


def _prenorm_modulate(h_ref, g_ref, sh_ref, sc_ref, xn_ref):
    gain = g_ref[...] * (1.0 + sc_ref[0])
    shift = sh_ref[0]

    def chunk(r, carry):
        rows = pl.ds(pl.multiple_of(r * PRENORM_ROWS, PRENORM_ROWS), PRENORM_ROWS)
        xn_ref[rows, :] = (_rms(h_ref[0, rows, :]) * gain + shift).astype(BF16)
        return carry

    lax.fori_loop(0, xn_ref.shape[0] // PRENORM_ROWS, chunk, 0, unroll=4)


EVEN_PARTS = 5


def _even_in_kernel(h_ref, g_ref, sh_ref, sc_ref, w0_ref, w1_ref, w2_ref, w3_ref, w4_ref, fg_ref, cs_ref,
                    a_ref, sga_ref, p_ref, q_ref, sgb_ref, xn_ref):
    @pl.when(pl.program_id(2) == 0)
    def _():
        _prenorm_modulate(h_ref, g_ref, sh_ref, sc_ref, xn_ref)

    gd = a_ref.shape[2]
    w = jnp.concatenate([w0_ref[...], w1_ref[...], w2_ref[...], w3_ref[...], w4_ref[...]], axis=1)
    acc = _dot(xn_ref[...], w)
    part = lambda k: acc[:, k * gd:(k + 1) * gd]
    a_ref[0] = (part(0) * jax.nn.sigmoid(part(1))).astype(BF16)
    sga_ref[0] = _silu(part(2)).astype(BF16)
    bn = _rms(part(3)) * fg_ref[0]
    pq = _dot(bn.astype(BF16), cs_ref[...])
    p_ref[0] = pq[:, :gd].astype(BF16)
    q_ref[0] = pq[:, gd:].astype(BF16)
    sgb_ref[0] = _silu(part(4)).astype(BF16)


def even_in_proj(h, pre_g, shift, scale, w_in, fourier_g, cs_small, tm):
    bsz, length, d = h.shape
    gd = fourier_g.shape[1]
    width = FOURIER_GROUPS * gd
    wspecs = [pl.BlockSpec((d, gd), lambda b, i, n, k=k: (0, k * FOURIER_GROUPS + n)) for k in range(EVEN_PARTS)]
    ospec = pl.BlockSpec((1, tm, gd), lambda b, i, n: (b, i, n))
    vec = pl.BlockSpec((1, 1, d), lambda b, i, n: (b, 0, 0))
    out = jax.ShapeDtypeStruct((bsz, length, width), BF16)
    return pl.pallas_call(
        _even_in_kernel,
        name="even_in_proj",
        grid=(bsz, length // tm, FOURIER_GROUPS),
        in_specs=[
            pl.BlockSpec((1, tm, d), lambda b, i, n: (b, i, 0)),
            pl.BlockSpec((1, d), lambda b, i, n: (0, 0)),
            vec, vec,
            *wspecs,
            pl.BlockSpec((1, 1, gd), lambda b, i, n: (n, 0, 0)),
            pl.BlockSpec((gd, 2 * gd), lambda b, i, n: (0, 0)),
        ],
        out_specs=[ospec] * 5,
        out_shape=[out] * 5,
        scratch_shapes=[pltpu.VMEM((tm, d), BF16)],
        compiler_params=_params(("parallel", "parallel", "arbitrary")),
    )(h, pre_g.reshape(1, d), shift, scale, *([w_in] * EVEN_PARTS),
      fourier_g.reshape(FOURIER_GROUPS, 1, gd), cs_small)


CONV_HALO = 16
CONV_ROWS = 32


def _conv_kernel(a_ref, prev_ref, next_ref, w_ref, cb_ref, lg_ref, lb_ref, sga_ref, o_ref,
                 ext_ref, sh_ref, acc_ref):
    i = pl.program_id(1)
    last = pl.num_programs(1) - 1
    tl, width = acc_ref.shape
    ext_ref[pl.ds(CONV_HALO, tl), :] = a_ref[0].astype(F32)
    ext_ref[pl.ds(0, CONV_HALO), :] = jnp.where(i > 0, prev_ref[0].astype(F32), 0.0)
    ext_ref[pl.ds(CONV_HALO + tl, CONV_HALO), :] = jnp.where(i < last, next_ref[0].astype(F32), 0.0)
    base = CONV_HALO - CONV_K // 2
    span = sh_ref.shape[1]
    for r in range(1, SUBLANES):
        sh_ref[r - 1] = ext_ref[pl.ds(r, span), :]

    for cb in range(width // LANES):
        cols = pl.ds(cb * LANES, LANES)
        taps = [jnp.broadcast_to(w_ref[pl.ds(k, 1), cols], (SUBLANES, LANES)) for k in range(CONV_K)]
        bias = jnp.broadcast_to(cb_ref[:, cols], (SUBLANES, LANES))

        def rows(rb, carry, cols=cols, taps=taps, bias=bias):
            r0 = pl.multiple_of(rb * CONV_ROWS, CONV_ROWS)
            accs = [bias] * (CONV_ROWS // SUBLANES)
            for k in range(CONV_K):
                q, r = divmod(base + k, SUBLANES)
                src = ext_ref if r == 0 else sh_ref.at[r - 1]
                for j in range(len(accs)):
                    accs[j] = accs[j] + src[pl.ds(r0 + (q + j) * SUBLANES, SUBLANES), cols] * taps[k]
            for j, acc in enumerate(accs):
                acc_ref[pl.ds(r0 + j * SUBLANES, SUBLANES), cols] = acc
            return carry

        lax.fori_loop(0, tl // CONV_ROWS, rows, 0, unroll=2)
    x = acc_ref[...]
    mu = jnp.mean(x, axis=-1, keepdims=True)
    xc = x - mu
    var = jnp.mean(xc * xc, axis=-1, keepdims=True)
    y = xc * lax.rsqrt(var + EPS) * lg_ref[...] + lb_ref[...]
    o_ref[0] = (_silu(y) * sga_ref[0].astype(F32)).astype(BF16)


def conv_branch(a, sga, conv_w, conv_b, ln_g, ln_b, tl):
    bsz, length, width = a.shape
    hb = tl // CONV_HALO
    nh = length // CONV_HALO
    row = pl.BlockSpec((1, width), lambda b, i: (0, 0))
    main = pl.BlockSpec((1, tl, width), lambda b, i: (b, i, 0))
    return pl.pallas_call(
        _conv_kernel,
        name="conv_branch",
        grid=(bsz, length // tl),
        in_specs=[
            main,
            pl.BlockSpec((1, CONV_HALO, width), lambda b, i: (b, jnp.maximum(i * hb - 1, 0), 0)),
            pl.BlockSpec((1, CONV_HALO, width), lambda b, i: (b, jnp.minimum((i + 1) * hb, nh - 1), 0)),
            pl.BlockSpec((CONV_K, width), lambda b, i: (0, 0)),
            row, row, row,
            main,
        ],
        out_specs=main,
        out_shape=jax.ShapeDtypeStruct((bsz, length, width), BF16),
        scratch_shapes=[pltpu.VMEM((tl + 2 * CONV_HALO, width), F32),
                        pltpu.VMEM((SUBLANES - 1, tl + 2 * CONV_HALO - SUBLANES, width), F32),
                        pltpu.VMEM((tl, width), F32)],
        compiler_params=_params(("parallel", "parallel")),
    )(a, a, a, conv_w, conv_b.reshape(1, width), ln_g.reshape(1, width), ln_b.reshape(1, width), sga)


DFT_SPLIT = 64


def _dft_gen_kernel(ca_ref, sa_ref, cb_ref, sb_ref, c_ref, s_ref):
    ca, sa = ca_ref[0], sa_ref[0]
    cb, sb = cb_ref[...], sb_ref[...]
    c_ref[...] = (ca * cb - sa * sb).astype(BF16)
    s_ref[...] = (-(sa * cb + ca * sb)).astype(BF16)


def dft_matrices(length):
    na = length // DFT_SPLIT
    k = np.arange(length)[None, :]
    w = 2.0 * math.pi / length
    ang_a = ((DFT_SPLIT * np.arange(na)[:, None] * k) % length) * w
    ang_b = ((np.arange(DFT_SPLIT)[:, None] * k) % length) * w
    ca, sa = (jnp.asarray(f(ang_a).reshape(na, 1, length), F32) for f in (np.cos, np.sin))
    cb, sb = (jnp.asarray(f(ang_b), F32) for f in (np.cos, np.sin))
    tab_a = pl.BlockSpec((1, 1, length), lambda a: (a, 0, 0))
    tab_b = pl.BlockSpec((DFT_SPLIT, length), lambda a: (0, 0))
    out = pl.BlockSpec((DFT_SPLIT, length), lambda a: (a, 0))
    shp = jax.ShapeDtypeStruct((length, length), BF16)
    return pl.pallas_call(
        _dft_gen_kernel,
        name="dft_matrices",
        grid=(na,),
        in_specs=[tab_a, tab_a, tab_b, tab_b],
        out_specs=[out, out],
        out_shape=[shp, shp],
        compiler_params=_params(("parallel",)),
    )(ca, sa, cb, sb)


def _dft_kernel(c_ref, s_ref, p_ref, q_ref, sgb_ref, o_ref, acc_ref, *, scale):
    kk = pl.program_id(2)

    @pl.when(kk == 0)
    def _():
        acc_ref[...] = jnp.zeros_like(acc_ref)

    acc_ref[...] += _dot(c_ref[...], p_ref[0]) + _dot(s_ref[...], q_ref[0])

    @pl.when(kk == pl.num_programs(2) - 1)
    def _():
        o_ref[0] = (acc_ref[...] * scale * sgb_ref[0].astype(F32)).astype(BF16)


def fourier_branch(cmat, smat, p, q, sgb, scale, tm, tk):
    bsz, length, width = p.shape
    return pl.pallas_call(
        functools.partial(_dft_kernel, scale=scale),
        name="fourier_branch",
        grid=(bsz, length // tm, length // tk),
        in_specs=[
            pl.BlockSpec((tm, tk), lambda b, m, k: (m, k)),
            pl.BlockSpec((tm, tk), lambda b, m, k: (m, k)),
            pl.BlockSpec((1, tk, width), lambda b, m, k: (b, k, 0)),
            pl.BlockSpec((1, tk, width), lambda b, m, k: (b, k, 0)),
            pl.BlockSpec((1, tm, width), lambda b, m, k: (b, m, 0)),
        ],
        out_specs=pl.BlockSpec((1, tm, width), lambda b, m, k: (b, m, 0)),
        out_shape=jax.ShapeDtypeStruct((bsz, length, width), BF16),
        scratch_shapes=[pltpu.VMEM((tm, width), F32)],
        compiler_params=_params(("parallel", "parallel", "arbitrary")),
    )(cmat, smat, p, q, sgb)


FFT_MINOR = 64
FFT_COLS = 256


def _fft_kernel(p_ref, q_ref, k1a_ref, k1b_ref, twr_ref, twi_ref, hre_ref, him_ref, sgb_ref, o_ref,
                x_ref, y_ref, z_ref, *, scale):
    length, cw = p_ref.shape[1], p_ref.shape[2]
    ncb = cw // LANES
    n_m = length // FFT_MINOR
    n_a1 = FFT_MINOR // SUBLANES
    n_kb = n_m // SUBLANES
    slab = n_m * SUBLANES
    piece = SUBLANES * SUBLANES
    for cb in range(ncb):
        cols = pl.ds(cb * LANES, LANES)
        x_ref[cb] = p_ref[0, :, cols].astype(F32).reshape(n_m, FFT_MINOR, LANES)
        x_ref[ncb + cb] = q_ref[0, :, cols].astype(F32).reshape(n_m, FFT_MINOR, LANES)

    def slab_of(part, a1):
        tiles = [x_ref[part * ncb + cb, :, pl.ds(a1 * SUBLANES, SUBLANES), :].reshape(slab, LANES)
                 for cb in range(ncb)]
        return jnp.concatenate(tiles, axis=1).astype(BF16)

    for a1 in range(n_a1):
        y = _dot(k1a_ref[...], slab_of(0, a1)) + _dot(k1b_ref[...], slab_of(1, a1))
        tr, ti = twr_ref[a1], twi_ref[a1]
        for cb in range(ncb):
            yre = y[:slab, cb * LANES:(cb + 1) * LANES]
            yim = y[slab:, cb * LANES:(cb + 1) * LANES]
            zre = (tr * yre - ti * yim).astype(BF16)
            zim = (tr * yim + ti * yre).astype(BF16)
            for kb in range(n_kb):
                dst = (pl.ds(a1 * piece, piece), pl.ds(cb * LANES, LANES))
                y_ref[(kb, 0) + dst] = zre[kb * piece:(kb + 1) * piece]
                y_ref[(kb, 1) + dst] = zim[kb * piece:(kb + 1) * piece]

    for kb in range(n_kb):
        out = _dot(hre_ref[...], y_ref[kb, 0]) + _dot(him_ref[...], y_ref[kb, 1])
        z_ref[:, kb] = out.reshape(FFT_MINOR, SUBLANES, cw)

    o_ref[0] = (z_ref[...].reshape(length, cw) * scale * sgb_ref[0].astype(F32)).astype(BF16)


def _fft_tables(length):
    n_a, s = FFT_MINOR, SUBLANES
    n_m = length // n_a
    n_a1 = n_a // s
    eye = np.eye(s)
    ang_m = (np.outer(np.arange(n_m), np.arange(n_m)) % n_m) * (2.0 * math.pi / n_m)
    cm, sm = np.kron(np.cos(ang_m), eye), np.kron(np.sin(ang_m), eye)
    k1a = np.concatenate([cm, -sm], axis=0)
    k1b = np.concatenate([-sm, -cm], axis=0)
    a_full = s * np.arange(n_a1)[:, None, None] + np.arange(s)[None, None, :]
    ang_t = ((np.arange(n_m)[None, :, None] * a_full) % length) * (2.0 * math.pi / length)
    rep = lambda t: np.broadcast_to(t.reshape(n_a1, n_m * s, 1), (n_a1, n_m * s, LANES))
    twr, twi = rep(np.cos(ang_t)), rep(-np.sin(ang_t))
    ang_a = ((np.arange(n_a)[:, None, None] * a_full.reshape(1, n_a1, s)) % n_a) * (2.0 * math.pi / n_a)
    spread = lambda t: np.einsum("kas,pq->kpaqs", t, eye).reshape(n_a * s, n_a1 * s * s)
    hre, him = spread(np.cos(ang_a)), spread(np.sin(ang_a))
    return ([jnp.asarray(t, BF16) for t in (k1a, k1b)] + [jnp.asarray(t, F32) for t in (twr, twi)]
            + [jnp.asarray(t, BF16) for t in (hre, him)])


def fourier_branch_fft(p, q, sgb, scale):
    bsz, length, width = p.shape
    tables = _fft_tables(length)
    n_m = length // FFT_MINOR
    tok = pl.BlockSpec((1, length, FFT_COLS), lambda b, n: (b, 0, n))
    const = lambda t: pl.BlockSpec(t.shape, lambda b, n, nd=t.ndim: (0,) * nd)
    return pl.pallas_call(
        functools.partial(_fft_kernel, scale=scale),
        name="fourier_branch_fft",
        grid=(bsz, width // FFT_COLS),
        in_specs=[tok, tok] + [const(t) for t in tables] + [tok],
        out_specs=tok,
        out_shape=jax.ShapeDtypeStruct((bsz, length, width), BF16),
        scratch_shapes=[pltpu.VMEM((2 * FFT_COLS // LANES, n_m, FFT_MINOR, LANES), F32),
                        pltpu.VMEM((n_m // SUBLANES, 2, FFT_MINOR * SUBLANES, FFT_COLS), BF16),
                        pltpu.VMEM((FFT_MINOR, n_m // SUBLANES, SUBLANES, FFT_COLS), F32)],
        compiler_params=_params(("parallel", "parallel")),
    )(p, q, *tables, sgb)


def _postnorm_residual(y, h_ref, pg_ref, gt_ref, o_ref):
    o_ref[0] = h_ref[0] + gt_ref[0] * (_rms(y) * pg_ref[...])


def _even_out_kernel(a_ref, b_ref, wa_ref, wb_ref, h_ref, pg_ref, gt_ref, o_ref):
    y = _dot(a_ref[0], wa_ref[...]) + _dot(b_ref[0], wb_ref[...])
    _postnorm_residual(y, h_ref, pg_ref, gt_ref, o_ref)


def even_out_proj(a, b, wa, wb, h, post_g, gate, tm):
    bsz, length, d = h.shape
    width = a.shape[2]
    half = pl.BlockSpec((1, tm, width), lambda bb, i: (bb, i, 0))
    wspec = pl.BlockSpec((width, d), lambda bb, i: (0, 0))
    full = pl.BlockSpec((1, tm, d), lambda bb, i: (bb, i, 0))
    return pl.pallas_call(
        _even_out_kernel,
        name="even_out_proj",
        grid=(bsz, length // tm),
        in_specs=[half, half, wspec, wspec, full,
                  pl.BlockSpec((1, d), lambda bb, i: (0, 0)),
                  pl.BlockSpec((1, 1, d), lambda bb, i: (bb, 0, 0))],
        out_specs=full,
        out_shape=jax.ShapeDtypeStruct((bsz, length, d), F32),
        compiler_params=_params(("parallel", "parallel")),
    )(a, b, wa, wb, h, post_g.reshape(1, d), gate)


ODD_TN = 1024


ODD_PARTS = 2


def _odd_in_kernel(h_ref, g_ref, sh_ref, sc_ref, w_ref, o_ref, xn_ref, *, gate_parts):
    n = pl.program_id(2)

    @pl.when(n == 0)
    def _():
        _prenorm_modulate(h_ref, g_ref, sh_ref, sc_ref, xn_ref)

    xn = xn_ref[...]
    pw = ODD_TN // ODD_PARTS
    for j in range(ODD_PARTS):
        cols = pl.ds(j * pw, pw)
        acc = _dot(xn, w_ref[:, cols])
        tiles = [g // ODD_PARTS for g in gate_parts if g % ODD_PARTS == j]
        if tiles:
            is_gate = functools.reduce(jnp.logical_or, [n == t for t in tiles])
            acc = jnp.where(is_gate, _silu(acc), acc)
        o_ref[0, :, cols] = acc.astype(BF16)


def odd_in_proj(h, pre_g, shift, scale, w_all, gate_ranges, tm):
    bsz, length, d = h.shape
    n_all = w_all.shape[1]
    pw = ODD_TN // ODD_PARTS
    assert n_all % ODD_TN == 0 and all(s % pw == 0 and e % pw == 0 for s, e in gate_ranges)
    gate_parts = tuple(p for s, e in gate_ranges for p in range(s // pw, e // pw))
    vec = pl.BlockSpec((1, 1, d), lambda b, i, n: (b, 0, 0))
    return pl.pallas_call(
        functools.partial(_odd_in_kernel, gate_parts=gate_parts),
        name="odd_in_proj",
        grid=(bsz, length // tm, n_all // ODD_TN),
        in_specs=[pl.BlockSpec((1, tm, d), lambda b, i, n: (b, i, 0)),
                  pl.BlockSpec((1, d), lambda b, i, n: (0, 0)),
                  vec, vec,
                  pl.BlockSpec((d, ODD_TN), lambda b, i, n: (0, n))],
        out_specs=pl.BlockSpec((1, tm, ODD_TN), lambda b, i, n: (b, i, n)),
        out_shape=jax.ShapeDtypeStruct((bsz, length, n_all), BF16),
        scratch_shapes=[pltpu.VMEM((tm, d), BF16)],
        compiler_params=_params(("parallel", "parallel", "arbitrary")),
    )(h, pre_g.reshape(1, d), shift, scale, w_all)


def _na_pair_geometry(variant, rows):
    r0 = {0: 4, 1: 0, 2: 2, 3: rows - 4, 4: rows - 2}[variant]
    ws = min(max(r0 - NA_ROWS // 2, 0), rows - NA_WIN_ROWS)
    return r0, ws


def _na_bias_kernel(rpb_ref, o_ref, *, rows):
    h = pl.program_id(0)
    n_dr, n_dc = 2 * NA_ROWS - 1, 2 * NA_COLS - 1
    qc = lax.broadcasted_iota(jnp.int32, (GRID_W, LANES), 0)
    lane = lax.broadcasted_iota(jnp.int32, (GRID_W, LANES), 1)
    kc = lane % GRID_W
    hi = lane // GRID_W
    diff = kc - qc
    c_start = jnp.clip(qc - NA_COLS // 2, 0, GRID_W - NA_COLS)
    col_ok = jnp.where(kc >= c_start, 1, 0) * jnp.where(kc < c_start + NA_COLS, 1, 0)
    blocks = []
    for dr in range(n_dr):
        val = jnp.full((GRID_W, LANES), NEG_INF, F32)
        for dc in range(n_dc):
            val = jnp.where(diff == dc - (NA_COLS - 1), rpb_ref[(h * n_dr + dr) * n_dc + dc], val)
        blocks.append(jnp.where(col_ok == 1, val, NEG_INF))
    masked = jnp.full((GRID_W, LANES), NEG_INF, F32)
    for variant in range(NA_VARIANTS):
        r0, ws = _na_pair_geometry(variant, rows)
        for ri in range(2):
            r = r0 + ri
            r_start = min(max(r - NA_ROWS // 2, 0), rows - NA_ROWS)
            for wp in range(NA_WIN_ROWS // 2):
                halves = []
                for a in (ws + 2 * wp, ws + 2 * wp + 1):
                    in_window = r_start <= a < r_start + NA_ROWS
                    halves.append(blocks[a - r + NA_ROWS - 1] if in_window else masked)
                o_ref[0, variant, pl.ds(ri * GRID_W, GRID_W), pl.ds(wp * LANES, LANES)] = jnp.where(
                    hi == 1, halves[1], halves[0])


def na_bias_table(rpb, rows):
    heads = rpb.shape[0]
    return pl.pallas_call(
        functools.partial(_na_bias_kernel, rows=rows),
        name="na_bias_table",
        grid=(heads,),
        in_specs=[pl.BlockSpec(memory_space=pltpu.SMEM)],
        out_specs=pl.BlockSpec((1, NA_VARIANTS, NA_PAIR, NA_WIN), lambda h: (h, 0, 0, 0)),
        out_shape=jax.ShapeDtypeStruct((heads, NA_VARIANTS, NA_PAIR, NA_WIN), F32),
        compiler_params=_params(("parallel",)),
    )(rpb.reshape(-1))


def _na_kernel(q_ref, k_ref, v_ref, kc_ref, vc_ref, bias_ref, sg_ref, o_ref,
               vx_ref, vcx_ref, s0_ref, s1_ref, p0_ref, p1_ref, *, rows):
    npairs = rows // 2
    last = npairs - 1
    nt = (((1,), (1,)), ((), ()))
    vx_ref[:, :HEAD_DIM] = v_ref[0]
    vx_ref[:, HEAD_DIM:] = jnp.ones((vx_ref.shape[0], HEAD_DIM), BF16)
    vcx_ref[:, :HEAD_DIM] = vc_ref[0]
    vcx_ref[:, HEAD_DIM:] = jnp.ones((vcx_ref.shape[0], HEAD_DIM), BF16)

    def window(pr):
        ws = min(max(2 * pr - NA_ROWS // 2, 0), rows - NA_WIN_ROWS)
        return pl.ds(ws * GRID_W, NA_WIN)

    def qrows(pr):
        return pl.ds(pr * NA_PAIR, NA_PAIR)

    def scores(pr, s_ref):
        variant = {0: 1, 1: 2, last - 1: 3, last: 4}.get(pr, 0)
        q = q_ref[0, qrows(pr), :]
        s_ref[:, :NA_WIN] = lax.dot_general(q, k_ref[0, window(pr), :], nt,
                                            preferred_element_type=F32) + bias_ref[0, variant]
        s_ref[:, NA_WIN:] = lax.dot_general(q, kc_ref[0], nt, preferred_element_type=F32)

    def probs(s_ref, p_ref):
        s = s_ref[...]
        p_ref[...] = jnp.exp(s - jnp.max(s, axis=-1, keepdims=True)).astype(BF16)

    def values(pr, p_ref):
        acc = _dot(p_ref[:, :NA_WIN], vx_ref[window(pr), :]) + _dot(p_ref[:, NA_WIN:], vcx_ref[...])
        o = acc[:, :HEAD_DIM] / acc[:, HEAD_DIM:]
        o_ref[0, qrows(pr), :] = (o * sg_ref[0, qrows(pr), :].astype(F32)).astype(BF16)

    bufs = ((s0_ref, p0_ref), (s1_ref, p1_ref))
    scores(0, s0_ref)
    scores(1, s1_ref)
    probs(s0_ref, p0_ref)
    for i in range(npairs):
        s_cur, p_cur = bufs[i % 2]
        s_nxt, p_nxt = bufs[(i + 1) % 2]
        values(i, p_cur)
        if i + 1 < npairs:
            probs(s_nxt, p_nxt)
        if i + 2 < npairs:
            scores(i + 2, s_cur)


def neighbourhood_attention(proj, proj_c, bias, na_w, gate_col):
    bsz, length, _ = proj.shape
    lc = proj_c.shape[1]
    heads = na_w // HEAD_DIM
    rows = length // GRID_W
    seq = lambda off: pl.BlockSpec((1, length, HEAD_DIM), lambda h, b: (b, 0, off + h))
    cseq = lambda off: pl.BlockSpec((1, lc, HEAD_DIM), lambda h, b: (b, 0, off + h))
    return pl.pallas_call(
        functools.partial(_na_kernel, rows=rows),
        name="neighbourhood_attention",
        grid=(heads, bsz),
        in_specs=[seq(0), seq(heads), seq(2 * heads), cseq(heads), cseq(2 * heads),
                  pl.BlockSpec((1, NA_VARIANTS, NA_PAIR, NA_WIN), lambda h, b: (h, 0, 0, 0)),
                  seq(gate_col // HEAD_DIM)],
        out_specs=seq(0),
        out_shape=jax.ShapeDtypeStruct((bsz, length, na_w), BF16),
        scratch_shapes=[pltpu.VMEM((length, 2 * HEAD_DIM), BF16), pltpu.VMEM((lc, 2 * HEAD_DIM), BF16),
                        pltpu.VMEM((NA_PAIR, NA_WIN + lc), F32), pltpu.VMEM((NA_PAIR, NA_WIN + lc), F32),
                        pltpu.VMEM((NA_PAIR, NA_WIN + lc), BF16), pltpu.VMEM((NA_PAIR, NA_WIN + lc), BF16)],
        compiler_params=_params(("parallel", "parallel")),
    )(proj, proj, proj, proj_c, proj_c, bias, proj)


S5_CHUNK = 128
S5_IN_BLOCK = LANES
S5_OUT_BLOCK = 256
S5_SCAN_COLS = 1024


def _s5_kernel(ufc_ref, ufx_ref, ubc_ref, ubx_ref, bf_ref, bb_ref, crf_ref, cif_ref, crb_ref, cib_ref,
               lre_ref, lim_ref, yf_ref, yb_ref, buf_ref, state_ref, stage_ref, *, ctx_chunks):
    bsz, t_len, width = ufx_ref.shape
    n_state = lre_ref.shape[1]
    in_ctx = pl.program_id(0) < ctx_chunks
    n_in = width // S5_IN_BLOCK
    blk = n_state // n_in
    rows8 = 2 * bsz

    @pl.when(pl.program_id(0) == 0)
    def _():
        state_ref[...] = jnp.zeros_like(state_ref)

    nlb = n_state // LANES
    per_in = blk // LANES
    ti = lax.broadcasted_iota(jnp.int32, (t_len, t_len), 0)
    tj = lax.broadcasted_iota(jnp.int32, (t_len, t_len), 1)
    rev = jnp.where(ti + tj == t_len - 1, 1.0, 0.0).astype(BF16)
    u_fwd = jnp.where(in_ctx, ufc_ref[...], ufx_ref[...]).reshape(bsz * t_len, width)
    u_nat = jnp.where(in_ctx, ubc_ref[...], ubx_ref[...])
    u_bwd = jnp.concatenate([_dot(rev, u_nat[b]).astype(BF16) for b in range(bsz)], axis=0)
    sides = ((u_fwd, bf_ref, yf_ref, crf_ref, cif_ref), (u_bwd, bb_ref, yb_ref, crb_ref, cib_ref))

    def project(kb):
        for di, (u, w_ref, _, _, _) in enumerate(sides):
            res = _dot(u[:, kb * S5_IN_BLOCK:(kb + 1) * S5_IN_BLOCK], w_ref[kb])
            for b in range(bsz):
                rsel = pl.ds(di * bsz + b, t_len, stride=rows8)
                for c in range(2 * per_in):
                    dst = (c // per_in) * nlb + kb * per_in + c % per_in
                    buf_ref[dst, rsel, :] = res[b * t_len:(b + 1) * t_len, c * LANES:(c + 1) * LANES]

    per = S5_SCAN_COLS // LANES

    def scan(group, t0, t1, carry):
        blocks = [group * per + i for i in range(per)]
        lre = [lre_ref[:, pl.ds(c * LANES, LANES)] for c in blocks]
        lim = [lim_ref[:, pl.ds(c * LANES, LANES)] for c in blocks]
        if carry is None:
            carry = [state_ref[:, pl.ds(part * n_state + c * LANES, LANES)] for c in blocks for part in range(2)]
        for t in range(t0, t1):
            row = pl.ds(t * rows8, rows8)
            for i, c in enumerate(blocks):
                hre, him = carry[2 * i], carry[2 * i + 1]
                carry[2 * i] = lre[i] * hre - lim[i] * him + buf_ref[c, row, :]
                carry[2 * i + 1] = lre[i] * him + lim[i] * hre + buf_ref[nlb + c, row, :]
                buf_ref[c, row, :] = carry[2 * i]
                buf_ref[nlb + c, row, :] = carry[2 * i + 1]
        if t1 == t_len:
            for i, c in enumerate(blocks):
                state_ref[:, pl.ds(c * LANES, LANES)] = carry[2 * i]
                state_ref[:, pl.ds(n_state + c * LANES, LANES)] = carry[2 * i + 1]
        return carry

    n_out = width // S5_OUT_BLOCK
    per_out = nlb // n_out

    def readout(di, j):
        _, _, y_ref, cr_ref, ci_ref = sides[di]
        stage = stage_ref.at[di * n_out + j]
        for part in range(2):
            for b in range(bsz):
                rsel = pl.ds(di * bsz + b, t_len, stride=rows8)
                for c in range(per_out):
                    stage[part, pl.ds(b * t_len, t_len), pl.ds(c * LANES, LANES)] = buf_ref[
                        part * nlb + j * per_out + c, rsel, :].astype(BF16)
        y = _dot(stage[0], cr_ref[j]) + _dot(stage[1], ci_ref[j])
        for b in range(bsz):
            yb = y[b * t_len:(b + 1) * t_len]
            if di == 1:
                hi = yb.astype(BF16)
                r1 = yb - hi.astype(F32)
                mid = r1.astype(BF16)
                lo = (r1 - mid.astype(F32)).astype(BF16)
                yb = _dot(rev, hi) + _dot(rev, mid) + _dot(rev, lo)
            y_ref[b, :, pl.ds(j * S5_OUT_BLOCK, S5_OUT_BLOCK)] = yb

    assert n_in == 4 and nlb // per == 2 and n_out == 2 and per_out == per
    half = t_len // 2
    project(0)
    project(1)
    carry = scan(0, 0, half, None)
    project(2)
    scan(0, half, t_len, carry)
    project(3)
    carry = scan(1, 0, half, None)
    readout(0, 0)
    scan(1, half, t_len, carry)
    readout(1, 0)
    readout(0, 1)
    readout(1, 1)


def _block_diag(x):
    nblk, g, r, c = x.shape
    eye = jnp.eye(g, dtype=x.dtype)
    return jnp.einsum("kgrc,gh->kgrhc", x, eye).reshape(nblk, g * r, g * c)


def _s5_direction_params(a_re, a_im, log_dt, b_re, b_im, c_re, c_im):
    groups, n_p = a_re.shape
    lam = lax.complex(a_re.astype(F32), a_im.astype(F32))
    dt = jnp.exp(log_dt.astype(F32))[:, None]
    lam_bar = jnp.exp(lam * dt)
    b_bar = ((lam_bar - 1.0) / lam)[..., None] * lax.complex(b_re.astype(F32), b_im.astype(F32))
    gi = S5_IN_BLOCK // SSM_GROUP
    bt = jnp.swapaxes(b_bar, 1, 2).reshape(groups // gi, gi, SSM_GROUP, n_p)
    b_mat = jnp.concatenate([_block_diag(jnp.real(bt)), _block_diag(jnp.imag(bt))], axis=-1)
    go = S5_OUT_BLOCK // SSM_GROUP
    ct = lambda c: jnp.swapaxes(c.astype(F32), 1, 2).reshape(groups // go, go, n_p, SSM_GROUP)
    return (b_mat.astype(BF16), _block_diag(ct(c_re)).astype(BF16), _block_diag(-ct(c_im)).astype(BF16),
            jnp.real(lam_bar).reshape(-1), jnp.imag(lam_bar).reshape(-1))


def s5_scan(proj_c, proj_x, d_col, width, fwd, bwd):
    bsz, lc, _ = proj_c.shape
    length = proj_x.shape[1]
    t_len = S5_CHUNK
    ncc, ncx = lc // t_len, length // t_len
    nc = ncc + ncx
    col = d_col // width
    n_state = fwd[3].shape[0]
    lre = jnp.concatenate([jnp.broadcast_to(fwd[3], (bsz, n_state)), jnp.broadcast_to(bwd[3], (bsz, n_state))])
    lim = jnp.concatenate([jnp.broadcast_to(fwd[4], (bsz, n_state)), jnp.broadcast_to(bwd[4], (bsz, n_state))])
    full = lambda x: pl.BlockSpec(x.shape, lambda c, nd=x.ndim: (0,) * nd)
    blk = lambda index: pl.BlockSpec((bsz, t_len, width), index)
    in_blocks = [blk(lambda c: (0, jnp.minimum(c, ncc - 1), col)),
                 blk(lambda c: (0, jnp.maximum(c - ncc, 0), col)),
                 blk(lambda c: (0, jnp.maximum(ncc - 1 - c, 0), col)),
                 blk(lambda c: (0, jnp.minimum(nc - 1 - c, ncx - 1), col))]
    out = jax.ShapeDtypeStruct((bsz, length, width), F32)
    consts = (fwd[0], bwd[0], fwd[1], fwd[2], bwd[1], bwd[2], lre, lim)
    return pl.pallas_call(
        functools.partial(_s5_kernel, ctx_chunks=ncc),
        name="s5_scan",
        grid=(nc,),
        in_specs=in_blocks + [full(x) for x in consts],
        out_specs=[blk(lambda c: (0, jnp.maximum(c - ncc, 0), 0)),
                   blk(lambda c: (0, jnp.minimum(nc - 1 - c, ncx - 1), 0))],
        out_shape=[out, out],
        scratch_shapes=[pltpu.VMEM((2 * n_state // LANES, t_len * 2 * bsz, LANES), F32),
                        pltpu.VMEM((2 * bsz, 2 * n_state), F32),
                        pltpu.VMEM((2 * (width // S5_OUT_BLOCK), 2, bsz * t_len,
                                    n_state // (width // S5_OUT_BLOCK)), BF16)],
        compiler_params=_params(("arbitrary",)),
    )(proj_c, proj_x, proj_c, proj_x, *consts)


def _odd_out_kernel(na_ref, yf_ref, yb_ref, d_ref, sdg_ref, dsk_ref, wglu_ref, wna_ref, wssm_ref,
                    h_ref, pg_ref, gt_ref, o_ref):
    y = yf_ref[0] + yb_ref[0] + dsk_ref[...] * d_ref[0].astype(F32)
    y = 0.5 * y * (1.0 + jnp.tanh(math.sqrt(2.0 / math.pi) * (y + 0.044715 * (y * y * y))))
    z = y * jax.nn.sigmoid(_dot(y.astype(BF16), wglu_ref[...]))
    s = (z * sdg_ref[0].astype(F32)).astype(BF16)
    out = _dot(na_ref[0], wna_ref[...]) + _dot(s, wssm_ref[...])
    _postnorm_residual(out, h_ref, pg_ref, gt_ref, o_ref)


def odd_out_proj(na, yf, yb, proj, d_col, dg_col, d_skip, w_glu, w_na, w_ssm, h, post_g, gate, tm):
    bsz, length, dm = h.shape
    na_w, ssm_w = na.shape[2], yf.shape[2]
    tok = lambda w, col=0: pl.BlockSpec((1, tm, w), lambda b, i: (b, i, col // w))
    const = lambda r, c: pl.BlockSpec((r, c), lambda b, i: (0, 0))
    return pl.pallas_call(
        _odd_out_kernel,
        name="odd_out_proj",
        grid=(bsz, length // tm),
        in_specs=[tok(na_w), tok(ssm_w), tok(ssm_w),
                  tok(ssm_w, d_col), tok(ssm_w, dg_col),
                  const(1, ssm_w), const(ssm_w, ssm_w), const(na_w, dm), const(ssm_w, dm),
                  tok(dm), const(1, dm),
                  pl.BlockSpec((1, 1, dm), lambda b, i: (b, 0, 0))],
        out_specs=tok(dm),
        out_shape=jax.ShapeDtypeStruct((bsz, length, dm), F32),
        compiler_params=_params(("parallel", "parallel")),
    )(na, yf, yb, proj, proj, d_skip.reshape(1, ssm_w), w_glu, w_na, w_ssm, h, post_g.reshape(1, dm), gate)


def _small_dft(n):
    ang = (np.outer(np.arange(n), np.arange(n)) % n) * (2.0 * math.pi / n)
    return jnp.asarray(np.concatenate([np.cos(ang), np.sin(ang)], axis=1), BF16)


def _conv_fourier_layer(h, pre_g, post_g, shift, scale, gate, w5, w_out_a, w_out_b, conv_w, conv_b,
                        ln_g, ln_b, fourier_g, cs_small, tm, tl, tdft, shared_mod):
    bsz, length, _ = h.shape
    gd = fourier_g.shape[1]
    fold = (lambda t: t.reshape(1, bsz * length, t.shape[2])) if shared_mod else (lambda t: t)
    unfold = (lambda t: t.reshape(bsz, length, t.shape[2])) if shared_mod else (lambda t: t)
    vec = (lambda v: v[:1]) if shared_mod else (lambda v: v)
    outs = even_in_proj(fold(h), pre_g, vec(shift), vec(scale), w5, fourier_g, cs_small, tm)
    a, sga, p, q, sgb = [unfold(t) for t in outs]
    a = conv_branch(a, sga, conv_w, conv_b, ln_g, ln_b, tl)
    ortho = 1.0 / math.sqrt(length * gd)
    if (length // FFT_MINOR) % SUBLANES == 0:
        f = fourier_branch_fft(p, q, sgb, ortho)
    else:
        cmat, smat = dft_matrices(length)
        f = fourier_branch(cmat, smat, p, q, sgb, ortho, tdft, tdft)
    return unfold(even_out_proj(fold(a), fold(f), w_out_a, w_out_b, fold(h), post_g, vec(gate), min(tm, 512)))


def kernel(x, c, ctx, c_ctx, pre_g, post_g, ada_w, ada_b, ab_w_in, ab_w_out, conv_w, conv_b, conv_ln_g,
           conv_ln_b, fourier_g, cd_w_in, cd_w_out, na_rpb, s5_a_re, s5_a_im, s5_log_dt, s5_b_re, s5_b_im,
           s5_c_re, s5_c_im, s5_d, s5_w_glu):
    bsz, length, d = x.shape
    lc = ctx.shape[1]
    depth = ada_w.shape[0]
    assert depth == 2 and length % (2 * GRID_W) == 0 and length // GRID_W >= NA_WIN_ROWS + 2
    assert lc % S5_CHUNK == 0 and length % S5_CHUNK == 0

    cond_rows = jnp.zeros((SUBLANES, d), F32).at[:bsz].set(c).at[bsz].set(c_ctx)
    mods = ada_modulation(cond_rows, ada_w, ada_b)

    def mod_vectors(i):
        xs = [mods[i, :bsz, k * d:(k + 1) * d][:, None, :] for k in range(3)]
        cs = [jnp.broadcast_to(mods[i, bsz, k * d:(k + 1) * d], (bsz, 1, d)) for k in range(3)]
        return xs, cs

    (sh_x, sc_x, gt_x), (sh_c, sc_c, gt_c) = mod_vectors(0)
    w_in = ab_w_in[0]
    conv_width = conv_w.shape[2]
    cw = conv_width
    fw = (w_in.shape[1] - 3 * cw) // 2
    assert cw == fw
    w5 = w_in.astype(BF16)
    w_out_a = ab_w_out[0, :cw].astype(BF16)
    w_out_b = ab_w_out[0, cw:].astype(BF16)
    cs_small = _small_dft(fourier_g.shape[2])
    layer0 = functools.partial(
        _conv_fourier_layer, pre_g=pre_g[0], post_g=post_g[0], w5=w5, w_out_a=w_out_a, w_out_b=w_out_b,
        conv_w=conv_w[0], conv_b=conv_b[0], ln_g=conv_ln_g[0], ln_b=conv_ln_b[0], fourier_g=fourier_g[0],
        cs_small=cs_small)
    h_x = layer0(x, shift=sh_x, scale=sc_x, gate=gt_x, tm=1024, tl=256, tdft=1024, shared_mod=False)
    h_c = layer0(ctx, shift=sh_c, scale=sc_c, gate=gt_c, tm=min(1024, bsz * lc), tl=lc, tdft=lc, shared_mod=True)

    (sh_x, sc_x, gt_x), (sh_c, sc_c, _) = mod_vectors(1)
    w_in = cd_w_in[0]
    ssm_w = s5_d.shape[1]
    na_w = (w_in.shape[1] - 2 * ssm_w) // 4
    g_col, d_col, dg_col = 3 * na_w, 4 * na_w, 4 * na_w + ssm_w
    gates = ((g_col, d_col), (dg_col, dg_col + ssm_w))
    col_scale = jnp.where(jnp.arange(w_in.shape[1]) < na_w, HEAD_DIM ** -0.5, 1.0).astype(F32)
    w_all = (w_in * col_scale).astype(BF16)
    proj_x = odd_in_proj(h_x, pre_g[1], sh_x, sc_x, w_all, gates, tm=1024)
    proj_c = odd_in_proj(h_c.reshape(1, bsz * lc, d), pre_g[1], sh_c[:1], sc_c[:1], w_all, gates,
                         tm=bsz * lc).reshape(bsz, lc, -1)

    bias = na_bias_table(na_rpb[0], length // GRID_W)
    na = neighbourhood_attention(proj_x, proj_c, bias, na_w, g_col)

    dirs = [_s5_direction_params(s5_a_re[0, i], s5_a_im[0, i], s5_log_dt[0, i], s5_b_re[0, i], s5_b_im[0, i],
                                 s5_c_re[0, i], s5_c_im[0, i]) for i in range(2)]
    y_f, y_b = s5_scan(proj_c, proj_x, d_col, ssm_w, *dirs)

    return odd_out_proj(na, y_f, y_b, proj_x, d_col, dg_col, s5_d[0], s5_w_glu[0].astype(BF16),
                        cd_w_out[0, :na_w].astype(BF16), cd_w_out[0, na_w:].astype(BF16),
                        h_x, post_g[1], gt_x, tm=512)
```

```python
import functools
import math

import jax
import jax.numpy as jnp
import numpy as np
from jax import lax
from jax.experimental import pallas as pl
from jax.experimental.pallas import tpu as pltpu

F32 = jnp.float32
BF16 = jnp.bfloat16

EPS = 1e-6
NEG_INF = -1e30

GRID_W = 64
CONV_K = 31
FOURIER_GROUPS = 4
HEAD_DIM = 128
NA_ROWS = 8
NA_COLS = 16
SSM_GROUP = 16
SSM_STATE = 64

LANES = 128
SUBLANES = 8
VMEM_LIMIT = 56 * 1024 * 1024

IN_PROJ_ROWS = 1024
OUT_PROJ_ROWS = 512
CONV_TILE_ROWS = 256
ADA_COLS = 1024


def _token_tile(n_tokens, cap):
    tile = min(cap, n_tokens)
    assert n_tokens % tile == 0
    return tile

NA_PAIR = 2 * GRID_W
NA_WIN_ROWS = NA_ROWS + 2
NA_WIN = NA_WIN_ROWS * GRID_W
NA_VARIANTS = 5
NA_HEADS_PER_STEP = 2


def _params(sem, vmem=VMEM_LIMIT):
    return pltpu.CompilerParams(dimension_semantics=sem, vmem_limit_bytes=vmem)


def _silu(x):
    return x * jax.nn.sigmoid(x)


def _rms(x):
    return x * lax.rsqrt(jnp.mean(x * x, axis=-1, keepdims=True) + EPS)


def _dot(a, b):
    return jnp.dot(a, b, preferred_element_type=F32)


def _ada_kernel(c_ref, w_ref, b_ref, o_ref):
    cond = _silu(c_ref[...])
    o_ref[0] = _dot(cond.astype(BF16), w_ref[0].astype(BF16)) + b_ref[0]


def ada_modulation(cond_rows, ada_w, ada_b):
    depth, d, n = ada_w.shape
    tn = _token_tile(n, ADA_COLS)
    rows = cond_rows.shape[0]
    return pl.pallas_call(
        _ada_kernel,
        name="ada_modulation",
        grid=(depth, n // tn),
        in_specs=[
            pl.BlockSpec((rows, d), lambda i, j: (0, 0)),
            pl.BlockSpec((1, d, tn), lambda i, j: (i, 0, j)),
            pl.BlockSpec((1, 1, tn), lambda i, j: (i, 0, j)),
        ],
        out_specs=pl.BlockSpec((1, rows, tn), lambda i, j: (i, 0, j)),
        out_shape=jax.ShapeDtypeStruct((depth, rows, n), F32),
        compiler_params=_params(("parallel", "parallel")),
    )(cond_rows, ada_w, ada_b.reshape(depth, 1, n))


PRENORM_ROWS = 16


def _prenorm_modulate(h_ref, g_ref, sh_ref, sc_ref, xn_ref):
    gain = g_ref[...] * (1.0 + sc_ref[0])
    shift = sh_ref[0]

    def chunk(r, carry):
        rows = pl.ds(pl.multiple_of(r * PRENORM_ROWS, PRENORM_ROWS), PRENORM_ROWS)
        xn_ref[rows, :] = (_rms(h_ref[0, rows, :]) * gain + shift).astype(BF16)
        return carry

    lax.fori_loop(0, xn_ref.shape[0] // PRENORM_ROWS, chunk, 0, unroll=4)


EVEN_PARTS = 5


def _even_in_kernel(h_ref, g_ref, sh_ref, sc_ref, w0_ref, w1_ref, w2_ref, w3_ref, w4_ref, fg_ref, cs_ref,
                    a_ref, sga_ref, p_ref, q_ref, sgb_ref, xn_ref):
    @pl.when(pl.program_id(2) == 0)
    def _():
        _prenorm_modulate(h_ref, g_ref, sh_ref, sc_ref, xn_ref)

    gd = a_ref.shape[2]
    w = jnp.concatenate([w0_ref[...], w1_ref[...], w2_ref[...], w3_ref[...], w4_ref[...]], axis=1)
    acc = _dot(xn_ref[...], w)
    part = lambda k: acc[:, k * gd:(k + 1) * gd]
    a_ref[0] = (part(0) * jax.nn.sigmoid(part(1))).astype(BF16)
    sga_ref[0] = _silu(part(2)).astype(BF16)
    bn = _rms(part(3)) * fg_ref[0]
    pq = _dot(bn.astype(BF16), cs_ref[...])
    p_ref[0] = pq[:, :gd].astype(BF16)
    q_ref[0] = pq[:, gd:].astype(BF16)
    sgb_ref[0] = _silu(part(4)).astype(BF16)


def even_in_proj(h, pre_g, shift, scale, w_in, fourier_g, cs_small, tm):
    bsz, length, d = h.shape
    gd = fourier_g.shape[1]
    width = FOURIER_GROUPS * gd
    wspecs = [pl.BlockSpec((d, gd), lambda b, i, n, k=k: (0, k * FOURIER_GROUPS + n)) for k in range(EVEN_PARTS)]
    ospec = pl.BlockSpec((1, tm, gd), lambda b, i, n: (b, i, n))
    vec = pl.BlockSpec((1, 1, d), lambda b, i, n: (b, 0, 0))
    out = jax.ShapeDtypeStruct((bsz, length, width), BF16)
    return pl.pallas_call(
        _even_in_kernel,
        name="even_in_proj",
        grid=(bsz, length // tm, FOURIER_GROUPS),
        in_specs=[
            pl.BlockSpec((1, tm, d), lambda b, i, n: (b, i, 0)),
            pl.BlockSpec((1, d), lambda b, i, n: (0, 0)),
            vec, vec,
            *wspecs,
            pl.BlockSpec((1, 1, gd), lambda b, i, n: (n, 0, 0)),
            pl.BlockSpec((gd, 2 * gd), lambda b, i, n: (0, 0)),
        ],
        out_specs=[ospec] * 5,
        out_shape=[out] * 5,
        scratch_shapes=[pltpu.VMEM((tm, d), BF16)],
        compiler_params=_params(("parallel", "parallel", "arbitrary")),
    )(h, pre_g.reshape(1, d), shift, scale, *([w_in] * EVEN_PARTS),
      fourier_g.reshape(FOURIER_GROUPS, 1, gd), cs_small)


CONV_HALO = 16
CONV_ROWS = 32


def _conv_kernel(a_ref, prev_ref, next_ref, w_ref, cb_ref, lg_ref, lb_ref, sga_ref, o_ref,
                 ext_ref, sh_ref, acc_ref):
    i = pl.program_id(1)
    last = pl.num_programs(1) - 1
    tl, width = acc_ref.shape
    ext_ref[pl.ds(CONV_HALO, tl), :] = a_ref[0].astype(F32)
    ext_ref[pl.ds(0, CONV_HALO), :] = jnp.where(i > 0, prev_ref[0].astype(F32), 0.0)
    ext_ref[pl.ds(CONV_HALO + tl, CONV_HALO), :] = jnp.where(i < last, next_ref[0].astype(F32), 0.0)
    base = CONV_HALO - CONV_K // 2
    span = sh_ref.shape[1]
    for r in range(1, SUBLANES):
        sh_ref[r - 1] = ext_ref[pl.ds(r, span), :]

    for cb in range(width // LANES):
        cols = pl.ds(cb * LANES, LANES)
        taps = [jnp.broadcast_to(w_ref[pl.ds(k, 1), cols], (SUBLANES, LANES)) for k in range(CONV_K)]
        bias = jnp.broadcast_to(cb_ref[:, cols], (SUBLANES, LANES))

        def rows(rb, carry, cols=cols, taps=taps, bias=bias):
            r0 = pl.multiple_of(rb * CONV_ROWS, CONV_ROWS)
            accs = [bias] * (CONV_ROWS // SUBLANES)
            for k in range(CONV_K):
                q, r = divmod(base + k, SUBLANES)
                src = ext_ref if r == 0 else sh_ref.at[r - 1]
                for j in range(len(accs)):
                    accs[j] = accs[j] + src[pl.ds(r0 + (q + j) * SUBLANES, SUBLANES), cols] * taps[k]
            for j, acc in enumerate(accs):
                acc_ref[pl.ds(r0 + j * SUBLANES, SUBLANES), cols] = acc
            return carry

        lax.fori_loop(0, tl // CONV_ROWS, rows, 0, unroll=2)
    x = acc_ref[...]
    mu = jnp.mean(x, axis=-1, keepdims=True)
    xc = x - mu
    var = jnp.mean(xc * xc, axis=-1, keepdims=True)
    y = xc * lax.rsqrt(var + EPS) * lg_ref[...] + lb_ref[...]
    o_ref[0] = (_silu(y) * sga_ref[0].astype(F32)).astype(BF16)


def conv_branch(a, sga, conv_w, conv_b, ln_g, ln_b, tl):
    bsz, length, width = a.shape
    hb = tl // CONV_HALO
    nh = length // CONV_HALO
    row = pl.BlockSpec((1, width), lambda b, i: (0, 0))
    main = pl.BlockSpec((1, tl, width), lambda b, i: (b, i, 0))
    return pl.pallas_call(
        _conv_kernel,
        name="conv_branch",
        grid=(bsz, length // tl),
        in_specs=[
            main,
            pl.BlockSpec((1, CONV_HALO, width), lambda b, i: (b, jnp.maximum(i * hb - 1, 0), 0)),
            pl.BlockSpec((1, CONV_HALO, width), lambda b, i: (b, jnp.minimum((i + 1) * hb, nh - 1), 0)),
            pl.BlockSpec((CONV_K, width), lambda b, i: (0, 0)),
            row, row, row,
            main,
        ],
        out_specs=main,
        out_shape=jax.ShapeDtypeStruct((bsz, length, width), BF16),
        scratch_shapes=[pltpu.VMEM((tl + 2 * CONV_HALO, width), F32),
                        pltpu.VMEM((SUBLANES - 1, tl + 2 * CONV_HALO - SUBLANES, width), F32),
                        pltpu.VMEM((tl, width), F32)],
        compiler_params=_params(("parallel", "parallel")),
    )(a, a, a, conv_w, conv_b.reshape(1, width), ln_g.reshape(1, width), ln_b.reshape(1, width), sga)


DFT_SPLIT = 64


def _dft_gen_kernel(ca_ref, sa_ref, cb_ref, sb_ref, c_ref, s_ref):
    ca, sa = ca_ref[0], sa_ref[0]
    cb, sb = cb_ref[...], sb_ref[...]
    c_ref[...] = (ca * cb - sa * sb).astype(BF16)
    s_ref[...] = (-(sa * cb + ca * sb)).astype(BF16)


def dft_matrices(length):
    na = length // DFT_SPLIT
    k = np.arange(length)[None, :]
    w = 2.0 * math.pi / length
    ang_a = ((DFT_SPLIT * np.arange(na)[:, None] * k) % length) * w
    ang_b = ((np.arange(DFT_SPLIT)[:, None] * k) % length) * w
    ca, sa = (jnp.asarray(f(ang_a).reshape(na, 1, length), F32) for f in (np.cos, np.sin))
    cb, sb = (jnp.asarray(f(ang_b), F32) for f in (np.cos, np.sin))
    tab_a = pl.BlockSpec((1, 1, length), lambda a: (a, 0, 0))
    tab_b = pl.BlockSpec((DFT_SPLIT, length), lambda a: (0, 0))
    out = pl.BlockSpec((DFT_SPLIT, length), lambda a: (a, 0))
    shp = jax.ShapeDtypeStruct((length, length), BF16)
    return pl.pallas_call(
        _dft_gen_kernel,
        name="dft_matrices",
        grid=(na,),
        in_specs=[tab_a, tab_a, tab_b, tab_b],
        out_specs=[out, out],
        out_shape=[shp, shp],
        compiler_params=_params(("parallel",)),
    )(ca, sa, cb, sb)


def _dft_kernel(c_ref, s_ref, p_ref, q_ref, sgb_ref, o_ref, acc_ref, *, scale):
    kk = pl.program_id(2)

    @pl.when(kk == 0)
    def _():
        acc_ref[...] = jnp.zeros_like(acc_ref)

    acc_ref[...] += _dot(c_ref[...], p_ref[0]) + _dot(s_ref[...], q_ref[0])

    @pl.when(kk == pl.num_programs(2) - 1)
    def _():
        o_ref[0] = (acc_ref[...] * scale * sgb_ref[0].astype(F32)).astype(BF16)


def fourier_branch(cmat, smat, p, q, sgb, scale, tm, tk):
    bsz, length, width = p.shape
    return pl.pallas_call(
        functools.partial(_dft_kernel, scale=scale),
        name="fourier_branch",
        grid=(bsz, length // tm, length // tk),
        in_specs=[
            pl.BlockSpec((tm, tk), lambda b, m, k: (m, k)),
            pl.BlockSpec((tm, tk), lambda b, m, k: (m, k)),
            pl.BlockSpec((1, tk, width), lambda b, m, k: (b, k, 0)),
            pl.BlockSpec((1, tk, width), lambda b, m, k: (b, k, 0)),
            pl.BlockSpec((1, tm, width), lambda b, m, k: (b, m, 0)),
        ],
        out_specs=pl.BlockSpec((1, tm, width), lambda b, m, k: (b, m, 0)),
        out_shape=jax.ShapeDtypeStruct((bsz, length, width), BF16),
        scratch_shapes=[pltpu.VMEM((tm, width), F32)],
        compiler_params=_params(("parallel", "parallel", "arbitrary")),
    )(cmat, smat, p, q, sgb)


FFT_MINOR = 64
FFT_COLS = 256


def _fft_kernel(p_ref, q_ref, k1a_ref, k1b_ref, twr_ref, twi_ref, hre_ref, him_ref, sgb_ref, o_ref,
                x_ref, y_ref, z_ref, *, scale):
    length, cw = p_ref.shape[1], p_ref.shape[2]
    ncb = cw // LANES
    n_m = length // FFT_MINOR
    n_a1 = FFT_MINOR // SUBLANES
    n_kb = n_m // SUBLANES
    slab = n_m * SUBLANES
    piece = SUBLANES * SUBLANES
    for cb in range(ncb):
        cols = pl.ds(cb * LANES, LANES)
        x_ref[cb] = p_ref[0, :, cols].astype(F32).reshape(n_m, FFT_MINOR, LANES)
        x_ref[ncb + cb] = q_ref[0, :, cols].astype(F32).reshape(n_m, FFT_MINOR, LANES)

    def slab_of(part, a1):
        tiles = [x_ref[part * ncb + cb, :, pl.ds(a1 * SUBLANES, SUBLANES), :].reshape(slab, LANES)
                 for cb in range(ncb)]
        return jnp.concatenate(tiles, axis=1).astype(BF16)

    for a1 in range(n_a1):
        y = _dot(k1a_ref[...], slab_of(0, a1)) + _dot(k1b_ref[...], slab_of(1, a1))
        tr, ti = twr_ref[a1], twi_ref[a1]
        for cb in range(ncb):
            yre = y[:slab, cb * LANES:(cb + 1) * LANES]
            yim = y[slab:, cb * LANES:(cb + 1) * LANES]
            zre = (tr * yre - ti * yim).astype(BF16)
            zim = (tr * yim + ti * yre).astype(BF16)
            for kb in range(n_kb):
                dst = (pl.ds(a1 * piece, piece), pl.ds(cb * LANES, LANES))
                y_ref[(kb, 0) + dst] = zre[kb * piece:(kb + 1) * piece]
                y_ref[(kb, 1) + dst] = zim[kb * piece:(kb + 1) * piece]

    for kb in range(n_kb):
        out = _dot(hre_ref[...], y_ref[kb, 0]) + _dot(him_ref[...], y_ref[kb, 1])
        z_ref[:, kb] = out.reshape(FFT_MINOR, SUBLANES, cw)

    o_ref[0] = (z_ref[...].reshape(length, cw) * scale * sgb_ref[0].astype(F32)).astype(BF16)


def _fft_tables(length):
    n_a, s = FFT_MINOR, SUBLANES
    n_m = length // n_a
    n_a1 = n_a // s
    eye = np.eye(s)
    ang_m = (np.outer(np.arange(n_m), np.arange(n_m)) % n_m) * (2.0 * math.pi / n_m)
    cm, sm = np.kron(np.cos(ang_m), eye), np.kron(np.sin(ang_m), eye)
    k1a = np.concatenate([cm, -sm], axis=0)
    k1b = np.concatenate([-sm, -cm], axis=0)
    a_full = s * np.arange(n_a1)[:, None, None] + np.arange(s)[None, None, :]
    ang_t = ((np.arange(n_m)[None, :, None] * a_full) % length) * (2.0 * math.pi / length)
    rep = lambda t: np.broadcast_to(t.reshape(n_a1, n_m * s, 1), (n_a1, n_m * s, LANES))
    twr, twi = rep(np.cos(ang_t)), rep(-np.sin(ang_t))
    ang_a = ((np.arange(n_a)[:, None, None] * a_full.reshape(1, n_a1, s)) % n_a) * (2.0 * math.pi / n_a)
    spread = lambda t: np.einsum("kas,pq->kpaqs", t, eye).reshape(n_a * s, n_a1 * s * s)
    hre, him = spread(np.cos(ang_a)), spread(np.sin(ang_a))
    return ([jnp.asarray(t, BF16) for t in (k1a, k1b)] + [jnp.asarray(t, F32) for t in (twr, twi)]
            + [jnp.asarray(t, BF16) for t in (hre, him)])


def fourier_branch_fft(p, q, sgb, scale):
    bsz, length, width = p.shape
    tables = _fft_tables(length)
    n_m = length // FFT_MINOR
    tok = pl.BlockSpec((1, length, FFT_COLS), lambda b, n: (b, 0, n))
    const = lambda t: pl.BlockSpec(t.shape, lambda b, n, nd=t.ndim: (0,) * nd)
    return pl.pallas_call(
        functools.partial(_fft_kernel, scale=scale),
        name="fourier_branch_fft",
        grid=(bsz, width // FFT_COLS),
        in_specs=[tok, tok] + [const(t) for t in tables] + [tok],
        out_specs=tok,
        out_shape=jax.ShapeDtypeStruct((bsz, length, width), BF16),
        scratch_shapes=[pltpu.VMEM((2 * FFT_COLS // LANES, n_m, FFT_MINOR, LANES), F32),
                        pltpu.VMEM((n_m // SUBLANES, 2, FFT_MINOR * SUBLANES, FFT_COLS), BF16),
                        pltpu.VMEM((FFT_MINOR, n_m // SUBLANES, SUBLANES, FFT_COLS), F32)],
        compiler_params=_params(("parallel", "parallel")),
    )(p, q, *tables, sgb)


def _postnorm_residual(y, h_ref, pg_ref, gt_ref, o_ref):
    o_ref[0] = h_ref[0] + gt_ref[0] * (_rms(y) * pg_ref[...])


def _even_out_kernel(a_ref, b_ref, wa_ref, wb_ref, h_ref, pg_ref, gt_ref, o_ref):
    y = _dot(a_ref[0], wa_ref[...]) + _dot(b_ref[0], wb_ref[...])
    _postnorm_residual(y, h_ref, pg_ref, gt_ref, o_ref)


def even_out_proj(a, b, wa, wb, h, post_g, gate, tm):
    bsz, length, d = h.shape
    width = a.shape[2]
    half = pl.BlockSpec((1, tm, width), lambda bb, i: (bb, i, 0))
    wspec = pl.BlockSpec((width, d), lambda bb, i: (0, 0))
    full = pl.BlockSpec((1, tm, d), lambda bb, i: (bb, i, 0))
    return pl.pallas_call(
        _even_out_kernel,
        name="even_out_proj",
        grid=(bsz, length // tm),
        in_specs=[half, half, wspec, wspec, full,
                  pl.BlockSpec((1, d), lambda bb, i: (0, 0)),
                  pl.BlockSpec((1, 1, d), lambda bb, i: (bb, 0, 0))],
        out_specs=full,
        out_shape=jax.ShapeDtypeStruct((bsz, length, d), F32),
        compiler_params=_params(("parallel", "parallel")),
    )(a, b, wa, wb, h, post_g.reshape(1, d), gate)


ODD_TN = 1024


ODD_PARTS = 2


def _odd_in_kernel(h_ref, g_ref, sh_ref, sc_ref, w_ref, o_ref, xn_ref, *, gate_parts):
    n = pl.program_id(2)

    @pl.when(n == 0)
    def _():
        _prenorm_modulate(h_ref, g_ref, sh_ref, sc_ref, xn_ref)

    xn = xn_ref[...]
    pw = ODD_TN // ODD_PARTS
    for j in range(ODD_PARTS):
        cols = pl.ds(j * pw, pw)
        acc = _dot(xn, w_ref[:, cols])
        tiles = [g // ODD_PARTS for g in gate_parts if g % ODD_PARTS == j]
        if tiles:
            is_gate = functools.reduce(jnp.logical_or, [n == t for t in tiles])
            acc = jnp.where(is_gate, _silu(acc), acc)
        o_ref[0, :, cols] = acc.astype(BF16)


def odd_in_proj(h, pre_g, shift, scale, w_all, gate_ranges, tm):
    bsz, length, d = h.shape
    n_all = w_all.shape[1]
    pw = ODD_TN // ODD_PARTS
    assert n_all % ODD_TN == 0 and all(s % pw == 0 and e % pw == 0 for s, e in gate_ranges)
    gate_parts = tuple(p for s, e in gate_ranges for p in range(s // pw, e // pw))
    vec = pl.BlockSpec((1, 1, d), lambda b, i, n: (b, 0, 0))
    return pl.pallas_call(
        functools.partial(_odd_in_kernel, gate_parts=gate_parts),
        name="odd_in_proj",
        grid=(bsz, length // tm, n_all // ODD_TN),
        in_specs=[pl.BlockSpec((1, tm, d), lambda b, i, n: (b, i, 0)),
                  pl.BlockSpec((1, d), lambda b, i, n: (0, 0)),
                  vec, vec,
                  pl.BlockSpec((d, ODD_TN), lambda b, i, n: (0, n))],
        out_specs=pl.BlockSpec((1, tm, ODD_TN), lambda b, i, n: (b, i, n)),
        out_shape=jax.ShapeDtypeStruct((bsz, length, n_all), BF16),
        scratch_shapes=[pltpu.VMEM((tm, d), BF16)],
        compiler_params=_params(("parallel", "parallel", "arbitrary")),
    )(h, pre_g.reshape(1, d), shift, scale, w_all)


def _na_pair_geometry(variant, rows):
    r0 = {0: 4, 1: 0, 2: 2, 3: rows - 4, 4: rows - 2}[variant]
    ws = min(max(r0 - NA_ROWS // 2, 0), rows - NA_WIN_ROWS)
    return r0, ws


def _na_bias_kernel(rpb_ref, o_ref, *, rows):
    h = pl.program_id(0)
    n_dr, n_dc = 2 * NA_ROWS - 1, 2 * NA_COLS - 1
    qc = lax.broadcasted_iota(jnp.int32, (GRID_W, LANES), 0)
    lane = lax.broadcasted_iota(jnp.int32, (GRID_W, LANES), 1)
    kc = lane % GRID_W
    hi = lane // GRID_W
    diff = kc - qc
    c_start = jnp.clip(qc - NA_COLS // 2, 0, GRID_W - NA_COLS)
    col_ok = jnp.where(kc >= c_start, 1, 0) * jnp.where(kc < c_start + NA_COLS, 1, 0)
    blocks = []
    for dr in range(n_dr):
        val = jnp.full((GRID_W, LANES), NEG_INF, F32)
        for dc in range(n_dc):
            val = jnp.where(diff == dc - (NA_COLS - 1), rpb_ref[(h * n_dr + dr) * n_dc + dc], val)
        blocks.append(jnp.where(col_ok == 1, val, NEG_INF))
    masked = jnp.full((GRID_W, LANES), NEG_INF, F32)
    for variant in range(NA_VARIANTS):
        r0, ws = _na_pair_geometry(variant, rows)
        for ri in range(2):
            r = r0 + ri
            r_start = min(max(r - NA_ROWS // 2, 0), rows - NA_ROWS)
            for wp in range(NA_WIN_ROWS // 2):
                halves = []
                for a in (ws + 2 * wp, ws + 2 * wp + 1):
                    in_window = r_start <= a < r_start + NA_ROWS
                    halves.append(blocks[a - r + NA_ROWS - 1] if in_window else masked)
                o_ref[0, variant, pl.ds(ri * GRID_W, GRID_W), pl.ds(wp * LANES, LANES)] = jnp.where(
                    hi == 1, halves[1], halves[0])


def na_bias_table(rpb, rows):
    heads = rpb.shape[0]
    return pl.pallas_call(
        functools.partial(_na_bias_kernel, rows=rows),
        name="na_bias_table",
        grid=(heads,),
        in_specs=[pl.BlockSpec(memory_space=pltpu.SMEM)],
        out_specs=pl.BlockSpec((1, NA_VARIANTS, NA_PAIR, NA_WIN), lambda h: (h, 0, 0, 0)),
        out_shape=jax.ShapeDtypeStruct((heads, NA_VARIANTS, NA_PAIR, NA_WIN), F32),
        compiler_params=_params(("parallel",)),
    )(rpb.reshape(-1))


def _na_kernel(q_ref, k_ref, v_ref, kc_ref, vc_ref, bias_ref, sg_ref, o_ref,
               vx_ref, vcx_ref, s0_ref, s1_ref, p0_ref, p1_ref, *, rows):
    npairs = rows // 2
    last = npairs - 1
    n_heads = q_ref.shape[2] // HEAD_DIM
    nt = (((1,), (1,)), ((), ()))
    head = lambda hh: pl.ds(hh * HEAD_DIM, HEAD_DIM)
    for hh in range(n_heads):
        vx_ref[hh, :, :HEAD_DIM] = v_ref[0, :, head(hh)]
        vx_ref[hh, :, HEAD_DIM:] = jnp.ones((vx_ref.shape[1], HEAD_DIM), BF16)
        vcx_ref[hh, :, :HEAD_DIM] = vc_ref[0, :, head(hh)]
        vcx_ref[hh, :, HEAD_DIM:] = jnp.ones((vcx_ref.shape[1], HEAD_DIM), BF16)

    def window(pr):
        ws = min(max(2 * pr - NA_ROWS // 2, 0), rows - NA_WIN_ROWS)
        return pl.ds(ws * GRID_W, NA_WIN)

    def qrows(pr):
        return pl.ds(pr * NA_PAIR, NA_PAIR)

    def scores(unit, s_ref):
        hh, pr = divmod(unit, npairs)
        variant = {0: 1, 1: 2, last - 1: 3, last: 4}.get(pr, 0)
        q = q_ref[0, qrows(pr), head(hh)]
        s_ref[:, :NA_WIN] = lax.dot_general(q, k_ref[0, window(pr), head(hh)], nt,
                                            preferred_element_type=F32) + bias_ref[hh, variant]
        s_ref[:, NA_WIN:] = lax.dot_general(q, kc_ref[0, :, head(hh)], nt, preferred_element_type=F32)

    def probs(s_ref, p_ref):
        s = s_ref[...]
        p_ref[...] = jnp.exp(s - jnp.max(s, axis=-1, keepdims=True)).astype(BF16)

    def values(unit, p_ref):
        hh, pr = divmod(unit, npairs)
        acc = _dot(p_ref[:, :NA_WIN], vx_ref[hh, window(pr), :]) + _dot(p_ref[:, NA_WIN:], vcx_ref[hh])
        o = acc[:, :HEAD_DIM] / acc[:, HEAD_DIM:]
        o_ref[0, qrows(pr), head(hh)] = (o * sg_ref[0, qrows(pr), head(hh)].astype(F32)).astype(BF16)

    units = n_heads * npairs
    bufs = ((s0_ref, p0_ref), (s1_ref, p1_ref))
    scores(0, s0_ref)
    scores(1, s1_ref)
    probs(s0_ref, p0_ref)
    for i in range(units):
        s_cur, p_cur = bufs[i % 2]
        s_nxt, p_nxt = bufs[(i + 1) % 2]
        values(i, p_cur)
        if i + 1 < units:
            probs(s_nxt, p_nxt)
        if i + 2 < units:
            scores(i + 2, s_cur)


def neighbourhood_attention(proj, proj_c, bias, na_w, gate_col):
    bsz, length, _ = proj.shape
    lc = proj_c.shape[1]
    heads = na_w // HEAD_DIM
    rows = length // GRID_W
    nh = NA_HEADS_PER_STEP
    assert heads % nh == 0 and gate_col % (nh * HEAD_DIM) == 0
    groups = heads // nh
    seq = lambda off: pl.BlockSpec((1, length, nh * HEAD_DIM), lambda h, b: (b, 0, off + h))
    cseq = lambda off: pl.BlockSpec((1, lc, nh * HEAD_DIM), lambda h, b: (b, 0, off + h))
    return pl.pallas_call(
        functools.partial(_na_kernel, rows=rows),
        name="neighbourhood_attention",
        grid=(groups, bsz),
        in_specs=[seq(0), seq(groups), seq(2 * groups), cseq(groups), cseq(2 * groups),
                  pl.BlockSpec((nh, NA_VARIANTS, NA_PAIR, NA_WIN), lambda h, b: (h, 0, 0, 0)),
                  seq(gate_col // (nh * HEAD_DIM))],
        out_specs=seq(0),
        out_shape=jax.ShapeDtypeStruct((bsz, length, na_w), BF16),
        scratch_shapes=[pltpu.VMEM((nh, length, 2 * HEAD_DIM), BF16), pltpu.VMEM((nh, lc, 2 * HEAD_DIM), BF16),
                        pltpu.VMEM((NA_PAIR, NA_WIN + lc), F32), pltpu.VMEM((NA_PAIR, NA_WIN + lc), F32),
                        pltpu.VMEM((NA_PAIR, NA_WIN + lc), BF16), pltpu.VMEM((NA_PAIR, NA_WIN + lc), BF16)],
        compiler_params=_params(("parallel", "parallel")),
    )(proj, proj, proj, proj_c, proj_c, bias, proj)


S5_CHUNK = 128
S5_IN_BLOCK = LANES
S5_OUT_BLOCK = 256
S5_SCAN_COLS = 1024


def _s5_kernel(ufc_ref, ufx_ref, ubc_ref, ubx_ref, bf_ref, bb_ref, crf_ref, cif_ref, crb_ref, cib_ref,
               lre_ref, lim_ref, yf_ref, yb_ref, buf_ref, state_ref, stage_ref, *, ctx_chunks):
    bsz, t_len, width = ufx_ref.shape
    n_state = lre_ref.shape[1]
    in_ctx = pl.program_id(0) < ctx_chunks
    n_in = width // S5_IN_BLOCK
    blk = n_state // n_in
    rows8 = 2 * bsz

    @pl.when(pl.program_id(0) == 0)
    def _():
        state_ref[...] = jnp.zeros_like(state_ref)

    nlb = n_state // LANES
    per_in = blk // LANES
    ti = lax.broadcasted_iota(jnp.int32, (t_len, t_len), 0)
    tj = lax.broadcasted_iota(jnp.int32, (t_len, t_len), 1)
    rev = jnp.where(ti + tj == t_len - 1, 1.0, 0.0).astype(BF16)
    u_fwd = jnp.where(in_ctx, ufc_ref[...], ufx_ref[...]).reshape(bsz * t_len, width)
    u_nat = jnp.where(in_ctx, ubc_ref[...], ubx_ref[...])
    u_bwd = jnp.concatenate([_dot(rev, u_nat[b]).astype(BF16) for b in range(bsz)], axis=0)
    sides = ((u_fwd, bf_ref, yf_ref, crf_ref, cif_ref), (u_bwd, bb_ref, yb_ref, crb_ref, cib_ref))

    def project(kb):
        for di, (u, w_ref, _, _, _) in enumerate(sides):
            res = _dot(u[:, kb * S5_IN_BLOCK:(kb + 1) * S5_IN_BLOCK], w_ref[kb])
            for b in range(bsz):
                rsel = pl.ds(di * bsz + b, t_len, stride=rows8)
                for c in range(2 * per_in):
                    dst = (c // per_in) * nlb + kb * per_in + c % per_in
                    buf_ref[dst, rsel, :] = res[b * t_len:(b + 1) * t_len, c * LANES:(c + 1) * LANES]

    per = S5_SCAN_COLS // LANES

    def scan(group, t0, t1, carry):
        blocks = [group * per + i for i in range(per)]
        lre = [lre_ref[:, pl.ds(c * LANES, LANES)] for c in blocks]
        lim = [lim_ref[:, pl.ds(c * LANES, LANES)] for c in blocks]
        if carry is None:
            carry = [state_ref[:, pl.ds(part * n_state + c * LANES, LANES)] for c in blocks for part in range(2)]
        for t in range(t0, t1):
            row = pl.ds(t * rows8, rows8)
            for i, c in enumerate(blocks):
                hre, him = carry[2 * i], carry[2 * i + 1]
                carry[2 * i] = lre[i] * hre - lim[i] * him + buf_ref[c, row, :]
                carry[2 * i + 1] = lre[i] * him + lim[i] * hre + buf_ref[nlb + c, row, :]
                buf_ref[c, row, :] = carry[2 * i]
                buf_ref[nlb + c, row, :] = carry[2 * i + 1]
        if t1 == t_len:
            for i, c in enumerate(blocks):
                state_ref[:, pl.ds(c * LANES, LANES)] = carry[2 * i]
                state_ref[:, pl.ds(n_state + c * LANES, LANES)] = carry[2 * i + 1]
        return carry

    n_out = width // S5_OUT_BLOCK
    per_out = nlb // n_out

    def readout(di, j):
        _, _, y_ref, cr_ref, ci_ref = sides[di]
        stage = stage_ref.at[di * n_out + j]
        for part in range(2):
            for b in range(bsz):
                rsel = pl.ds(di * bsz + b, t_len, stride=rows8)
                for c in range(per_out):
                    stage[part, pl.ds(b * t_len, t_len), pl.ds(c * LANES, LANES)] = buf_ref[
                        part * nlb + j * per_out + c, rsel, :].astype(BF16)
        y = _dot(stage[0], cr_ref[j]) + _dot(stage[1], ci_ref[j])
        for b in range(bsz):
            yb = y[b * t_len:(b + 1) * t_len]
            if di == 1:
                hi = yb.astype(BF16)
                r1 = yb - hi.astype(F32)
                mid = r1.astype(BF16)
                lo = (r1 - mid.astype(F32)).astype(BF16)
                yb = _dot(rev, hi) + _dot(rev, mid) + _dot(rev, lo)
            y_ref[b, :, pl.ds(j * S5_OUT_BLOCK, S5_OUT_BLOCK)] = yb

    assert n_in == 4 and nlb // per == 2 and n_out == 2 and per_out == per
    half = t_len // 2
    project(0)
    project(1)
    carry = scan(0, 0, half, None)
    project(2)
    scan(0, half, t_len, carry)
    project(3)
    carry = scan(1, 0, half, None)
    readout(0, 0)
    scan(1, half, t_len, carry)
    readout(1, 0)
    readout(0, 1)
    readout(1, 1)


def _block_diag(x):
    nblk, g, r, c = x.shape
    eye = jnp.eye(g, dtype=x.dtype)
    return jnp.einsum("kgrc,gh->kgrhc", x, eye).reshape(nblk, g * r, g * c)


def _s5_direction_params(a_re, a_im, log_dt, b_re, b_im, c_re, c_im):
    groups, n_p = a_re.shape
    lam = lax.complex(a_re.astype(F32), a_im.astype(F32))
    dt = jnp.exp(log_dt.astype(F32))[:, None]
    lam_bar = jnp.exp(lam * dt)
    b_bar = ((lam_bar - 1.0) / lam)[..., None] * lax.complex(b_re.astype(F32), b_im.astype(F32))
    gi = S5_IN_BLOCK // SSM_GROUP
    bt = jnp.swapaxes(b_bar, 1, 2).reshape(groups // gi, gi, SSM_GROUP, n_p)
    b_mat = jnp.concatenate([_block_diag(jnp.real(bt)), _block_diag(jnp.imag(bt))], axis=-1)
    go = S5_OUT_BLOCK // SSM_GROUP
    ct = lambda c: jnp.swapaxes(c.astype(F32), 1, 2).reshape(groups // go, go, n_p, SSM_GROUP)
    return (b_mat.astype(BF16), _block_diag(ct(c_re)).astype(BF16), _block_diag(-ct(c_im)).astype(BF16),
            jnp.real(lam_bar).reshape(-1), jnp.imag(lam_bar).reshape(-1))


def s5_scan(proj_c, proj_x, d_col, width, fwd, bwd):
    bsz, lc, _ = proj_c.shape
    length = proj_x.shape[1]
    t_len = S5_CHUNK
    ncc, ncx = lc // t_len, length // t_len
    nc = ncc + ncx
    col = d_col // width
    n_state = fwd[3].shape[0]
    lre = jnp.concatenate([jnp.broadcast_to(fwd[3], (bsz, n_state)), jnp.broadcast_to(bwd[3], (bsz, n_state))])
    lim = jnp.concatenate([jnp.broadcast_to(fwd[4], (bsz, n_state)), jnp.broadcast_to(bwd[4], (bsz, n_state))])
    full = lambda x: pl.BlockSpec(x.shape, lambda c, nd=x.ndim: (0,) * nd)
    blk = lambda index: pl.BlockSpec((bsz, t_len, width), index)
    in_blocks = [blk(lambda c: (0, jnp.minimum(c, ncc - 1), col)),
                 blk(lambda c: (0, jnp.maximum(c - ncc, 0), col)),
                 blk(lambda c: (0, jnp.maximum(ncc - 1 - c, 0), col)),
                 blk(lambda c: (0, jnp.minimum(nc - 1 - c, ncx - 1), col))]
    out = jax.ShapeDtypeStruct((bsz, length, width), F32)
    consts = (fwd[0], bwd[0], fwd[1], fwd[2], bwd[1], bwd[2], lre, lim)
    return pl.pallas_call(
        functools.partial(_s5_kernel, ctx_chunks=ncc),
        name="s5_scan",
        grid=(nc,),
        in_specs=in_blocks + [full(x) for x in consts],
        out_specs=[blk(lambda c: (0, jnp.maximum(c - ncc, 0), 0)),
                   blk(lambda c: (0, jnp.minimum(nc - 1 - c, ncx - 1), 0))],
        out_shape=[out, out],
        scratch_shapes=[pltpu.VMEM((2 * n_state // LANES, t_len * 2 * bsz, LANES), F32),
                        pltpu.VMEM((2 * bsz, 2 * n_state), F32),
                        pltpu.VMEM((2 * (width // S5_OUT_BLOCK), 2, bsz * t_len,
                                    n_state // (width // S5_OUT_BLOCK)), BF16)],
        compiler_params=_params(("arbitrary",)),
    )(proj_c, proj_x, proj_c, proj_x, *consts)


def _odd_out_kernel(na_ref, yf_ref, yb_ref, d_ref, sdg_ref, dsk_ref, wglu_ref, wna_ref, wssm_ref,
                    h_ref, pg_ref, gt_ref, o_ref):
    y = yf_ref[0] + yb_ref[0] + dsk_ref[...] * d_ref[0].astype(F32)
    y = 0.5 * y * (1.0 + jnp.tanh(math.sqrt(2.0 / math.pi) * (y + 0.044715 * (y * y * y))))
    z = y * jax.nn.sigmoid(_dot(y.astype(BF16), wglu_ref[...]))
    s = (z * sdg_ref[0].astype(F32)).astype(BF16)
    out = _dot(na_ref[0], wna_ref[...]) + _dot(s, wssm_ref[...])
    _postnorm_residual(out, h_ref, pg_ref, gt_ref, o_ref)


def odd_out_proj(na, yf, yb, proj, d_col, dg_col, d_skip, w_glu, w_na, w_ssm, h, post_g, gate, tm):
    bsz, length, dm = h.shape
    na_w, ssm_w = na.shape[2], yf.shape[2]
    tok = lambda w, col=0: pl.BlockSpec((1, tm, w), lambda b, i: (b, i, col // w))
    const = lambda r, c: pl.BlockSpec((r, c), lambda b, i: (0, 0))
    return pl.pallas_call(
        _odd_out_kernel,
        name="odd_out_proj",
        grid=(bsz, length // tm),
        in_specs=[tok(na_w), tok(ssm_w), tok(ssm_w),
                  tok(ssm_w, d_col), tok(ssm_w, dg_col),
                  const(1, ssm_w), const(ssm_w, ssm_w), const(na_w, dm), const(ssm_w, dm),
                  tok(dm), const(1, dm),
                  pl.BlockSpec((1, 1, dm), lambda b, i: (b, 0, 0))],
        out_specs=tok(dm),
        out_shape=jax.ShapeDtypeStruct((bsz, length, dm), F32),
        compiler_params=_params(("parallel", "parallel")),
    )(na, yf, yb, proj, proj, d_skip.reshape(1, ssm_w), w_glu, w_na, w_ssm, h, post_g.reshape(1, dm), gate)


def _small_dft(n):
    ang = (np.outer(np.arange(n), np.arange(n)) % n) * (2.0 * math.pi / n)
    return jnp.asarray(np.concatenate([np.cos(ang), np.sin(ang)], axis=1), BF16)


def _conv_fourier_layer(h, pre_g, post_g, shift, scale, gate, w5, w_out_a, w_out_b, conv_w, conv_b,
                        ln_g, ln_b, fourier_g, cs_small, shared_mod):
    bsz, length, _ = h.shape
    gd = fourier_g.shape[1]
    fold = (lambda t: t.reshape(1, bsz * length, t.shape[2])) if shared_mod else (lambda t: t)
    unfold = (lambda t: t.reshape(bsz, length, t.shape[2])) if shared_mod else (lambda t: t)
    vec = (lambda v: v[:1]) if shared_mod else (lambda v: v)
    tokens = bsz * length if shared_mod else length
    outs = even_in_proj(fold(h), pre_g, vec(shift), vec(scale), w5, fourier_g, cs_small,
                        _token_tile(tokens, IN_PROJ_ROWS))
    a, sga, p, q, sgb = [unfold(t) for t in outs]
    a = conv_branch(a, sga, conv_w, conv_b, ln_g, ln_b, _token_tile(length, CONV_TILE_ROWS))
    ortho = 1.0 / math.sqrt(length * gd)
    if (length // FFT_MINOR) % SUBLANES == 0:
        f = fourier_branch_fft(p, q, sgb, ortho)
    else:
        cmat, smat = dft_matrices(length)
        f = fourier_branch(cmat, smat, p, q, sgb, ortho, length, length)
    return unfold(even_out_proj(fold(a), fold(f), w_out_a, w_out_b, fold(h), post_g, vec(gate),
                                _token_tile(tokens, OUT_PROJ_ROWS)))


def kernel(x, c, ctx, c_ctx, pre_g, post_g, ada_w, ada_b, ab_w_in, ab_w_out, conv_w, conv_b, conv_ln_g,
           conv_ln_b, fourier_g, cd_w_in, cd_w_out, na_rpb, s5_a_re, s5_a_im, s5_log_dt, s5_b_re, s5_b_im,
           s5_c_re, s5_c_im, s5_d, s5_w_glu):
    bsz, length, d = x.shape
    lc = ctx.shape[1]
    depth = ada_w.shape[0]
    assert depth == 2 and length % (2 * GRID_W) == 0 and length // GRID_W >= NA_WIN_ROWS + 2
    assert lc % S5_CHUNK == 0 and length % S5_CHUNK == 0

    cond_rows = jnp.zeros((SUBLANES, d), F32).at[:bsz].set(c).at[bsz].set(c_ctx)
    mods = ada_modulation(cond_rows, ada_w, ada_b)

    def mod_vectors(i):
        xs = [mods[i, :bsz, k * d:(k + 1) * d][:, None, :] for k in range(3)]
        cs = [jnp.broadcast_to(mods[i, bsz, k * d:(k + 1) * d], (bsz, 1, d)) for k in range(3)]
        return xs, cs

    (sh_x, sc_x, gt_x), (sh_c, sc_c, gt_c) = mod_vectors(0)
    w_in = ab_w_in[0]
    conv_width = conv_w.shape[2]
    cw = conv_width
    fw = (w_in.shape[1] - 3 * cw) // 2
    assert cw == fw
    w5 = w_in.astype(BF16)
    w_out_a = ab_w_out[0, :cw].astype(BF16)
    w_out_b = ab_w_out[0, cw:].astype(BF16)
    cs_small = _small_dft(fourier_g.shape[2])
    layer0 = functools.partial(
        _conv_fourier_layer, pre_g=pre_g[0], post_g=post_g[0], w5=w5, w_out_a=w_out_a, w_out_b=w_out_b,
        conv_w=conv_w[0], conv_b=conv_b[0], ln_g=conv_ln_g[0], ln_b=conv_ln_b[0], fourier_g=fourier_g[0],
        cs_small=cs_small)
    h_x = layer0(x, shift=sh_x, scale=sc_x, gate=gt_x, shared_mod=False)
    h_c = layer0(ctx, shift=sh_c, scale=sc_c, gate=gt_c, shared_mod=True)

    (sh_x, sc_x, gt_x), (sh_c, sc_c, _) = mod_vectors(1)
    w_in = cd_w_in[0]
    ssm_w = s5_d.shape[1]
    na_w = (w_in.shape[1] - 2 * ssm_w) // 4
    g_col, d_col, dg_col = 3 * na_w, 4 * na_w, 4 * na_w + ssm_w
    gates = ((g_col, d_col), (dg_col, dg_col + ssm_w))
    col_scale = jnp.where(jnp.arange(w_in.shape[1]) < na_w, HEAD_DIM ** -0.5, 1.0).astype(F32)
    w_all = (w_in * col_scale).astype(BF16)
    proj_x = odd_in_proj(h_x, pre_g[1], sh_x, sc_x, w_all, gates, _token_tile(length, IN_PROJ_ROWS))
    proj_c = odd_in_proj(h_c.reshape(1, bsz * lc, d), pre_g[1], sh_c[:1], sc_c[:1], w_all, gates,
                         _token_tile(bsz * lc, IN_PROJ_ROWS)).reshape(bsz, lc, -1)

    bias = na_bias_table(na_rpb[0], length // GRID_W)
    na = neighbourhood_attention(proj_x, proj_c, bias, na_w, g_col)

    dirs = [_s5_direction_params(s5_a_re[0, i], s5_a_im[0, i], s5_log_dt[0, i], s5_b_re[0, i], s5_b_im[0, i],
                                 s5_c_re[0, i], s5_c_im[0, i]) for i in range(2)]
    y_f, y_b = s5_scan(proj_c, proj_x, d_col, ssm_w, *dirs)

    return odd_out_proj(na, y_f, y_b, proj_x, d_col, dg_col, s5_d[0], s5_w_glu[0].astype(BF16),
                        cd_w_out[0, :na_w].astype(BF16), cd_w_out[0, na_w:].astype(BF16),
                        h_x, post_g[1], gt_x, _token_tile(length, OUT_PROJ_ROWS))
```

```python
import functools
import math

import jax
import jax.numpy as jnp
import numpy as np
from jax import lax
from jax.experimental import pallas as pl
from jax.experimental.pallas import tpu as pltpu

F32 = jnp.float32
BF16 = jnp.bfloat16

EPS = 1e-6
NEG_INF = -1e30

GRID_W = 64
CONV_K = 31
FOURIER_GROUPS = 4
HEAD_DIM = 128
NA_ROWS = 8
NA_COLS = 16
SSM_GROUP = 16
SSM_STATE = 64

LANES = 128
SUBLANES = 8
VMEM_LIMIT = 56 * 1024 * 1024

IN_PROJ_ROWS = 1024
OUT_PROJ_ROWS = 512
CONV_TILE_ROWS = 256
ADA_COLS = 1024


def _token_tile(n_tokens, cap):
    tile = min(cap, n_tokens)
    assert n_tokens % tile == 0
    return tile

NA_QROWS = 4
NA_PAIR = NA_QROWS * GRID_W
NA_WIN_ROWS = NA_ROWS + NA_QROWS
NA_WIN = NA_WIN_ROWS * GRID_W
NA_VARIANTS = 3
NA_HEADS_PER_STEP = 2


def _params(sem, vmem=VMEM_LIMIT):
    return pltpu.CompilerParams(dimension_semantics=sem, vmem_limit_bytes=vmem)


def _silu(x):
    return x * jax.nn.sigmoid(x)


def _rms(x):
    return x * lax.rsqrt(jnp.mean(x * x, axis=-1, keepdims=True) + EPS)


def _dot(a, b):
    return jnp.dot(a, b, preferred_element_type=F32)


def _ada_kernel(c_ref, w_ref, b_ref, o_ref):
    cond = _silu(c_ref[...])
    o_ref[0] = _dot(cond.astype(BF16), w_ref[0].astype(BF16)) + b_ref[0]


def ada_modulation(cond_rows, ada_w, ada_b):
    depth, d, n = ada_w.shape
    tn = _token_tile(n, ADA_COLS)
    rows = cond_rows.shape[0]
    return pl.pallas_call(
        _ada_kernel,
        name="ada_modulation",
        grid=(depth, n // tn),
        in_specs=[
            pl.BlockSpec((rows, d), lambda i, j: (0, 0)),
            pl.BlockSpec((1, d, tn), lambda i, j: (i, 0, j)),
            pl.BlockSpec((1, 1, tn), lambda i, j: (i, 0, j)),
        ],
        out_specs=pl.BlockSpec((1, rows, tn), lambda i, j: (i, 0, j)),
        out_shape=jax.ShapeDtypeStruct((depth, rows, n), F32),
        compiler_params=_params(("parallel", "parallel")),
    )(cond_rows, ada_w, ada_b.reshape(depth, 1, n))


PRENORM_ROWS = 16


def _prenorm_modulate(h_ref, g_ref, sh_ref, sc_ref, xn_ref):
    gain = g_ref[...] * (1.0 + sc_ref[0])
    shift = sh_ref[0]

    def chunk(r, carry):
        rows = pl.ds(pl.multiple_of(r * PRENORM_ROWS, PRENORM_ROWS), PRENORM_ROWS)
        xn_ref[rows, :] = (_rms(h_ref[0, rows, :]) * gain + shift).astype(BF16)
        return carry

    lax.fori_loop(0, xn_ref.shape[0] // PRENORM_ROWS, chunk, 0, unroll=4)


EVEN_PARTS = 5


def _even_in_kernel(h_ref, g_ref, sh_ref, sc_ref, w0_ref, w1_ref, w2_ref, w3_ref, w4_ref, fg_ref, cs_ref,
                    a_ref, sga_ref, p_ref, q_ref, sgb_ref, xn_ref):
    @pl.when(pl.program_id(2) == 0)
    def _():
        _prenorm_modulate(h_ref, g_ref, sh_ref, sc_ref, xn_ref)

    gd = a_ref.shape[2]
    w = jnp.concatenate([w0_ref[...], w1_ref[...], w2_ref[...], w3_ref[...], w4_ref[...]], axis=1)
    acc = _dot(xn_ref[...], w)
    part = lambda k: acc[:, k * gd:(k + 1) * gd]
    a_ref[0] = (part(0) * jax.nn.sigmoid(part(1))).astype(BF16)
    sga_ref[0] = _silu(part(2)).astype(BF16)
    bn = _rms(part(3)) * fg_ref[0]
    pq = _dot(bn.astype(BF16), cs_ref[...])
    p_ref[0] = pq[:, :gd].astype(BF16)
    q_ref[0] = pq[:, gd:].astype(BF16)
    sgb_ref[0] = _silu(part(4)).astype(BF16)


def even_in_proj(h, pre_g, shift, scale, w_in, fourier_g, cs_small, tm):
    bsz, length, d = h.shape
    gd = fourier_g.shape[1]
    width = FOURIER_GROUPS * gd
    wspecs = [pl.BlockSpec((d, gd), lambda b, i, n, k=k: (0, k * FOURIER_GROUPS + n)) for k in range(EVEN_PARTS)]
    ospec = pl.BlockSpec((1, tm, gd), lambda b, i, n: (b, i, n))
    vec = pl.BlockSpec((1, 1, d), lambda b, i, n: (b, 0, 0))
    out = jax.ShapeDtypeStruct((bsz, length, width), BF16)
    return pl.pallas_call(
        _even_in_kernel,
        name="even_in_proj",
        grid=(bsz, length // tm, FOURIER_GROUPS),
        in_specs=[
            pl.BlockSpec((1, tm, d), lambda b, i, n: (b, i, 0)),
            pl.BlockSpec((1, d), lambda b, i, n: (0, 0)),
            vec, vec,
            *wspecs,
            pl.BlockSpec((1, 1, gd), lambda b, i, n: (n, 0, 0)),
            pl.BlockSpec((gd, 2 * gd), lambda b, i, n: (0, 0)),
        ],
        out_specs=[ospec] * 5,
        out_shape=[out] * 5,
        scratch_shapes=[pltpu.VMEM((tm, d), BF16)],
        compiler_params=_params(("parallel", "parallel", "arbitrary")),
    )(h, pre_g.reshape(1, d), shift, scale, *([w_in] * EVEN_PARTS),
      fourier_g.reshape(FOURIER_GROUPS, 1, gd), cs_small)


CONV_HALO = 16
CONV_ROWS = 32


def _conv_kernel(a_ref, prev_ref, next_ref, w_ref, cb_ref, lg_ref, lb_ref, sga_ref, o_ref,
                 ext_ref, sh_ref, acc_ref):
    i = pl.program_id(1)
    last = pl.num_programs(1) - 1
    tl, width = acc_ref.shape
    ext_ref[pl.ds(CONV_HALO, tl), :] = a_ref[0].astype(F32)
    ext_ref[pl.ds(0, CONV_HALO), :] = jnp.where(i > 0, prev_ref[0].astype(F32), 0.0)
    ext_ref[pl.ds(CONV_HALO + tl, CONV_HALO), :] = jnp.where(i < last, next_ref[0].astype(F32), 0.0)
    base = CONV_HALO - CONV_K // 2
    span = sh_ref.shape[1]
    for r in range(1, SUBLANES):
        sh_ref[r - 1] = ext_ref[pl.ds(r, span), :]

    for cb in range(width // LANES):
        cols = pl.ds(cb * LANES, LANES)
        taps = [jnp.broadcast_to(w_ref[pl.ds(k, 1), cols], (SUBLANES, LANES)) for k in range(CONV_K)]
        bias = jnp.broadcast_to(cb_ref[:, cols], (SUBLANES, LANES))

        def rows(rb, carry, cols=cols, taps=taps, bias=bias):
            r0 = pl.multiple_of(rb * CONV_ROWS, CONV_ROWS)
            accs = [bias] * (CONV_ROWS // SUBLANES)
            for k in range(CONV_K):
                q, r = divmod(base + k, SUBLANES)
                src = ext_ref if r == 0 else sh_ref.at[r - 1]
                for j in range(len(accs)):
                    accs[j] = accs[j] + src[pl.ds(r0 + (q + j) * SUBLANES, SUBLANES), cols] * taps[k]
            for j, acc in enumerate(accs):
                acc_ref[pl.ds(r0 + j * SUBLANES, SUBLANES), cols] = acc
            return carry

        lax.fori_loop(0, tl // CONV_ROWS, rows, 0, unroll=2)
    x = acc_ref[...]
    mu = jnp.mean(x, axis=-1, keepdims=True)
    xc = x - mu
    var = jnp.mean(xc * xc, axis=-1, keepdims=True)
    y = xc * lax.rsqrt(var + EPS) * lg_ref[...] + lb_ref[...]
    o_ref[0] = (_silu(y) * sga_ref[0].astype(F32)).astype(BF16)


def conv_branch(a, sga, conv_w, conv_b, ln_g, ln_b, tl):
    bsz, length, width = a.shape
    hb = tl // CONV_HALO
    nh = length // CONV_HALO
    row = pl.BlockSpec((1, width), lambda b, i: (0, 0))
    main = pl.BlockSpec((1, tl, width), lambda b, i: (b, i, 0))
    return pl.pallas_call(
        _conv_kernel,
        name="conv_branch",
        grid=(bsz, length // tl),
        in_specs=[
            main,
            pl.BlockSpec((1, CONV_HALO, width), lambda b, i: (b, jnp.maximum(i * hb - 1, 0), 0)),
            pl.BlockSpec((1, CONV_HALO, width), lambda b, i: (b, jnp.minimum((i + 1) * hb, nh - 1), 0)),
            pl.BlockSpec((CONV_K, width), lambda b, i: (0, 0)),
            row, row, row,
            main,
        ],
        out_specs=main,
        out_shape=jax.ShapeDtypeStruct((bsz, length, width), BF16),
        scratch_shapes=[pltpu.VMEM((tl + 2 * CONV_HALO, width), F32),
                        pltpu.VMEM((SUBLANES - 1, tl + 2 * CONV_HALO - SUBLANES, width), F32),
                        pltpu.VMEM((tl, width), F32)],
        compiler_params=_params(("parallel", "parallel")),
    )(a, a, a, conv_w, conv_b.reshape(1, width), ln_g.reshape(1, width), ln_b.reshape(1, width), sga)


DFT_SPLIT = 64


def _dft_gen_kernel(ca_ref, sa_ref, cb_ref, sb_ref, c_ref, s_ref):
    ca, sa = ca_ref[0], sa_ref[0]
    cb, sb = cb_ref[...], sb_ref[...]
    c_ref[...] = (ca * cb - sa * sb).astype(BF16)
    s_ref[...] = (-(sa * cb + ca * sb)).astype(BF16)


def dft_matrices(length):
    na = length // DFT_SPLIT
    k = np.arange(length)[None, :]
    w = 2.0 * math.pi / length
    ang_a = ((DFT_SPLIT * np.arange(na)[:, None] * k) % length) * w
    ang_b = ((np.arange(DFT_SPLIT)[:, None] * k) % length) * w
    ca, sa = (jnp.asarray(f(ang_a).reshape(na, 1, length), F32) for f in (np.cos, np.sin))
    cb, sb = (jnp.asarray(f(ang_b), F32) for f in (np.cos, np.sin))
    tab_a = pl.BlockSpec((1, 1, length), lambda a: (a, 0, 0))
    tab_b = pl.BlockSpec((DFT_SPLIT, length), lambda a: (0, 0))
    out = pl.BlockSpec((DFT_SPLIT, length), lambda a: (a, 0))
    shp = jax.ShapeDtypeStruct((length, length), BF16)
    return pl.pallas_call(
        _dft_gen_kernel,
        name="dft_matrices",
        grid=(na,),
        in_specs=[tab_a, tab_a, tab_b, tab_b],
        out_specs=[out, out],
        out_shape=[shp, shp],
        compiler_params=_params(("parallel",)),
    )(ca, sa, cb, sb)


def _dft_kernel(c_ref, s_ref, p_ref, q_ref, sgb_ref, o_ref, acc_ref, *, scale):
    kk = pl.program_id(2)

    @pl.when(kk == 0)
    def _():
        acc_ref[...] = jnp.zeros_like(acc_ref)

    acc_ref[...] += _dot(c_ref[...], p_ref[0]) + _dot(s_ref[...], q_ref[0])

    @pl.when(kk == pl.num_programs(2) - 1)
    def _():
        o_ref[0] = (acc_ref[...] * scale * sgb_ref[0].astype(F32)).astype(BF16)


def fourier_branch(cmat, smat, p, q, sgb, scale, tm, tk):
    bsz, length, width = p.shape
    return pl.pallas_call(
        functools.partial(_dft_kernel, scale=scale),
        name="fourier_branch",
        grid=(bsz, length // tm, length // tk),
        in_specs=[
            pl.BlockSpec((tm, tk), lambda b, m, k: (m, k)),
            pl.BlockSpec((tm, tk), lambda b, m, k: (m, k)),
            pl.BlockSpec((1, tk, width), lambda b, m, k: (b, k, 0)),
            pl.BlockSpec((1, tk, width), lambda b, m, k: (b, k, 0)),
            pl.BlockSpec((1, tm, width), lambda b, m, k: (b, m, 0)),
        ],
        out_specs=pl.BlockSpec((1, tm, width), lambda b, m, k: (b, m, 0)),
        out_shape=jax.ShapeDtypeStruct((bsz, length, width), BF16),
        scratch_shapes=[pltpu.VMEM((tm, width), F32)],
        compiler_params=_params(("parallel", "parallel", "arbitrary")),
    )(cmat, smat, p, q, sgb)


FFT_MINOR = 64
FFT_COLS = 256


def _fft_kernel(p_ref, q_ref, k1a_ref, k1b_ref, twr_ref, twi_ref, hre_ref, him_ref, sgb_ref, o_ref,
                x_ref, y_ref, z_ref, *, scale):
    length, cw = p_ref.shape[1], p_ref.shape[2]
    ncb = cw // LANES
    n_m = length // FFT_MINOR
    n_a1 = FFT_MINOR // SUBLANES
    n_kb = n_m // SUBLANES
    slab = n_m * SUBLANES
    piece = SUBLANES * SUBLANES
    for cb in range(ncb):
        cols = pl.ds(cb * LANES, LANES)
        x_ref[cb] = p_ref[0, :, cols].astype(F32).reshape(n_m, FFT_MINOR, LANES)
        x_ref[ncb + cb] = q_ref[0, :, cols].astype(F32).reshape(n_m, FFT_MINOR, LANES)

    def slab_of(part, a1):
        tiles = [x_ref[part * ncb + cb, :, pl.ds(a1 * SUBLANES, SUBLANES), :].reshape(slab, LANES)
                 for cb in range(ncb)]
        return jnp.concatenate(tiles, axis=1).astype(BF16)

    for a1 in range(n_a1):
        y = _dot(k1a_ref[...], slab_of(0, a1)) + _dot(k1b_ref[...], slab_of(1, a1))
        tr, ti = twr_ref[a1], twi_ref[a1]
        for cb in range(ncb):
            yre = y[:slab, cb * LANES:(cb + 1) * LANES]
            yim = y[slab:, cb * LANES:(cb + 1) * LANES]
            zre = (tr * yre - ti * yim).astype(BF16)
            zim = (tr * yim + ti * yre).astype(BF16)
            for kb in range(n_kb):
                dst = (pl.ds(a1 * piece, piece), pl.ds(cb * LANES, LANES))
                y_ref[(kb, 0) + dst] = zre[kb * piece:(kb + 1) * piece]
                y_ref[(kb, 1) + dst] = zim[kb * piece:(kb + 1) * piece]

    for kb in range(n_kb):
        out = _dot(hre_ref[...], y_ref[kb, 0]) + _dot(him_ref[...], y_ref[kb, 1])
        z_ref[:, kb] = out.reshape(FFT_MINOR, SUBLANES, cw)

    o_ref[0] = (z_ref[...].reshape(length, cw) * scale * sgb_ref[0].astype(F32)).astype(BF16)


def _fft_tables(length):
    n_a, s = FFT_MINOR, SUBLANES
    n_m = length // n_a
    n_a1 = n_a // s
    eye = np.eye(s)
    ang_m = (np.outer(np.arange(n_m), np.arange(n_m)) % n_m) * (2.0 * math.pi / n_m)
    cm, sm = np.kron(np.cos(ang_m), eye), np.kron(np.sin(ang_m), eye)
    k1a = np.concatenate([cm, -sm], axis=0)
    k1b = np.concatenate([-sm, -cm], axis=0)
    a_full = s * np.arange(n_a1)[:, None, None] + np.arange(s)[None, None, :]
    ang_t = ((np.arange(n_m)[None, :, None] * a_full) % length) * (2.0 * math.pi / length)
    rep = lambda t: np.broadcast_to(t.reshape(n_a1, n_m * s, 1), (n_a1, n_m * s, LANES))
    twr, twi = rep(np.cos(ang_t)), rep(-np.sin(ang_t))
    ang_a = ((np.arange(n_a)[:, None, None] * a_full.reshape(1, n_a1, s)) % n_a) * (2.0 * math.pi / n_a)
    spread = lambda t: np.einsum("kas,pq->kpaqs", t, eye).reshape(n_a * s, n_a1 * s * s)
    hre, him = spread(np.cos(ang_a)), spread(np.sin(ang_a))
    return ([jnp.asarray(t, BF16) for t in (k1a, k1b)] + [jnp.asarray(t, F32) for t in (twr, twi)]
            + [jnp.asarray(t, BF16) for t in (hre, him)])


def fourier_branch_fft(p, q, sgb, scale):
    bsz, length, width = p.shape
    tables = _fft_tables(length)
    n_m = length // FFT_MINOR
    tok = pl.BlockSpec((1, length, FFT_COLS), lambda b, n: (b, 0, n))
    const = lambda t: pl.BlockSpec(t.shape, lambda b, n, nd=t.ndim: (0,) * nd)
    return pl.pallas_call(
        functools.partial(_fft_kernel, scale=scale),
        name="fourier_branch_fft",
        grid=(bsz, width // FFT_COLS),
        in_specs=[tok, tok] + [const(t) for t in tables] + [tok],
        out_specs=tok,
        out_shape=jax.ShapeDtypeStruct((bsz, length, width), BF16),
        scratch_shapes=[pltpu.VMEM((2 * FFT_COLS // LANES, n_m, FFT_MINOR, LANES), F32),
                        pltpu.VMEM((n_m // SUBLANES, 2, FFT_MINOR * SUBLANES, FFT_COLS), BF16),
                        pltpu.VMEM((FFT_MINOR, n_m // SUBLANES, SUBLANES, FFT_COLS), F32)],
        compiler_params=_params(("parallel", "parallel")),
    )(p, q, *tables, sgb)


def _postnorm_residual(y, h_ref, pg_ref, gt_ref, o_ref):
    o_ref[0] = h_ref[0] + gt_ref[0] * (_rms(y) * pg_ref[...])


def _even_out_kernel(a_ref, b_ref, wa_ref, wb_ref, h_ref, pg_ref, gt_ref, o_ref):
    y = _dot(a_ref[0], wa_ref[...]) + _dot(b_ref[0], wb_ref[...])
    _postnorm_residual(y, h_ref, pg_ref, gt_ref, o_ref)


def even_out_proj(a, b, wa, wb, h, post_g, gate, tm):
    bsz, length, d = h.shape
    width = a.shape[2]
    half = pl.BlockSpec((1, tm, width), lambda bb, i: (bb, i, 0))
    wspec = pl.BlockSpec((width, d), lambda bb, i: (0, 0))
    full = pl.BlockSpec((1, tm, d), lambda bb, i: (bb, i, 0))
    return pl.pallas_call(
        _even_out_kernel,
        name="even_out_proj",
        grid=(bsz, length // tm),
        in_specs=[half, half, wspec, wspec, full,
                  pl.BlockSpec((1, d), lambda bb, i: (0, 0)),
                  pl.BlockSpec((1, 1, d), lambda bb, i: (bb, 0, 0))],
        out_specs=full,
        out_shape=jax.ShapeDtypeStruct((bsz, length, d), F32),
        compiler_params=_params(("parallel", "parallel")),
    )(a, b, wa, wb, h, post_g.reshape(1, d), gate)


ODD_TN = 1024


ODD_PARTS = 2


def _odd_in_kernel(h_ref, g_ref, sh_ref, sc_ref, w_ref, o_ref, xn_ref, *, gate_parts):
    n = pl.program_id(2)

    @pl.when(n == 0)
    def _():
        _prenorm_modulate(h_ref, g_ref, sh_ref, sc_ref, xn_ref)

    xn = xn_ref[...]
    pw = ODD_TN // ODD_PARTS
    for j in range(ODD_PARTS):
        cols = pl.ds(j * pw, pw)
        acc = _dot(xn, w_ref[:, cols])
        tiles = [g // ODD_PARTS for g in gate_parts if g % ODD_PARTS == j]
        if tiles:
            is_gate = functools.reduce(jnp.logical_or, [n == t for t in tiles])
            acc = jnp.where(is_gate, _silu(acc), acc)
        o_ref[0, :, cols] = acc.astype(BF16)


def odd_in_proj(h, pre_g, shift, scale, w_all, gate_ranges, tm):
    bsz, length, d = h.shape
    n_all = w_all.shape[1]
    pw = ODD_TN // ODD_PARTS
    assert n_all % ODD_TN == 0 and all(s % pw == 0 and e % pw == 0 for s, e in gate_ranges)
    gate_parts = tuple(p for s, e in gate_ranges for p in range(s // pw, e // pw))
    vec = pl.BlockSpec((1, 1, d), lambda b, i, n: (b, 0, 0))
    return pl.pallas_call(
        functools.partial(_odd_in_kernel, gate_parts=gate_parts),
        name="odd_in_proj",
        grid=(bsz, length // tm, n_all // ODD_TN),
        in_specs=[pl.BlockSpec((1, tm, d), lambda b, i, n: (b, i, 0)),
                  pl.BlockSpec((1, d), lambda b, i, n: (0, 0)),
                  vec, vec,
                  pl.BlockSpec((d, ODD_TN), lambda b, i, n: (0, n))],
        out_specs=pl.BlockSpec((1, tm, ODD_TN), lambda b, i, n: (b, i, n)),
        out_shape=jax.ShapeDtypeStruct((bsz, length, n_all), BF16),
        scratch_shapes=[pltpu.VMEM((tm, d), BF16)],
        compiler_params=_params(("parallel", "parallel", "arbitrary")),
    )(h, pre_g.reshape(1, d), shift, scale, w_all)


def _na_pair_geometry(variant, rows):
    r0 = {0: NA_QROWS, 1: 0, 2: rows - NA_QROWS}[variant]
    ws = min(max(r0 - NA_ROWS // 2, 0), rows - NA_WIN_ROWS)
    return r0, ws


def _na_bias_kernel(rpb_ref, o_ref, *, rows):
    h = pl.program_id(0)
    n_dr, n_dc = 2 * NA_ROWS - 1, 2 * NA_COLS - 1
    qc = lax.broadcasted_iota(jnp.int32, (GRID_W, LANES), 0)
    lane = lax.broadcasted_iota(jnp.int32, (GRID_W, LANES), 1)
    kc = lane % GRID_W
    hi = lane // GRID_W
    diff = kc - qc
    c_start = jnp.clip(qc - NA_COLS // 2, 0, GRID_W - NA_COLS)
    col_ok = jnp.where(kc >= c_start, 1, 0) * jnp.where(kc < c_start + NA_COLS, 1, 0)
    blocks = []
    for dr in range(n_dr):
        val = jnp.full((GRID_W, LANES), NEG_INF, F32)
        for dc in range(n_dc):
            val = jnp.where(diff == dc - (NA_COLS - 1), rpb_ref[(h * n_dr + dr) * n_dc + dc], val)
        blocks.append(jnp.where(col_ok == 1, val, NEG_INF))
    masked = jnp.full((GRID_W, LANES), NEG_INF, F32)
    for variant in range(NA_VARIANTS):
        r0, ws = _na_pair_geometry(variant, rows)
        for ri in range(NA_QROWS):
            r = r0 + ri
            r_start = min(max(r - NA_ROWS // 2, 0), rows - NA_ROWS)
            for wp in range(NA_WIN_ROWS // 2):
                halves = []
                for a in (ws + 2 * wp, ws + 2 * wp + 1):
                    in_window = r_start <= a < r_start + NA_ROWS
                    halves.append(blocks[a - r + NA_ROWS - 1] if in_window else masked)
                o_ref[0, variant, pl.ds(ri * GRID_W, GRID_W), pl.ds(wp * LANES, LANES)] = jnp.where(
                    hi == 1, halves[1], halves[0])


def na_bias_table(rpb, rows):
    heads = rpb.shape[0]
    return pl.pallas_call(
        functools.partial(_na_bias_kernel, rows=rows),
        name="na_bias_table",
        grid=(heads,),
        in_specs=[pl.BlockSpec(memory_space=pltpu.SMEM)],
        out_specs=pl.BlockSpec((1, NA_VARIANTS, NA_PAIR, NA_WIN), lambda h: (h, 0, 0, 0)),
        out_shape=jax.ShapeDtypeStruct((heads, NA_VARIANTS, NA_PAIR, NA_WIN), F32),
        compiler_params=_params(("parallel",)),
    )(rpb.reshape(-1))


def _na_kernel(q_ref, k_ref, v_ref, kc_ref, vc_ref, bias_ref, sg_ref, o_ref,
               vx_ref, vcx_ref, s0_ref, s1_ref, p0_ref, p1_ref, *, rows):
    npairs = rows // NA_QROWS
    last = npairs - 1
    n_heads = q_ref.shape[2] // HEAD_DIM
    nt = (((1,), (1,)), ((), ()))
    head = lambda hh: pl.ds(hh * HEAD_DIM, HEAD_DIM)
    for hh in range(n_heads):
        vx_ref[hh, :, :HEAD_DIM] = v_ref[0, :, head(hh)]
        vx_ref[hh, :, HEAD_DIM:] = jnp.ones((vx_ref.shape[1], HEAD_DIM), BF16)
        vcx_ref[hh, :, :HEAD_DIM] = vc_ref[0, :, head(hh)]
        vcx_ref[hh, :, HEAD_DIM:] = jnp.ones((vcx_ref.shape[1], HEAD_DIM), BF16)

    def window(pr):
        ws = min(max(NA_QROWS * pr - NA_ROWS // 2, 0), rows - NA_WIN_ROWS)
        return pl.ds(ws * GRID_W, NA_WIN)

    def qrows(pr):
        return pl.ds(pr * NA_PAIR, NA_PAIR)

    def scores(unit, s_ref):
        hh, pr = divmod(unit, npairs)
        variant = {0: 1, last: 2}.get(pr, 0)
        q = q_ref[0, qrows(pr), head(hh)]
        s_ref[:, :NA_WIN] = lax.dot_general(q, k_ref[0, window(pr), head(hh)], nt,
                                            preferred_element_type=F32) + bias_ref[hh, variant]
        s_ref[:, NA_WIN:] = lax.dot_general(q, kc_ref[0, :, head(hh)], nt, preferred_element_type=F32)

    def probs(s_ref, p_ref):
        s = s_ref[...]
        p_ref[...] = jnp.exp(s - jnp.max(s, axis=-1, keepdims=True)).astype(BF16)

    def values(unit, p_ref):
        hh, pr = divmod(unit, npairs)
        acc = _dot(p_ref[:, :NA_WIN], vx_ref[hh, window(pr), :]) + _dot(p_ref[:, NA_WIN:], vcx_ref[hh])
        o = acc[:, :HEAD_DIM] / acc[:, HEAD_DIM:]
        o_ref[0, qrows(pr), head(hh)] = (o * sg_ref[0, qrows(pr), head(hh)].astype(F32)).astype(BF16)

    units = n_heads * npairs
    bufs = ((s0_ref, p0_ref), (s1_ref, p1_ref))
    scores(0, s0_ref)
    scores(1, s1_ref)
    probs(s0_ref, p0_ref)
    for i in range(units):
        s_cur, p_cur = bufs[i % 2]
        s_nxt, p_nxt = bufs[(i + 1) % 2]
        values(i, p_cur)
        if i + 1 < units:
            probs(s_nxt, p_nxt)
        if i + 2 < units:
            scores(i + 2, s_cur)


def neighbourhood_attention(proj, proj_c, bias, na_w, gate_col):
    bsz, length, _ = proj.shape
    lc = proj_c.shape[1]
    heads = na_w // HEAD_DIM
    rows = length // GRID_W
    nh = NA_HEADS_PER_STEP
    assert heads % nh == 0 and gate_col % (nh * HEAD_DIM) == 0
    groups = heads // nh
    seq = lambda off: pl.BlockSpec((1, length, nh * HEAD_DIM), lambda h, b: (b, 0, off + h))
    cseq = lambda off: pl.BlockSpec((1, lc, nh * HEAD_DIM), lambda h, b: (b, 0, off + h))
    return pl.pallas_call(
        functools.partial(_na_kernel, rows=rows),
        name="neighbourhood_attention",
        grid=(groups, bsz),
        in_specs=[seq(0), seq(groups), seq(2 * groups), cseq(groups), cseq(2 * groups),
                  pl.BlockSpec((nh, NA_VARIANTS, NA_PAIR, NA_WIN), lambda h, b: (h, 0, 0, 0)),
                  seq(gate_col // (nh * HEAD_DIM))],
        out_specs=seq(0),
        out_shape=jax.ShapeDtypeStruct((bsz, length, na_w), BF16),
        scratch_shapes=[pltpu.VMEM((nh, length, 2 * HEAD_DIM), BF16), pltpu.VMEM((nh, lc, 2 * HEAD_DIM), BF16),
                        pltpu.VMEM((NA_PAIR, NA_WIN + lc), F32), pltpu.VMEM((NA_PAIR, NA_WIN + lc), F32),
                        pltpu.VMEM((NA_PAIR, NA_WIN + lc), BF16), pltpu.VMEM((NA_PAIR, NA_WIN + lc), BF16)],
        compiler_params=_params(("parallel", "parallel")),
    )(proj, proj, proj, proj_c, proj_c, bias, proj)


S5_CHUNK = 128
S5_IN_BLOCK = LANES
S5_OUT_BLOCK = 256
S5_SCAN_COLS = 1024


def _s5_kernel(ufc_ref, ufx_ref, ubc_ref, ubx_ref, bf_ref, bb_ref, crf_ref, cif_ref, crb_ref, cib_ref,
               lre_ref, lim_ref, yf_ref, yb_ref, buf_ref, state_ref, stage_ref, *, ctx_chunks):
    bsz, t_len, width = ufx_ref.shape
    n_state = lre_ref.shape[1]
    in_ctx = pl.program_id(0) < ctx_chunks
    n_in = width // S5_IN_BLOCK
    blk = n_state // n_in
    rows8 = 2 * bsz

    @pl.when(pl.program_id(0) == 0)
    def _():
        state_ref[...] = jnp.zeros_like(state_ref)

    nlb = n_state // LANES
    per_in = blk // LANES
    ti = lax.broadcasted_iota(jnp.int32, (t_len, t_len), 0)
    tj = lax.broadcasted_iota(jnp.int32, (t_len, t_len), 1)
    rev = jnp.where(ti + tj == t_len - 1, 1.0, 0.0).astype(BF16)
    u_fwd = jnp.where(in_ctx, ufc_ref[...], ufx_ref[...]).reshape(bsz * t_len, width)
    u_nat = jnp.where(in_ctx, ubc_ref[...], ubx_ref[...])
    u_bwd = jnp.concatenate([_dot(rev, u_nat[b]).astype(BF16) for b in range(bsz)], axis=0)
    sides = ((u_fwd, bf_ref, yf_ref, crf_ref, cif_ref), (u_bwd, bb_ref, yb_ref, crb_ref, cib_ref))

    def project(kb):
        for di, (u, w_ref, _, _, _) in enumerate(sides):
            res = _dot(u[:, kb * S5_IN_BLOCK:(kb + 1) * S5_IN_BLOCK], w_ref[kb])
            for b in range(bsz):
                rsel = pl.ds(di * bsz + b, t_len, stride=rows8)
                for c in range(2 * per_in):
                    dst = (c // per_in) * nlb + kb * per_in + c % per_in
                    buf_ref[dst, rsel, :] = res[b * t_len:(b + 1) * t_len, c * LANES:(c + 1) * LANES]

    per = S5_SCAN_COLS // LANES

    def scan(group, t0, t1, carry):
        blocks = [group * per + i for i in range(per)]
        lre = [lre_ref[:, pl.ds(c * LANES, LANES)] for c in blocks]
        lim = [lim_ref[:, pl.ds(c * LANES, LANES)] for c in blocks]
        if carry is None:
            carry = [state_ref[:, pl.ds(part * n_state + c * LANES, LANES)] for c in blocks for part in range(2)]
        for t in range(t0, t1):
            row = pl.ds(t * rows8, rows8)
            for i, c in enumerate(blocks):
                hre, him = carry[2 * i], carry[2 * i + 1]
                carry[2 * i] = lre[i] * hre - lim[i] * him + buf_ref[c, row, :]
                carry[2 * i + 1] = lre[i] * him + lim[i] * hre + buf_ref[nlb + c, row, :]
                buf_ref[c, row, :] = carry[2 * i]
                buf_ref[nlb + c, row, :] = carry[2 * i + 1]
        if t1 == t_len:
            for i, c in enumerate(blocks):
                state_ref[:, pl.ds(c * LANES, LANES)] = carry[2 * i]
                state_ref[:, pl.ds(n_state + c * LANES, LANES)] = carry[2 * i + 1]
        return carry

    n_out = width // S5_OUT_BLOCK
    per_out = nlb // n_out

    def readout(di, j):
        _, _, y_ref, cr_ref, ci_ref = sides[di]
        stage = stage_ref.at[di * n_out + j]
        for part in range(2):
            for b in range(bsz):
                rsel = pl.ds(di * bsz + b, t_len, stride=rows8)
                for c in range(per_out):
                    stage[part, pl.ds(b * t_len, t_len), pl.ds(c * LANES, LANES)] = buf_ref[
                        part * nlb + j * per_out + c, rsel, :].astype(BF16)
        y = _dot(stage[0], cr_ref[j]) + _dot(stage[1], ci_ref[j])
        for b in range(bsz):
            yb = y[b * t_len:(b + 1) * t_len]
            if di == 1:
                hi = yb.astype(BF16)
                r1 = yb - hi.astype(F32)
                mid = r1.astype(BF16)
                lo = (r1 - mid.astype(F32)).astype(BF16)
                yb = _dot(rev, hi) + _dot(rev, mid) + _dot(rev, lo)
            y_ref[b, :, pl.ds(j * S5_OUT_BLOCK, S5_OUT_BLOCK)] = yb

    assert n_in == 4 and nlb // per == 2 and n_out == 2 and per_out == per
    half = t_len // 2
    project(0)
    project(1)
    carry = scan(0, 0, half, None)
    project(2)
    scan(0, half, t_len, carry)
    project(3)
    carry = scan(1, 0, half, None)
    readout(0, 0)
    scan(1, half, t_len, carry)
    readout(1, 0)
    readout(0, 1)
    readout(1, 1)


def _block_diag(x):
    nblk, g, r, c = x.shape
    eye = jnp.eye(g, dtype=x.dtype)
    return jnp.einsum("kgrc,gh->kgrhc", x, eye).reshape(nblk, g * r, g * c)


def _s5_direction_params(a_re, a_im, log_dt, b_re, b_im, c_re, c_im):
    groups, n_p = a_re.shape
    lam = lax.complex(a_re.astype(F32), a_im.astype(F32))
    dt = jnp.exp(log_dt.astype(F32))[:, None]
    lam_bar = jnp.exp(lam * dt)
    b_bar = ((lam_bar - 1.0) / lam)[..., None] * lax.complex(b_re.astype(F32), b_im.astype(F32))
    gi = S5_IN_BLOCK // SSM_GROUP
    bt = jnp.swapaxes(b_bar, 1, 2).reshape(groups // gi, gi, SSM_GROUP, n_p)
    b_mat = jnp.concatenate([_block_diag(jnp.real(bt)), _block_diag(jnp.imag(bt))], axis=-1)
    go = S5_OUT_BLOCK // SSM_GROUP
    ct = lambda c: jnp.swapaxes(c.astype(F32), 1, 2).reshape(groups // go, go, n_p, SSM_GROUP)
    return (b_mat.astype(BF16), _block_diag(ct(c_re)).astype(BF16), _block_diag(-ct(c_im)).astype(BF16),
            jnp.real(lam_bar).reshape(-1), jnp.imag(lam_bar).reshape(-1))


def s5_scan(proj_c, proj_x, d_col, width, fwd, bwd):
    bsz, lc, _ = proj_c.shape
    length = proj_x.shape[1]
    t_len = S5_CHUNK
    ncc, ncx = lc // t_len, length // t_len
    nc = ncc + ncx
    col = d_col // width
    n_state = fwd[3].shape[0]
    lre = jnp.concatenate([jnp.broadcast_to(fwd[3], (bsz, n_state)), jnp.broadcast_to(bwd[3], (bsz, n_state))])
    lim = jnp.concatenate([jnp.broadcast_to(fwd[4], (bsz, n_state)), jnp.broadcast_to(bwd[4], (bsz, n_state))])
    full = lambda x: pl.BlockSpec(x.shape, lambda c, nd=x.ndim: (0,) * nd)
    blk = lambda index: pl.BlockSpec((bsz, t_len, width), index)
    in_blocks = [blk(lambda c: (0, jnp.minimum(c, ncc - 1), col)),
                 blk(lambda c: (0, jnp.maximum(c - ncc, 0), col)),
                 blk(lambda c: (0, jnp.maximum(ncc - 1 - c, 0), col)),
                 blk(lambda c: (0, jnp.minimum(nc - 1 - c, ncx - 1), col))]
    out = jax.ShapeDtypeStruct((bsz, length, width), F32)
    consts = (fwd[0], bwd[0], fwd[1], fwd[2], bwd[1], bwd[2], lre, lim)
    return pl.pallas_call(
        functools.partial(_s5_kernel, ctx_chunks=ncc),
        name="s5_scan",
        grid=(nc,),
        in_specs=in_blocks + [full(x) for x in consts],
        out_specs=[blk(lambda c: (0, jnp.maximum(c - ncc, 0), 0)),
                   blk(lambda c: (0, jnp.minimum(nc - 1 - c, ncx - 1), 0))],
        out_shape=[out, out],
        scratch_shapes=[pltpu.VMEM((2 * n_state // LANES, t_len * 2 * bsz, LANES), F32),
                        pltpu.VMEM((2 * bsz, 2 * n_state), F32),
                        pltpu.VMEM((2 * (width // S5_OUT_BLOCK), 2, bsz * t_len,
                                    n_state // (width // S5_OUT_BLOCK)), BF16)],
        compiler_params=_params(("arbitrary",)),
    )(proj_c, proj_x, proj_c, proj_x, *consts)


def _odd_out_kernel(na_ref, yf_ref, yb_ref, d_ref, sdg_ref, dsk_ref, wglu_ref, wna_ref, wssm_ref,
                    h_ref, pg_ref, gt_ref, o_ref):
    y = yf_ref[0] + yb_ref[0] + dsk_ref[...] * d_ref[0].astype(F32)
    y = 0.5 * y * (1.0 + jnp.tanh(math.sqrt(2.0 / math.pi) * (y + 0.044715 * (y * y * y))))
    z = y * jax.nn.sigmoid(_dot(y.astype(BF16), wglu_ref[...]))
    s = (z * sdg_ref[0].astype(F32)).astype(BF16)
    out = _dot(na_ref[0], wna_ref[...]) + _dot(s, wssm_ref[...])
    _postnorm_residual(out, h_ref, pg_ref, gt_ref, o_ref)


def odd_out_proj(na, yf, yb, proj, d_col, dg_col, d_skip, w_glu, w_na, w_ssm, h, post_g, gate, tm):
    bsz, length, dm = h.shape
    na_w, ssm_w = na.shape[2], yf.shape[2]
    tok = lambda w, col=0: pl.BlockSpec((1, tm, w), lambda b, i: (b, i, col // w))
    const = lambda r, c: pl.BlockSpec((r, c), lambda b, i: (0, 0))
    return pl.pallas_call(
        _odd_out_kernel,
        name="odd_out_proj",
        grid=(bsz, length // tm),
        in_specs=[tok(na_w), tok(ssm_w), tok(ssm_w),
                  tok(ssm_w, d_col), tok(ssm_w, dg_col),
                  const(1, ssm_w), const(ssm_w, ssm_w), const(na_w, dm), const(ssm_w, dm),
                  tok(dm), const(1, dm),
                  pl.BlockSpec((1, 1, dm), lambda b, i: (b, 0, 0))],
        out_specs=tok(dm),
        out_shape=jax.ShapeDtypeStruct((bsz, length, dm), F32),
        compiler_params=_params(("parallel", "parallel")),
    )(na, yf, yb, proj, proj, d_skip.reshape(1, ssm_w), w_glu, w_na, w_ssm, h, post_g.reshape(1, dm), gate)


def _small_dft(n):
    ang = (np.outer(np.arange(n), np.arange(n)) % n) * (2.0 * math.pi / n)
    return jnp.asarray(np.concatenate([np.cos(ang), np.sin(ang)], axis=1), BF16)


def _conv_fourier_layer(h, pre_g, post_g, shift, scale, gate, w5, w_out_a, w_out_b, conv_w, conv_b,
                        ln_g, ln_b, fourier_g, cs_small, shared_mod):
    bsz, length, _ = h.shape
    gd = fourier_g.shape[1]
    fold = (lambda t: t.reshape(1, bsz * length, t.shape[2])) if shared_mod else (lambda t: t)
    unfold = (lambda t: t.reshape(bsz, length, t.shape[2])) if shared_mod else (lambda t: t)
    vec = (lambda v: v[:1]) if shared_mod else (lambda v: v)
    tokens = bsz * length if shared_mod else length
    outs = even_in_proj(fold(h), pre_g, vec(shift), vec(scale), w5, fourier_g, cs_small,
                        _token_tile(tokens, IN_PROJ_ROWS))
    a, sga, p, q, sgb = [unfold(t) for t in outs]
    a = conv_branch(a, sga, conv_w, conv_b, ln_g, ln_b, _token_tile(length, CONV_TILE_ROWS))
    ortho = 1.0 / math.sqrt(length * gd)
    if (length // FFT_MINOR) % SUBLANES == 0:
        f = fourier_branch_fft(p, q, sgb, ortho)
    else:
        cmat, smat = dft_matrices(length)
        f = fourier_branch(cmat, smat, p, q, sgb, ortho, length, length)
    return unfold(even_out_proj(fold(a), fold(f), w_out_a, w_out_b, fold(h), post_g, vec(gate),
                                _token_tile(tokens, OUT_PROJ_ROWS)))


def kernel(x, c, ctx, c_ctx, pre_g, post_g, ada_w, ada_b, ab_w_in, ab_w_out, conv_w, conv_b, conv_ln_g,
           conv_ln_b, fourier_g, cd_w_in, cd_w_out, na_rpb, s5_a_re, s5_a_im, s5_log_dt, s5_b_re, s5_b_im,
           s5_c_re, s5_c_im, s5_d, s5_w_glu):
    bsz, length, d = x.shape
    lc = ctx.shape[1]
    depth = ada_w.shape[0]
    assert depth == 2 and length % NA_PAIR == 0 and length // GRID_W >= NA_WIN_ROWS
    assert lc % S5_CHUNK == 0 and length % S5_CHUNK == 0

    cond_rows = jnp.zeros((SUBLANES, d), F32).at[:bsz].set(c).at[bsz].set(c_ctx)
    mods = ada_modulation(cond_rows, ada_w, ada_b)

    def mod_vectors(i):
        xs = [mods[i, :bsz, k * d:(k + 1) * d][:, None, :] for k in range(3)]
        cs = [jnp.broadcast_to(mods[i, bsz, k * d:(k + 1) * d], (bsz, 1, d)) for k in range(3)]
        return xs, cs

    (sh_x, sc_x, gt_x), (sh_c, sc_c, gt_c) = mod_vectors(0)
    w_in = ab_w_in[0]
    conv_width = conv_w.shape[2]
    cw = conv_width
    fw = (w_in.shape[1] - 3 * cw) // 2
    assert cw == fw
    w5 = w_in.astype(BF16)
    w_out_a = ab_w_out[0, :cw].astype(BF16)
    w_out_b = ab_w_out[0, cw:].astype(BF16)
    cs_small = _small_dft(fourier_g.shape[2])
    layer0 = functools.partial(
        _conv_fourier_layer, pre_g=pre_g[0], post_g=post_g[0], w5=w5, w_out_a=w_out_a, w_out_b=w_out_b,
        conv_w=conv_w[0], conv_b=conv_b[0], ln_g=conv_ln_g[0], ln_b=conv_ln_b[0], fourier_g=fourier_g[0],
        cs_small=cs_small)
    h_x = layer0(x, shift=sh_x, scale=sc_x, gate=gt_x, shared_mod=False)
    h_c = layer0(ctx, shift=sh_c, scale=sc_c, gate=gt_c, shared_mod=True)

    (sh_x, sc_x, gt_x), (sh_c, sc_c, _) = mod_vectors(1)
    w_in = cd_w_in[0]
    ssm_w = s5_d.shape[1]
    na_w = (w_in.shape[1] - 2 * ssm_w) // 4
    g_col, d_col, dg_col = 3 * na_w, 4 * na_w, 4 * na_w + ssm_w
    gates = ((g_col, d_col), (dg_col, dg_col + ssm_w))
    col_scale = jnp.where(jnp.arange(w_in.shape[1]) < na_w, HEAD_DIM ** -0.5, 1.0).astype(F32)
    w_all = (w_in * col_scale).astype(BF16)
    proj_x = odd_in_proj(h_x, pre_g[1], sh_x, sc_x, w_all, gates, _token_tile(length, IN_PROJ_ROWS))
    proj_c = odd_in_proj(h_c.reshape(1, bsz * lc, d), pre_g[1], sh_c[:1], sc_c[:1], w_all, gates,
                         _token_tile(bsz * lc, IN_PROJ_ROWS)).reshape(bsz, lc, -1)

    bias = na_bias_table(na_rpb[0], length // GRID_W)
    na = neighbourhood_attention(proj_x, proj_c, bias, na_w, g_col)

    dirs = [_s5_direction_params(s5_a_re[0, i], s5_a_im[0, i], s5_log_dt[0, i], s5_b_re[0, i], s5_b_im[0, i],
                                 s5_c_re[0, i], s5_c_im[0, i]) for i in range(2)]
    y_f, y_b = s5_scan(proj_c, proj_x, d_col, ssm_w, *dirs)

    return odd_out_proj(na, y_f, y_b, proj_x, d_col, dg_col, s5_d[0], s5_w_glu[0].astype(BF16),
                        cd_w_out[0, :na_w].astype(BF16), cd_w_out[0, na_w:].astype(BF16),
                        h_x, post_g[1], gt_x, _token_tile(length, OUT_PROJ_ROWS))
```

```python
import functools
import math

import jax
import jax.numpy as jnp
import numpy as np
from jax import lax
from jax.experimental import pallas as pl
from jax.experimental.pallas import tpu as pltpu

F32 = jnp.float32
BF16 = jnp.bfloat16

EPS = 1e-6
NEG_INF = -1e30

GRID_W = 64
CONV_K = 31
FOURIER_GROUPS = 4
HEAD_DIM = 128
NA_ROWS = 8
NA_COLS = 16
SSM_GROUP = 16
SSM_STATE = 64

LANES = 128
SUBLANES = 8
VMEM_LIMIT = 56 * 1024 * 1024

IN_PROJ_ROWS = 1024
OUT_PROJ_ROWS = 512
CONV_TILE_ROWS = 256
ADA_COLS = 1024


def _token_tile(n_tokens, cap):
    tile = min(cap, n_tokens)
    assert n_tokens % tile == 0
    return tile

NA_QROWS = 4
NA_PAIR = NA_QROWS * GRID_W
NA_WIN_ROWS = NA_ROWS + NA_QROWS
NA_WIN = NA_WIN_ROWS * GRID_W
NA_VARIANTS = 3
NA_HEADS_PER_STEP = 2


def _params(sem, vmem=VMEM_LIMIT):
    return pltpu.CompilerParams(dimension_semantics=sem, vmem_limit_bytes=vmem)


def _silu(x):
    return x * jax.nn.sigmoid(x)


def _rms(x):
    return x * lax.rsqrt(jnp.mean(x * x, axis=-1, keepdims=True) + EPS)


def _dot(a, b):
    return jnp.dot(a, b, preferred_element_type=F32)


def _ada_kernel(c_ref, w_ref, b_ref, o_ref):
    cond = _silu(c_ref[...])
    o_ref[0] = _dot(cond.astype(BF16), w_ref[0].astype(BF16)) + b_ref[0]


def ada_modulation(cond_rows, ada_w, ada_b):
    depth, d, n = ada_w.shape
    tn = _token_tile(n, ADA_COLS)
    rows = cond_rows.shape[0]
    return pl.pallas_call(
        _ada_kernel,
        name="ada_modulation",
        grid=(depth, n // tn),
        in_specs=[
            pl.BlockSpec((rows, d), lambda i, j: (0, 0)),
            pl.BlockSpec((1, d, tn), lambda i, j: (i, 0, j)),
            pl.BlockSpec((1, 1, tn), lambda i, j: (i, 0, j)),
        ],
        out_specs=pl.BlockSpec((1, rows, tn), lambda i, j: (i, 0, j)),
        out_shape=jax.ShapeDtypeStruct((depth, rows, n), F32),
        compiler_params=_params(("parallel", "parallel")),
    )(cond_rows, ada_w, ada_b.reshape(depth, 1, n))


PRENORM_ROWS = 16


def _prenorm_modulate(h_ref, g_ref, sh_ref, sc_ref, xn_ref):
    gain = g_ref[...] * (1.0 + sc_ref[0])
    shift = sh_ref[0]

    def chunk(r, carry):
        rows = pl.ds(pl.multiple_of(r * PRENORM_ROWS, PRENORM_ROWS), PRENORM_ROWS)
        xn_ref[rows, :] = (_rms(h_ref[0, rows, :]) * gain + shift).astype(BF16)
        return carry

    lax.fori_loop(0, xn_ref.shape[0] // PRENORM_ROWS, chunk, 0, unroll=4)


EVEN_PARTS = 5


def _even_in_kernel(h_ref, g_ref, sh_ref, sc_ref, w0_ref, w1_ref, w2_ref, w3_ref, w4_ref, fg_ref, cs_ref,
                    a_ref, sga_ref, p_ref, q_ref, sgb_ref, xn_ref):
    @pl.when(pl.program_id(2) == 0)
    def _():
        _prenorm_modulate(h_ref, g_ref, sh_ref, sc_ref, xn_ref)

    gd = a_ref.shape[2]
    w = jnp.concatenate([w0_ref[...], w1_ref[...], w2_ref[...], w3_ref[...], w4_ref[...]], axis=1)
    acc = _dot(xn_ref[...], w)
    part = lambda k: acc[:, k * gd:(k + 1) * gd]
    a_ref[0] = (part(0) * jax.nn.sigmoid(part(1))).astype(BF16)
    sga_ref[0] = _silu(part(2)).astype(BF16)
    bn = _rms(part(3)) * fg_ref[0]
    pq = _dot(bn.astype(BF16), cs_ref[...])
    p_ref[0] = pq[:, :gd].astype(BF16)
    q_ref[0] = pq[:, gd:].astype(BF16)
    sgb_ref[0] = _silu(part(4)).astype(BF16)


def even_in_proj(h, pre_g, shift, scale, w_in, fourier_g, cs_small, tm):
    bsz, length, d = h.shape
    gd = fourier_g.shape[1]
    width = FOURIER_GROUPS * gd
    wspecs = [pl.BlockSpec((d, gd), lambda b, i, n, k=k: (0, k * FOURIER_GROUPS + n)) for k in range(EVEN_PARTS)]
    ospec = pl.BlockSpec((1, tm, gd), lambda b, i, n: (b, i, n))
    vec = pl.BlockSpec((1, 1, d), lambda b, i, n: (b, 0, 0))
    out = jax.ShapeDtypeStruct((bsz, length, width), BF16)
    return pl.pallas_call(
        _even_in_kernel,
        name="even_in_proj",
        grid=(bsz, length // tm, FOURIER_GROUPS),
        in_specs=[
            pl.BlockSpec((1, tm, d), lambda b, i, n: (b, i, 0)),
            pl.BlockSpec((1, d), lambda b, i, n: (0, 0)),
            vec, vec,
            *wspecs,
            pl.BlockSpec((1, 1, gd), lambda b, i, n: (n, 0, 0)),
            pl.BlockSpec((gd, 2 * gd), lambda b, i, n: (0, 0)),
        ],
        out_specs=[ospec] * 5,
        out_shape=[out] * 5,
        scratch_shapes=[pltpu.VMEM((tm, d), BF16)],
        compiler_params=_params(("parallel", "parallel", "arbitrary")),
    )(h, pre_g.reshape(1, d), shift, scale, *([w_in] * EVEN_PARTS),
      fourier_g.reshape(FOURIER_GROUPS, 1, gd), cs_small)


CONV_HALO = 16
CONV_ROWS = 32


def _conv_kernel(a_ref, prev_ref, next_ref, w_ref, cb_ref, lg_ref, lb_ref, sga_ref, o_ref,
                 ext_ref, sh_ref, acc_ref):
    i = pl.program_id(1)
    last = pl.num_programs(1) - 1
    tl, width = acc_ref.shape
    ext_ref[pl.ds(CONV_HALO, tl), :] = a_ref[0].astype(F32)
    ext_ref[pl.ds(0, CONV_HALO), :] = jnp.where(i > 0, prev_ref[0].astype(F32), 0.0)
    ext_ref[pl.ds(CONV_HALO + tl, CONV_HALO), :] = jnp.where(i < last, next_ref[0].astype(F32), 0.0)
    base = CONV_HALO - CONV_K // 2
    span = sh_ref.shape[1]
    for r in range(1, SUBLANES):
        sh_ref[r - 1] = ext_ref[pl.ds(r, span), :]

    for cb in range(width // LANES):
        cols = pl.ds(cb * LANES, LANES)
        taps = [jnp.broadcast_to(w_ref[pl.ds(k, 1), cols], (SUBLANES, LANES)) for k in range(CONV_K)]
        bias = jnp.broadcast_to(cb_ref[:, cols], (SUBLANES, LANES))

        def rows(rb, carry, cols=cols, taps=taps, bias=bias):
            r0 = pl.multiple_of(rb * CONV_ROWS, CONV_ROWS)
            accs = [bias] * (CONV_ROWS // SUBLANES)
            for k in range(CONV_K):
                q, r = divmod(base + k, SUBLANES)
                src = ext_ref if r == 0 else sh_ref.at[r - 1]
                for j in range(len(accs)):
                    accs[j] = accs[j] + src[pl.ds(r0 + (q + j) * SUBLANES, SUBLANES), cols] * taps[k]
            for j, acc in enumerate(accs):
                acc_ref[pl.ds(r0 + j * SUBLANES, SUBLANES), cols] = acc
            return carry

        lax.fori_loop(0, tl // CONV_ROWS, rows, 0, unroll=2)
    x = acc_ref[...]
    mu = jnp.mean(x, axis=-1, keepdims=True)
    xc = x - mu
    var = jnp.mean(xc * xc, axis=-1, keepdims=True)
    y = xc * lax.rsqrt(var + EPS) * lg_ref[...] + lb_ref[...]
    o_ref[0] = (_silu(y) * sga_ref[0].astype(F32)).astype(BF16)


def conv_branch(a, sga, conv_w, conv_b, ln_g, ln_b, tl):
    bsz, length, width = a.shape
    hb = tl // CONV_HALO
    nh = length // CONV_HALO
    row = pl.BlockSpec((1, width), lambda b, i: (0, 0))
    main = pl.BlockSpec((1, tl, width), lambda b, i: (b, i, 0))
    return pl.pallas_call(
        _conv_kernel,
        name="conv_branch",
        grid=(bsz, length // tl),
        in_specs=[
            main,
            pl.BlockSpec((1, CONV_HALO, width), lambda b, i: (b, jnp.maximum(i * hb - 1, 0), 0)),
            pl.BlockSpec((1, CONV_HALO, width), lambda b, i: (b, jnp.minimum((i + 1) * hb, nh - 1), 0)),
            pl.BlockSpec((CONV_K, width), lambda b, i: (0, 0)),
            row, row, row,
            main,
        ],
        out_specs=main,
        out_shape=jax.ShapeDtypeStruct((bsz, length, width), BF16),
        scratch_shapes=[pltpu.VMEM((tl + 2 * CONV_HALO, width), F32),
                        pltpu.VMEM((SUBLANES - 1, tl + 2 * CONV_HALO - SUBLANES, width), F32),
                        pltpu.VMEM((tl, width), F32)],
        compiler_params=_params(("parallel", "parallel")),
    )(a, a, a, conv_w, conv_b.reshape(1, width), ln_g.reshape(1, width), ln_b.reshape(1, width), sga)


DFT_SPLIT = 64


def _dft_gen_kernel(ca_ref, sa_ref, cb_ref, sb_ref, c_ref, s_ref):
    ca, sa = ca_ref[0], sa_ref[0]
    cb, sb = cb_ref[...], sb_ref[...]
    c_ref[...] = (ca * cb - sa * sb).astype(BF16)
    s_ref[...] = (-(sa * cb + ca * sb)).astype(BF16)


def dft_matrices(length):
    na = length // DFT_SPLIT
    k = np.arange(length)[None, :]
    w = 2.0 * math.pi / length
    ang_a = ((DFT_SPLIT * np.arange(na)[:, None] * k) % length) * w
    ang_b = ((np.arange(DFT_SPLIT)[:, None] * k) % length) * w
    ca, sa = (jnp.asarray(f(ang_a).reshape(na, 1, length), F32) for f in (np.cos, np.sin))
    cb, sb = (jnp.asarray(f(ang_b), F32) for f in (np.cos, np.sin))
    tab_a = pl.BlockSpec((1, 1, length), lambda a: (a, 0, 0))
    tab_b = pl.BlockSpec((DFT_SPLIT, length), lambda a: (0, 0))
    out = pl.BlockSpec((DFT_SPLIT, length), lambda a: (a, 0))
    shp = jax.ShapeDtypeStruct((length, length), BF16)
    return pl.pallas_call(
        _dft_gen_kernel,
        name="dft_matrices",
        grid=(na,),
        in_specs=[tab_a, tab_a, tab_b, tab_b],
        out_specs=[out, out],
        out_shape=[shp, shp],
        compiler_params=_params(("parallel",)),
    )(ca, sa, cb, sb)


def _dft_kernel(c_ref, s_ref, p_ref, q_ref, sgb_ref, o_ref, acc_ref, *, scale):
    kk = pl.program_id(2)

    @pl.when(kk == 0)
    def _():
        acc_ref[...] = jnp.zeros_like(acc_ref)

    acc_ref[...] += _dot(c_ref[...], p_ref[0]) + _dot(s_ref[...], q_ref[0])

    @pl.when(kk == pl.num_programs(2) - 1)
    def _():
        o_ref[0] = (acc_ref[...] * scale * sgb_ref[0].astype(F32)).astype(BF16)


def fourier_branch(cmat, smat, p, q, sgb, scale, tm, tk):
    bsz, length, width = p.shape
    return pl.pallas_call(
        functools.partial(_dft_kernel, scale=scale),
        name="fourier_branch",
        grid=(bsz, length // tm, length // tk),
        in_specs=[
            pl.BlockSpec((tm, tk), lambda b, m, k: (m, k)),
            pl.BlockSpec((tm, tk), lambda b, m, k: (m, k)),
            pl.BlockSpec((1, tk, width), lambda b, m, k: (b, k, 0)),
            pl.BlockSpec((1, tk, width), lambda b, m, k: (b, k, 0)),
            pl.BlockSpec((1, tm, width), lambda b, m, k: (b, m, 0)),
        ],
        out_specs=pl.BlockSpec((1, tm, width), lambda b, m, k: (b, m, 0)),
        out_shape=jax.ShapeDtypeStruct((bsz, length, width), BF16),
        scratch_shapes=[pltpu.VMEM((tm, width), F32)],
        compiler_params=_params(("parallel", "parallel", "arbitrary")),
    )(cmat, smat, p, q, sgb)


FFT_MINOR = 64
FFT_COLS = 256


def _fft_kernel(p_ref, q_ref, k1a_ref, k1b_ref, twr_ref, twi_ref, hre_ref, him_ref, sgb_ref, o_ref,
                x_ref, y_ref, z_ref, *, scale):
    length, cw = p_ref.shape[1], p_ref.shape[2]
    ncb = cw // LANES
    n_m = length // FFT_MINOR
    n_a1 = FFT_MINOR // SUBLANES
    n_kb = n_m // SUBLANES
    slab = n_m * SUBLANES
    piece = SUBLANES * SUBLANES
    for cb in range(ncb):
        cols = pl.ds(cb * LANES, LANES)
        x_ref[cb] = p_ref[0, :, cols].astype(F32).reshape(n_m, FFT_MINOR, LANES)
        x_ref[ncb + cb] = q_ref[0, :, cols].astype(F32).reshape(n_m, FFT_MINOR, LANES)

    def slab_of(part, a1):
        tiles = [x_ref[part * ncb + cb, :, pl.ds(a1 * SUBLANES, SUBLANES), :].reshape(slab, LANES)
                 for cb in range(ncb)]
        return jnp.concatenate(tiles, axis=1).astype(BF16)

    for a1 in range(n_a1):
        y = _dot(k1a_ref[...], slab_of(0, a1)) + _dot(k1b_ref[...], slab_of(1, a1))
        tr, ti = twr_ref[a1], twi_ref[a1]
        for cb in range(ncb):
            yre = y[:slab, cb * LANES:(cb + 1) * LANES]
            yim = y[slab:, cb * LANES:(cb + 1) * LANES]
            zre = (tr * yre - ti * yim).astype(BF16)
            zim = (tr * yim + ti * yre).astype(BF16)
            for kb in range(n_kb):
                dst = (pl.ds(a1 * piece, piece), pl.ds(cb * LANES, LANES))
                y_ref[(kb, 0) + dst] = zre[kb * piece:(kb + 1) * piece]
                y_ref[(kb, 1) + dst] = zim[kb * piece:(kb + 1) * piece]

    for kb in range(n_kb):
        out = _dot(hre_ref[...], y_ref[kb, 0]) + _dot(him_ref[...], y_ref[kb, 1])
        z_ref[:, kb] = out.reshape(FFT_MINOR, SUBLANES, cw)

    o_ref[0] = (z_ref[...].reshape(length, cw) * scale * sgb_ref[0].astype(F32)).astype(BF16)


def _fft_tables(length):
    n_a, s = FFT_MINOR, SUBLANES
    n_m = length // n_a
    n_a1 = n_a // s
    eye = np.eye(s)
    ang_m = (np.outer(np.arange(n_m), np.arange(n_m)) % n_m) * (2.0 * math.pi / n_m)
    cm, sm = np.kron(np.cos(ang_m), eye), np.kron(np.sin(ang_m), eye)
    k1a = np.concatenate([cm, -sm], axis=0)
    k1b = np.concatenate([-sm, -cm], axis=0)
    a_full = s * np.arange(n_a1)[:, None, None] + np.arange(s)[None, None, :]
    ang_t = ((np.arange(n_m)[None, :, None] * a_full) % length) * (2.0 * math.pi / length)
    rep = lambda t: np.broadcast_to(t.reshape(n_a1, n_m * s, 1), (n_a1, n_m * s, LANES))
    twr, twi = rep(np.cos(ang_t)), rep(-np.sin(ang_t))
    ang_a = ((np.arange(n_a)[:, None, None] * a_full.reshape(1, n_a1, s)) % n_a) * (2.0 * math.pi / n_a)
    spread = lambda t: np.einsum("kas,pq->kpaqs", t, eye).reshape(n_a * s, n_a1 * s * s)
    hre, him = spread(np.cos(ang_a)), spread(np.sin(ang_a))
    return ([jnp.asarray(t, BF16) for t in (k1a, k1b)] + [jnp.asarray(t, F32) for t in (twr, twi)]
            + [jnp.asarray(t, BF16) for t in (hre, him)])


def fourier_branch_fft(p, q, sgb, scale):
    bsz, length, width = p.shape
    tables = _fft_tables(length)
    n_m = length // FFT_MINOR
    tok = pl.BlockSpec((1, length, FFT_COLS), lambda b, n: (b, 0, n))
    const = lambda t: pl.BlockSpec(t.shape, lambda b, n, nd=t.ndim: (0,) * nd)
    return pl.pallas_call(
        functools.partial(_fft_kernel, scale=scale),
        name="fourier_branch_fft",
        grid=(bsz, width // FFT_COLS),
        in_specs=[tok, tok] + [const(t) for t in tables] + [tok],
        out_specs=tok,
        out_shape=jax.ShapeDtypeStruct((bsz, length, width), BF16),
        scratch_shapes=[pltpu.VMEM((2 * FFT_COLS // LANES, n_m, FFT_MINOR, LANES), F32),
                        pltpu.VMEM((n_m // SUBLANES, 2, FFT_MINOR * SUBLANES, FFT_COLS), BF16),
                        pltpu.VMEM((FFT_MINOR, n_m // SUBLANES, SUBLANES, FFT_COLS), F32)],
        compiler_params=_params(("parallel", "parallel")),
    )(p, q, *tables, sgb)


def _postnorm_residual(y, h_ref, pg_ref, gt_ref, o_ref):
    o_ref[0] = h_ref[0] + gt_ref[0] * (_rms(y) * pg_ref[...])


def _even_out_kernel(a_ref, b_ref, wa_ref, wb_ref, h_ref, pg_ref, gt_ref, o_ref):
    y = _dot(a_ref[0], wa_ref[...]) + _dot(b_ref[0], wb_ref[...])
    _postnorm_residual(y, h_ref, pg_ref, gt_ref, o_ref)


def even_out_proj(a, b, wa, wb, h, post_g, gate, tm):
    bsz, length, d = h.shape
    width = a.shape[2]
    half = pl.BlockSpec((1, tm, width), lambda bb, i: (bb, i, 0))
    wspec = pl.BlockSpec((width, d), lambda bb, i: (0, 0))
    full = pl.BlockSpec((1, tm, d), lambda bb, i: (bb, i, 0))
    return pl.pallas_call(
        _even_out_kernel,
        name="even_out_proj",
        grid=(bsz, length // tm),
        in_specs=[half, half, wspec, wspec, full,
                  pl.BlockSpec((1, d), lambda bb, i: (0, 0)),
                  pl.BlockSpec((1, 1, d), lambda bb, i: (bb, 0, 0))],
        out_specs=full,
        out_shape=jax.ShapeDtypeStruct((bsz, length, d), F32),
        compiler_params=_params(("parallel", "parallel")),
    )(a, b, wa, wb, h, post_g.reshape(1, d), gate)


ODD_TN = 1024


ODD_PARTS = 2


def _odd_in_kernel(h_ref, g_ref, sh_ref, sc_ref, w_ref, o_ref, xn_ref, *, gate_parts, scaled_parts, scale):
    n = pl.program_id(2)

    @pl.when(n == 0)
    def _():
        _prenorm_modulate(h_ref, g_ref, sh_ref, sc_ref, xn_ref)

    def in_tiles(parts, j):
        tiles = [p // ODD_PARTS for p in parts if p % ODD_PARTS == j]
        return functools.reduce(jnp.logical_or, [n == t for t in tiles]) if tiles else None

    xn = xn_ref[...]
    pw = ODD_TN // ODD_PARTS
    for j in range(ODD_PARTS):
        cols = pl.ds(j * pw, pw)
        acc = _dot(xn, w_ref[:, cols].astype(BF16))
        is_scaled, is_gate = in_tiles(scaled_parts, j), in_tiles(gate_parts, j)
        if is_scaled is not None:
            acc = acc * jnp.where(is_scaled, scale, 1.0)
        if is_gate is not None:
            acc = jnp.where(is_gate, _silu(acc), acc)
        o_ref[0, :, cols] = acc.astype(BF16)


def odd_in_proj(h, pre_g, shift, scale, w_all, gate_ranges, scaled_range, out_scale, tm):
    bsz, length, d = h.shape
    n_all = w_all.shape[1]
    pw = ODD_TN // ODD_PARTS
    ranges = tuple(gate_ranges) + (scaled_range,)
    assert n_all % ODD_TN == 0 and all(s % pw == 0 and e % pw == 0 for s, e in ranges)
    parts = lambda rs: tuple(p for s, e in rs for p in range(s // pw, e // pw))
    vec = pl.BlockSpec((1, 1, d), lambda b, i, n: (b, 0, 0))
    return pl.pallas_call(
        functools.partial(_odd_in_kernel, gate_parts=parts(gate_ranges), scaled_parts=parts((scaled_range,)),
                          scale=out_scale),
        name="odd_in_proj",
        grid=(bsz, length // tm, n_all // ODD_TN),
        in_specs=[pl.BlockSpec((1, tm, d), lambda b, i, n: (b, i, 0)),
                  pl.BlockSpec((1, d), lambda b, i, n: (0, 0)),
                  vec, vec,
                  pl.BlockSpec((d, ODD_TN), lambda b, i, n: (0, n))],
        out_specs=pl.BlockSpec((1, tm, ODD_TN), lambda b, i, n: (b, i, n)),
        out_shape=jax.ShapeDtypeStruct((bsz, length, n_all), BF16),
        scratch_shapes=[pltpu.VMEM((tm, d), BF16)],
        compiler_params=_params(("parallel", "parallel", "arbitrary")),
    )(h, pre_g.reshape(1, d), shift, scale, w_all)


def _na_pair_geometry(variant, rows):
    r0 = {0: NA_QROWS, 1: 0, 2: rows - NA_QROWS}[variant]
    ws = min(max(r0 - NA_ROWS // 2, 0), rows - NA_WIN_ROWS)
    return r0, ws


def _na_bias_kernel(rpb_ref, o_ref, *, rows):
    h = pl.program_id(0)
    n_dr, n_dc = 2 * NA_ROWS - 1, 2 * NA_COLS - 1
    qc = lax.broadcasted_iota(jnp.int32, (GRID_W, LANES), 0)
    lane = lax.broadcasted_iota(jnp.int32, (GRID_W, LANES), 1)
    kc = lane % GRID_W
    hi = lane // GRID_W
    diff = kc - qc
    c_start = jnp.clip(qc - NA_COLS // 2, 0, GRID_W - NA_COLS)
    col_ok = jnp.where(kc >= c_start, 1, 0) * jnp.where(kc < c_start + NA_COLS, 1, 0)
    blocks = []
    for dr in range(n_dr):
        val = jnp.full((GRID_W, LANES), NEG_INF, F32)
        for dc in range(n_dc):
            val = jnp.where(diff == dc - (NA_COLS - 1), rpb_ref[(h * n_dr + dr) * n_dc + dc], val)
        blocks.append(jnp.where(col_ok == 1, val, NEG_INF))
    masked = jnp.full((GRID_W, LANES), NEG_INF, F32)
    for variant in range(NA_VARIANTS):
        r0, ws = _na_pair_geometry(variant, rows)
        for ri in range(NA_QROWS):
            r = r0 + ri
            r_start = min(max(r - NA_ROWS // 2, 0), rows - NA_ROWS)
            for wp in range(NA_WIN_ROWS // 2):
                halves = []
                for a in (ws + 2 * wp, ws + 2 * wp + 1):
                    in_window = r_start <= a < r_start + NA_ROWS
                    halves.append(blocks[a - r + NA_ROWS - 1] if in_window else masked)
                o_ref[0, variant, pl.ds(ri * GRID_W, GRID_W), pl.ds(wp * LANES, LANES)] = jnp.where(
                    hi == 1, halves[1], halves[0])


def na_bias_table(rpb, rows):
    heads = rpb.shape[0]
    return pl.pallas_call(
        functools.partial(_na_bias_kernel, rows=rows),
        name="na_bias_table",
        grid=(heads,),
        in_specs=[pl.BlockSpec(memory_space=pltpu.SMEM)],
        out_specs=pl.BlockSpec((1, NA_VARIANTS, NA_PAIR, NA_WIN), lambda h: (h, 0, 0, 0)),
        out_shape=jax.ShapeDtypeStruct((heads, NA_VARIANTS, NA_PAIR, NA_WIN), F32),
        compiler_params=_params(("parallel",)),
    )(rpb.reshape(-1))


def _na_kernel(q_ref, k_ref, v_ref, kc_ref, vc_ref, bias_ref, sg_ref, o_ref,
               vx_ref, vcx_ref, s0_ref, s1_ref, p0_ref, p1_ref, *, rows):
    npairs = rows // NA_QROWS
    last = npairs - 1
    n_heads = q_ref.shape[2] // HEAD_DIM
    nt = (((1,), (1,)), ((), ()))
    head = lambda hh: pl.ds(hh * HEAD_DIM, HEAD_DIM)
    for hh in range(n_heads):
        vx_ref[hh, :, :HEAD_DIM] = v_ref[0, :, head(hh)]
        vx_ref[hh, :, HEAD_DIM:] = jnp.ones((vx_ref.shape[1], HEAD_DIM), BF16)
        vcx_ref[hh, :, :HEAD_DIM] = vc_ref[0, :, head(hh)]
        vcx_ref[hh, :, HEAD_DIM:] = jnp.ones((vcx_ref.shape[1], HEAD_DIM), BF16)

    def window(pr):
        ws = min(max(NA_QROWS * pr - NA_ROWS // 2, 0), rows - NA_WIN_ROWS)
        return pl.ds(ws * GRID_W, NA_WIN)

    def qrows(pr):
        return pl.ds(pr * NA_PAIR, NA_PAIR)

    def scores(unit, s_ref):
        hh, pr = divmod(unit, npairs)
        variant = {0: 1, last: 2}.get(pr, 0)
        q = q_ref[0, qrows(pr), head(hh)]
        s_ref[:, :NA_WIN] = lax.dot_general(q, k_ref[0, window(pr), head(hh)], nt,
                                            preferred_element_type=F32) + bias_ref[hh, variant]
        s_ref[:, NA_WIN:] = lax.dot_general(q, kc_ref[0, :, head(hh)], nt, preferred_element_type=F32)

    def probs(s_ref, p_ref):
        s = s_ref[...]
        p_ref[...] = jnp.exp(s - jnp.max(s, axis=-1, keepdims=True)).astype(BF16)

    def values(unit, p_ref):
        hh, pr = divmod(unit, npairs)
        acc = _dot(p_ref[:, :NA_WIN], vx_ref[hh, window(pr), :]) + _dot(p_ref[:, NA_WIN:], vcx_ref[hh])
        o = acc[:, :HEAD_DIM] / acc[:, HEAD_DIM:]
        o_ref[0, qrows(pr), head(hh)] = (o * sg_ref[0, qrows(pr), head(hh)].astype(F32)).astype(BF16)

    units = n_heads * npairs
    bufs = ((s0_ref, p0_ref), (s1_ref, p1_ref))
    scores(0, s0_ref)
    scores(1, s1_ref)
    probs(s0_ref, p0_ref)
    for i in range(units):
        s_cur, p_cur = bufs[i % 2]
        s_nxt, p_nxt = bufs[(i + 1) % 2]
        values(i, p_cur)
        if i + 1 < units:
            probs(s_nxt, p_nxt)
        if i + 2 < units:
            scores(i + 2, s_cur)


def neighbourhood_attention(proj, proj_c, bias, na_w, gate_col):
    bsz, length, _ = proj.shape
    lc = proj_c.shape[1]
    heads = na_w // HEAD_DIM
    rows = length // GRID_W
    nh = NA_HEADS_PER_STEP
    assert heads % nh == 0 and gate_col % (nh * HEAD_DIM) == 0
    groups = heads // nh
    seq = lambda off: pl.BlockSpec((1, length, nh * HEAD_DIM), lambda h, b: (b, 0, off + h))
    cseq = lambda off: pl.BlockSpec((1, lc, nh * HEAD_DIM), lambda h, b: (b, 0, off + h))
    return pl.pallas_call(
        functools.partial(_na_kernel, rows=rows),
        name="neighbourhood_attention",
        grid=(groups, bsz),
        in_specs=[seq(0), seq(groups), seq(2 * groups), cseq(groups), cseq(2 * groups),
                  pl.BlockSpec((nh, NA_VARIANTS, NA_PAIR, NA_WIN), lambda h, b: (h, 0, 0, 0)),
                  seq(gate_col // (nh * HEAD_DIM))],
        out_specs=seq(0),
        out_shape=jax.ShapeDtypeStruct((bsz, length, na_w), BF16),
        scratch_shapes=[pltpu.VMEM((nh, length, 2 * HEAD_DIM), BF16), pltpu.VMEM((nh, lc, 2 * HEAD_DIM), BF16),
                        pltpu.VMEM((NA_PAIR, NA_WIN + lc), F32), pltpu.VMEM((NA_PAIR, NA_WIN + lc), F32),
                        pltpu.VMEM((NA_PAIR, NA_WIN + lc), BF16), pltpu.VMEM((NA_PAIR, NA_WIN + lc), BF16)],
        compiler_params=_params(("parallel", "parallel")),
    )(proj, proj, proj, proj_c, proj_c, bias, proj)


S5_CHUNK = 128
S5_IN_BLOCK = LANES
S5_OUT_BLOCK = 256
S5_SCAN_COLS = 1024


def _s5_kernel(ufc_ref, ufx_ref, ubc_ref, ubx_ref, bf_ref, bb_ref, crf_ref, cif_ref, crb_ref, cib_ref,
               lre_ref, lim_ref, yf_ref, yb_ref, buf_ref, state_ref, stage_ref, *, ctx_chunks):
    bsz, t_len, width = ufx_ref.shape
    n_state = lre_ref.shape[1]
    in_ctx = pl.program_id(0) < ctx_chunks
    n_in = width // S5_IN_BLOCK
    blk = n_state // n_in
    rows8 = 2 * bsz

    @pl.when(pl.program_id(0) == 0)
    def _():
        state_ref[...] = jnp.zeros_like(state_ref)

    nlb = n_state // LANES
    per_in = blk // LANES
    ti = lax.broadcasted_iota(jnp.int32, (t_len, t_len), 0)
    tj = lax.broadcasted_iota(jnp.int32, (t_len, t_len), 1)
    rev = jnp.where(ti + tj == t_len - 1, 1.0, 0.0).astype(BF16)
    u_fwd = jnp.where(in_ctx, ufc_ref[...], ufx_ref[...]).reshape(bsz * t_len, width)
    u_nat = jnp.where(in_ctx, ubc_ref[...], ubx_ref[...])
    u_bwd = jnp.concatenate([_dot(rev, u_nat[b]).astype(BF16) for b in range(bsz)], axis=0)
    sides = ((u_fwd, bf_ref, yf_ref, crf_ref, cif_ref), (u_bwd, bb_ref, yb_ref, crb_ref, cib_ref))

    def project(kb):
        for di, (u, w_ref, _, _, _) in enumerate(sides):
            res = _dot(u[:, kb * S5_IN_BLOCK:(kb + 1) * S5_IN_BLOCK], w_ref[kb])
            for b in range(bsz):
                rsel = pl.ds(di * bsz + b, t_len, stride=rows8)
                for c in range(2 * per_in):
                    dst = (c // per_in) * nlb + kb * per_in + c % per_in
                    buf_ref[dst, rsel, :] = res[b * t_len:(b + 1) * t_len, c * LANES:(c + 1) * LANES]

    per = S5_SCAN_COLS // LANES

    def scan(group, t0, t1, carry):
        blocks = [group * per + i for i in range(per)]
        lre = [lre_ref[:, pl.ds(c * LANES, LANES)] for c in blocks]
        lim = [lim_ref[:, pl.ds(c * LANES, LANES)] for c in blocks]
        if carry is None:
            carry = [state_ref[:, pl.ds(part * n_state + c * LANES, LANES)] for c in blocks for part in range(2)]
        for t in range(t0, t1):
            row = pl.ds(t * rows8, rows8)
            for i, c in enumerate(blocks):
                hre, him = carry[2 * i], carry[2 * i + 1]
                carry[2 * i] = lre[i] * hre - lim[i] * him + buf_ref[c, row, :]
                carry[2 * i + 1] = lre[i] * him + lim[i] * hre + buf_ref[nlb + c, row, :]
                buf_ref[c, row, :] = carry[2 * i]
                buf_ref[nlb + c, row, :] = carry[2 * i + 1]
        if t1 == t_len:
            for i, c in enumerate(blocks):
                state_ref[:, pl.ds(c * LANES, LANES)] = carry[2 * i]
                state_ref[:, pl.ds(n_state + c * LANES, LANES)] = carry[2 * i + 1]
        return carry

    n_out = width // S5_OUT_BLOCK
    per_out = nlb // n_out

    def readout(di, j):
        _, _, y_ref, cr_ref, ci_ref = sides[di]
        stage = stage_ref.at[di * n_out + j]
        for part in range(2):
            for b in range(bsz):
                rsel = pl.ds(di * bsz + b, t_len, stride=rows8)
                for c in range(per_out):
                    stage[part, pl.ds(b * t_len, t_len), pl.ds(c * LANES, LANES)] = buf_ref[
                        part * nlb + j * per_out + c, rsel, :].astype(BF16)
        y = _dot(stage[0], cr_ref[j]) + _dot(stage[1], ci_ref[j])
        for b in range(bsz):
            yb = y[b * t_len:(b + 1) * t_len]
            if di == 1:
                hi = yb.astype(BF16)
                r1 = yb - hi.astype(F32)
                mid = r1.astype(BF16)
                lo = (r1 - mid.astype(F32)).astype(BF16)
                yb = _dot(rev, hi) + _dot(rev, mid) + _dot(rev, lo)
            y_ref[b, :, pl.ds(j * S5_OUT_BLOCK, S5_OUT_BLOCK)] = yb

    assert n_in == 4 and nlb // per == 2 and n_out == 2 and per_out == per
    half = t_len // 2
    project(0)
    project(1)
    carry = scan(0, 0, half, None)
    project(2)
    scan(0, half, t_len, carry)
    project(3)
    carry = scan(1, 0, half, None)
    readout(0, 0)
    scan(1, half, t_len, carry)
    readout(1, 0)
    readout(0, 1)
    readout(1, 1)


def _block_diag(x):
    nblk, g, r, c = x.shape
    eye = jnp.eye(g, dtype=x.dtype)
    return jnp.einsum("kgrc,gh->kgrhc", x, eye).reshape(nblk, g * r, g * c)


def _s5_direction_params(a_re, a_im, log_dt, b_re, b_im, c_re, c_im):
    groups, n_p = a_re.shape
    lam = lax.complex(a_re.astype(F32), a_im.astype(F32))
    dt = jnp.exp(log_dt.astype(F32))[:, None]
    lam_bar = jnp.exp(lam * dt)
    b_bar = ((lam_bar - 1.0) / lam)[..., None] * lax.complex(b_re.astype(F32), b_im.astype(F32))
    gi = S5_IN_BLOCK // SSM_GROUP
    bt = jnp.swapaxes(b_bar, 1, 2).reshape(groups // gi, gi, SSM_GROUP, n_p)
    b_mat = jnp.concatenate([_block_diag(jnp.real(bt)), _block_diag(jnp.imag(bt))], axis=-1)
    go = S5_OUT_BLOCK // SSM_GROUP
    ct = lambda c: jnp.swapaxes(c.astype(F32), 1, 2).reshape(groups // go, go, n_p, SSM_GROUP)
    return (b_mat.astype(BF16), _block_diag(ct(c_re)).astype(BF16), _block_diag(-ct(c_im)).astype(BF16),
            jnp.real(lam_bar).reshape(-1), jnp.imag(lam_bar).reshape(-1))


def s5_scan(proj_c, proj_x, d_col, width, fwd, bwd):
    bsz, lc, _ = proj_c.shape
    length = proj_x.shape[1]
    t_len = S5_CHUNK
    ncc, ncx = lc // t_len, length // t_len
    nc = ncc + ncx
    col = d_col // width
    n_state = fwd[3].shape[0]
    lre = jnp.concatenate([jnp.broadcast_to(fwd[3], (bsz, n_state)), jnp.broadcast_to(bwd[3], (bsz, n_state))])
    lim = jnp.concatenate([jnp.broadcast_to(fwd[4], (bsz, n_state)), jnp.broadcast_to(bwd[4], (bsz, n_state))])
    full = lambda x: pl.BlockSpec(x.shape, lambda c, nd=x.ndim: (0,) * nd)
    blk = lambda index: pl.BlockSpec((bsz, t_len, width), index)
    in_blocks = [blk(lambda c: (0, jnp.minimum(c, ncc - 1), col)),
                 blk(lambda c: (0, jnp.maximum(c - ncc, 0), col)),
                 blk(lambda c: (0, jnp.maximum(ncc - 1 - c, 0), col)),
                 blk(lambda c: (0, jnp.minimum(nc - 1 - c, ncx - 1), col))]
    out = jax.ShapeDtypeStruct((bsz, length, width), F32)
    consts = (fwd[0], bwd[0], fwd[1], fwd[2], bwd[1], bwd[2], lre, lim)
    return pl.pallas_call(
        functools.partial(_s5_kernel, ctx_chunks=ncc),
        name="s5_scan",
        grid=(nc,),
        in_specs=in_blocks + [full(x) for x in consts],
        out_specs=[blk(lambda c: (0, jnp.maximum(c - ncc, 0), 0)),
                   blk(lambda c: (0, jnp.minimum(nc - 1 - c, ncx - 1), 0))],
        out_shape=[out, out],
        scratch_shapes=[pltpu.VMEM((2 * n_state // LANES, t_len * 2 * bsz, LANES), F32),
                        pltpu.VMEM((2 * bsz, 2 * n_state), F32),
                        pltpu.VMEM((2 * (width // S5_OUT_BLOCK), 2, bsz * t_len,
                                    n_state // (width // S5_OUT_BLOCK)), BF16)],
        compiler_params=_params(("arbitrary",)),
    )(proj_c, proj_x, proj_c, proj_x, *consts)


def _odd_out_kernel(na_ref, yf_ref, yb_ref, d_ref, sdg_ref, dsk_ref, wglu_ref, wna_ref, wssm_ref,
                    h_ref, pg_ref, gt_ref, o_ref):
    y = yf_ref[0] + yb_ref[0] + dsk_ref[...] * d_ref[0].astype(F32)
    y = 0.5 * y * (1.0 + jnp.tanh(math.sqrt(2.0 / math.pi) * (y + 0.044715 * (y * y * y))))
    z = y * jax.nn.sigmoid(_dot(y.astype(BF16), wglu_ref[...]))
    s = (z * sdg_ref[0].astype(F32)).astype(BF16)
    out = _dot(na_ref[0], wna_ref[...]) + _dot(s, wssm_ref[...])
    _postnorm_residual(out, h_ref, pg_ref, gt_ref, o_ref)


def odd_out_proj(na, yf, yb, proj, d_col, dg_col, d_skip, w_glu, w_na, w_ssm, h, post_g, gate, tm):
    bsz, length, dm = h.shape
    na_w, ssm_w = na.shape[2], yf.shape[2]
    tok = lambda w, col=0: pl.BlockSpec((1, tm, w), lambda b, i: (b, i, col // w))
    const = lambda r, c: pl.BlockSpec((r, c), lambda b, i: (0, 0))
    return pl.pallas_call(
        _odd_out_kernel,
        name="odd_out_proj",
        grid=(bsz, length // tm),
        in_specs=[tok(na_w), tok(ssm_w), tok(ssm_w),
                  tok(ssm_w, d_col), tok(ssm_w, dg_col),
                  const(1, ssm_w), const(ssm_w, ssm_w), const(na_w, dm), const(ssm_w, dm),
                  tok(dm), const(1, dm),
                  pl.BlockSpec((1, 1, dm), lambda b, i: (b, 0, 0))],
        out_specs=tok(dm),
        out_shape=jax.ShapeDtypeStruct((bsz, length, dm), F32),
        compiler_params=_params(("parallel", "parallel")),
    )(na, yf, yb, proj, proj, d_skip.reshape(1, ssm_w), w_glu, w_na, w_ssm, h, post_g.reshape(1, dm), gate)


def _small_dft(n):
    ang = (np.outer(np.arange(n), np.arange(n)) % n) * (2.0 * math.pi / n)
    return jnp.asarray(np.concatenate([np.cos(ang), np.sin(ang)], axis=1), BF16)


def _conv_fourier_layer(h, pre_g, post_g, shift, scale, gate, w5, w_out_a, w_out_b, conv_w, conv_b,
                        ln_g, ln_b, fourier_g, cs_small, shared_mod):
    bsz, length, _ = h.shape
    gd = fourier_g.shape[1]
    fold = (lambda t: t.reshape(1, bsz * length, t.shape[2])) if shared_mod else (lambda t: t)
    unfold = (lambda t: t.reshape(bsz, length, t.shape[2])) if shared_mod else (lambda t: t)
    vec = (lambda v: v[:1]) if shared_mod else (lambda v: v)
    tokens = bsz * length if shared_mod else length
    outs = even_in_proj(fold(h), pre_g, vec(shift), vec(scale), w5, fourier_g, cs_small,
                        _token_tile(tokens, IN_PROJ_ROWS))
    a, sga, p, q, sgb = [unfold(t) for t in outs]
    a = conv_branch(a, sga, conv_w, conv_b, ln_g, ln_b, _token_tile(length, CONV_TILE_ROWS))
    ortho = 1.0 / math.sqrt(length * gd)
    if (length // FFT_MINOR) % SUBLANES == 0:
        f = fourier_branch_fft(p, q, sgb, ortho)
    else:
        cmat, smat = dft_matrices(length)
        f = fourier_branch(cmat, smat, p, q, sgb, ortho, length, length)
    return unfold(even_out_proj(fold(a), fold(f), w_out_a, w_out_b, fold(h), post_g, vec(gate),
                                _token_tile(tokens, OUT_PROJ_ROWS)))


def kernel(x, c, ctx, c_ctx, pre_g, post_g, ada_w, ada_b, ab_w_in, ab_w_out, conv_w, conv_b, conv_ln_g,
           conv_ln_b, fourier_g, cd_w_in, cd_w_out, na_rpb, s5_a_re, s5_a_im, s5_log_dt, s5_b_re, s5_b_im,
           s5_c_re, s5_c_im, s5_d, s5_w_glu):
    bsz, length, d = x.shape
    lc = ctx.shape[1]
    depth = ada_w.shape[0]
    assert depth == 2 and length % NA_PAIR == 0 and length // GRID_W >= NA_WIN_ROWS
    assert lc % S5_CHUNK == 0 and length % S5_CHUNK == 0

    cond_rows = jnp.zeros((SUBLANES, d), F32).at[:bsz].set(c).at[bsz].set(c_ctx)
    mods = ada_modulation(cond_rows, ada_w, ada_b)

    def mod_vectors(i):
        xs = [mods[i, :bsz, k * d:(k + 1) * d][:, None, :] for k in range(3)]
        cs = [jnp.broadcast_to(mods[i, bsz, k * d:(k + 1) * d], (bsz, 1, d)) for k in range(3)]
        return xs, cs

    (sh_x, sc_x, gt_x), (sh_c, sc_c, gt_c) = mod_vectors(0)
    w_in = ab_w_in[0]
    conv_width = conv_w.shape[2]
    cw = conv_width
    fw = (w_in.shape[1] - 3 * cw) // 2
    assert cw == fw
    w5 = w_in.astype(BF16)
    w_out_a = ab_w_out[0, :cw].astype(BF16)
    w_out_b = ab_w_out[0, cw:].astype(BF16)
    cs_small = _small_dft(fourier_g.shape[2])
    layer0 = functools.partial(
        _conv_fourier_layer, pre_g=pre_g[0], post_g=post_g[0], w5=w5, w_out_a=w_out_a, w_out_b=w_out_b,
        conv_w=conv_w[0], conv_b=conv_b[0], ln_g=conv_ln_g[0], ln_b=conv_ln_b[0], fourier_g=fourier_g[0],
        cs_small=cs_small)
    h_x = layer0(x, shift=sh_x, scale=sc_x, gate=gt_x, shared_mod=False)
    h_c = layer0(ctx, shift=sh_c, scale=sc_c, gate=gt_c, shared_mod=True)

    (sh_x, sc_x, gt_x), (sh_c, sc_c, _) = mod_vectors(1)
    w_in = cd_w_in[0]
    ssm_w = s5_d.shape[1]
    na_w = (w_in.shape[1] - 2 * ssm_w) // 4
    g_col, d_col, dg_col = 3 * na_w, 4 * na_w, 4 * na_w + ssm_w
    gates = ((g_col, d_col), (dg_col, dg_col + ssm_w))
    q_scaled = ((0, na_w), HEAD_DIM ** -0.5)
    proj_x = odd_in_proj(h_x, pre_g[1], sh_x, sc_x, w_in, gates, *q_scaled, _token_tile(length, IN_PROJ_ROWS))
    proj_c = odd_in_proj(h_c.reshape(1, bsz * lc, d), pre_g[1], sh_c[:1], sc_c[:1], w_in, gates, *q_scaled,
                         _token_tile(bsz * lc, IN_PROJ_ROWS)).reshape(bsz, lc, -1)

    bias = na_bias_table(na_rpb[0], length // GRID_W)
    na = neighbourhood_attention(proj_x, proj_c, bias, na_w, g_col)

    dirs = [_s5_direction_params(s5_a_re[0, i], s5_a_im[0, i], s5_log_dt[0, i], s5_b_re[0, i], s5_b_im[0, i],
                                 s5_c_re[0, i], s5_c_im[0, i]) for i in range(2)]
    y_f, y_b = s5_scan(proj_c, proj_x, d_col, ssm_w, *dirs)

    return odd_out_proj(na, y_f, y_b, proj_x, d_col, dg_col, s5_d[0], s5_w_glu[0].astype(BF16),
                        cd_w_out[0, :na_w].astype(BF16), cd_w_out[0, na_w:].astype(BF16),
                        h_x, post_g[1], gt_x, _token_tile(length, OUT_PROJ_ROWS))
```

```python
import functools
import math

import jax
import jax.numpy as jnp
import numpy as np
from jax import lax
from jax.experimental import pallas as pl
from jax.experimental.pallas import tpu as pltpu

F32 = jnp.float32
BF16 = jnp.bfloat16

EPS = 1e-6
NEG_INF = -1e30

GRID_W = 64
CONV_K = 31
FOURIER_GROUPS = 4
HEAD_DIM = 128
NA_ROWS = 8
NA_COLS = 16
SSM_GROUP = 16
SSM_STATE = 64

LANES = 128
SUBLANES = 8
VMEM_LIMIT = 56 * 1024 * 1024

IN_PROJ_ROWS = 1024
OUT_PROJ_ROWS = 512
CONV_TILE_ROWS = 256
ADA_COLS = 1024


def _token_tile(n_tokens, cap):
    tile = min(cap, n_tokens)
    assert n_tokens % tile == 0
    return tile

NA_QROWS = 4
NA_PAIR = NA_QROWS * GRID_W
NA_WIN_ROWS = NA_ROWS + NA_QROWS
NA_WIN = NA_WIN_ROWS * GRID_W
NA_VARIANTS = 3
NA_HEADS_PER_STEP = 2


def _params(sem, vmem=VMEM_LIMIT):
    return pltpu.CompilerParams(dimension_semantics=sem, vmem_limit_bytes=vmem)


def _silu(x):
    return x * jax.nn.sigmoid(x)


def _rms(x):
    return x * lax.rsqrt(jnp.mean(x * x, axis=-1, keepdims=True) + EPS)


def _dot(a, b):
    return jnp.dot(a, b, preferred_element_type=F32)


def _ada_kernel(c_ref, w_ref, b_ref, o_ref):
    cond = _silu(c_ref[...])
    o_ref[0] = _dot(cond.astype(BF16), w_ref[0].astype(BF16)) + b_ref[0]


def ada_modulation(cond_rows, ada_w, ada_b):
    depth, d, n = ada_w.shape
    tn = _token_tile(n, ADA_COLS)
    rows = cond_rows.shape[0]
    return pl.pallas_call(
        _ada_kernel,
        name="ada_modulation",
        grid=(depth, n // tn),
        in_specs=[
            pl.BlockSpec((rows, d), lambda i, j: (0, 0)),
            pl.BlockSpec((1, d, tn), lambda i, j: (i, 0, j)),
            pl.BlockSpec((1, 1, tn), lambda i, j: (i, 0, j)),
        ],
        out_specs=pl.BlockSpec((1, rows, tn), lambda i, j: (i, 0, j)),
        out_shape=jax.ShapeDtypeStruct((depth, rows, n), F32),
        compiler_params=_params(("parallel", "parallel")),
    )(cond_rows, ada_w, ada_b.reshape(depth, 1, n))


PRENORM_ROWS = 16


def _prenorm_modulate(h_ref, g_ref, sh_ref, sc_ref, xn_ref):
    gain = g_ref[...] * (1.0 + sc_ref[0])
    shift = sh_ref[0]

    def chunk(r, carry):
        rows = pl.ds(pl.multiple_of(r * PRENORM_ROWS, PRENORM_ROWS), PRENORM_ROWS)
        xn_ref[rows, :] = (_rms(h_ref[0, rows, :]) * gain + shift).astype(BF16)
        return carry

    lax.fori_loop(0, xn_ref.shape[0] // PRENORM_ROWS, chunk, 0, unroll=4)


EVEN_PARTS = 5


def _even_in_kernel(h_ref, g_ref, sh_ref, sc_ref, w0_ref, w1_ref, w2_ref, w3_ref, w4_ref, fg_ref, cs_ref,
                    a_ref, sga_ref, p_ref, q_ref, sgb_ref, xn_ref):
    @pl.when(pl.program_id(2) == 0)
    def _():
        _prenorm_modulate(h_ref, g_ref, sh_ref, sc_ref, xn_ref)

    gd = a_ref.shape[2]
    w = jnp.concatenate([w0_ref[...], w1_ref[...], w2_ref[...], w3_ref[...], w4_ref[...]], axis=1)
    acc = _dot(xn_ref[...], w)
    part = lambda k: acc[:, k * gd:(k + 1) * gd]
    a_ref[0] = (part(0) * jax.nn.sigmoid(part(1))).astype(BF16)
    sga_ref[0] = _silu(part(2)).astype(BF16)
    bn = _rms(part(3)) * fg_ref[0]
    pq = _dot(bn.astype(BF16), cs_ref[...])
    p_ref[0] = pq[:, :gd].astype(BF16)
    q_ref[0] = pq[:, gd:].astype(BF16)
    sgb_ref[0] = _silu(part(4)).astype(BF16)


def even_in_proj(h, pre_g, shift, scale, w_in, fourier_g, cs_small, tm):
    bsz, length, d = h.shape
    gd = fourier_g.shape[1]
    width = FOURIER_GROUPS * gd
    wspecs = [pl.BlockSpec((d, gd), lambda b, i, n, k=k: (0, k * FOURIER_GROUPS + n)) for k in range(EVEN_PARTS)]
    ospec = pl.BlockSpec((1, tm, gd), lambda b, i, n: (b, i, n))
    vec = pl.BlockSpec((1, 1, d), lambda b, i, n: (b, 0, 0))
    out = jax.ShapeDtypeStruct((bsz, length, width), BF16)
    return pl.pallas_call(
        _even_in_kernel,
        name="even_in_proj",
        grid=(bsz, length // tm, FOURIER_GROUPS),
        in_specs=[
            pl.BlockSpec((1, tm, d), lambda b, i, n: (b, i, 0)),
            pl.BlockSpec((1, d), lambda b, i, n: (0, 0)),
            vec, vec,
            *wspecs,
            pl.BlockSpec((1, 1, gd), lambda b, i, n: (n, 0, 0)),
            pl.BlockSpec((gd, 2 * gd), lambda b, i, n: (0, 0)),
        ],
        out_specs=[ospec] * 5,
        out_shape=[out] * 5,
        scratch_shapes=[pltpu.VMEM((tm, d), BF16)],
        compiler_params=_params(("parallel", "parallel", "arbitrary")),
    )(h, pre_g.reshape(1, d), shift, scale, *([w_in] * EVEN_PARTS),
      fourier_g.reshape(FOURIER_GROUPS, 1, gd), cs_small)


CONV_HALO = 16
CONV_ROWS = 32


def _conv_kernel(a_ref, prev_ref, next_ref, w_ref, cb_ref, lg_ref, lb_ref, sga_ref, o_ref,
                 ext_ref, sh_ref, acc_ref):
    i = pl.program_id(1)
    last = pl.num_programs(1) - 1
    tl, width = acc_ref.shape
    ext_ref[pl.ds(CONV_HALO, tl), :] = a_ref[0].astype(F32)
    ext_ref[pl.ds(0, CONV_HALO), :] = jnp.where(i > 0, prev_ref[0].astype(F32), 0.0)
    ext_ref[pl.ds(CONV_HALO + tl, CONV_HALO), :] = jnp.where(i < last, next_ref[0].astype(F32), 0.0)
    base = CONV_HALO - CONV_K // 2
    span = sh_ref.shape[1]
    for r in range(1, SUBLANES):
        sh_ref[r - 1] = ext_ref[pl.ds(r, span), :]

    for cb in range(width // LANES):
        cols = pl.ds(cb * LANES, LANES)
        taps = [jnp.broadcast_to(w_ref[pl.ds(k, 1), cols], (SUBLANES, LANES)) for k in range(CONV_K)]
        bias = jnp.broadcast_to(cb_ref[:, cols], (SUBLANES, LANES))

        def rows(rb, carry, cols=cols, taps=taps, bias=bias):
            r0 = pl.multiple_of(rb * CONV_ROWS, CONV_ROWS)
            accs = [bias] * (CONV_ROWS // SUBLANES)
            for k in range(CONV_K):
                q, r = divmod(base + k, SUBLANES)
                src = ext_ref if r == 0 else sh_ref.at[r - 1]
                for j in range(len(accs)):
                    accs[j] = accs[j] + src[pl.ds(r0 + (q + j) * SUBLANES, SUBLANES), cols] * taps[k]
            for j, acc in enumerate(accs):
                acc_ref[pl.ds(r0 + j * SUBLANES, SUBLANES), cols] = acc
            return carry

        lax.fori_loop(0, tl // CONV_ROWS, rows, 0, unroll=2)
    x = acc_ref[...]
    mu = jnp.mean(x, axis=-1, keepdims=True)
    xc = x - mu
    var = jnp.mean(xc * xc, axis=-1, keepdims=True)
    y = xc * lax.rsqrt(var + EPS) * lg_ref[...] + lb_ref[...]
    o_ref[0] = (_silu(y) * sga_ref[0].astype(F32)).astype(BF16)


def conv_branch(a, sga, conv_w, conv_b, ln_g, ln_b, tl):
    bsz, length, width = a.shape
    hb = tl // CONV_HALO
    nh = length // CONV_HALO
    row = pl.BlockSpec((1, width), lambda b, i: (0, 0))
    main = pl.BlockSpec((1, tl, width), lambda b, i: (b, i, 0))
    return pl.pallas_call(
        _conv_kernel,
        name="conv_branch",
        grid=(bsz, length // tl),
        in_specs=[
            main,
            pl.BlockSpec((1, CONV_HALO, width), lambda b, i: (b, jnp.maximum(i * hb - 1, 0), 0)),
            pl.BlockSpec((1, CONV_HALO, width), lambda b, i: (b, jnp.minimum((i + 1) * hb, nh - 1), 0)),
            pl.BlockSpec((CONV_K, width), lambda b, i: (0, 0)),
            row, row, row,
            main,
        ],
        out_specs=main,
        out_shape=jax.ShapeDtypeStruct((bsz, length, width), BF16),
        scratch_shapes=[pltpu.VMEM((tl + 2 * CONV_HALO, width), F32),
                        pltpu.VMEM((SUBLANES - 1, tl + 2 * CONV_HALO - SUBLANES, width), F32),
                        pltpu.VMEM((tl, width), F32)],
        compiler_params=_params(("parallel", "parallel")),
    )(a, a, a, conv_w, conv_b.reshape(1, width), ln_g.reshape(1, width), ln_b.reshape(1, width), sga)


DFT_SPLIT = 64


def _dft_gen_kernel(ca_ref, sa_ref, cb_ref, sb_ref, c_ref, s_ref):
    ca, sa = ca_ref[0], sa_ref[0]
    cb, sb = cb_ref[...], sb_ref[...]
    c_ref[...] = (ca * cb - sa * sb).astype(BF16)
    s_ref[...] = (-(sa * cb + ca * sb)).astype(BF16)


def dft_matrices(length):
    na = length // DFT_SPLIT
    k = np.arange(length)[None, :]
    w = 2.0 * math.pi / length
    ang_a = ((DFT_SPLIT * np.arange(na)[:, None] * k) % length) * w
    ang_b = ((np.arange(DFT_SPLIT)[:, None] * k) % length) * w
    ca, sa = (jnp.asarray(f(ang_a).reshape(na, 1, length), F32) for f in (np.cos, np.sin))
    cb, sb = (jnp.asarray(f(ang_b), F32) for f in (np.cos, np.sin))
    tab_a = pl.BlockSpec((1, 1, length), lambda a: (a, 0, 0))
    tab_b = pl.BlockSpec((DFT_SPLIT, length), lambda a: (0, 0))
    out = pl.BlockSpec((DFT_SPLIT, length), lambda a: (a, 0))
    shp = jax.ShapeDtypeStruct((length, length), BF16)
    return pl.pallas_call(
        _dft_gen_kernel,
        name="dft_matrices",
        grid=(na,),
        in_specs=[tab_a, tab_a, tab_b, tab_b],
        out_specs=[out, out],
        out_shape=[shp, shp],
        compiler_params=_params(("parallel",)),
    )(ca, sa, cb, sb)


def _dft_kernel(c_ref, s_ref, p_ref, q_ref, sgb_ref, o_ref, acc_ref, *, scale):
    kk = pl.program_id(2)

    @pl.when(kk == 0)
    def _():
        acc_ref[...] = jnp.zeros_like(acc_ref)

    acc_ref[...] += _dot(c_ref[...], p_ref[0]) + _dot(s_ref[...], q_ref[0])

    @pl.when(kk == pl.num_programs(2) - 1)
    def _():
        o_ref[0] = (acc_ref[...] * scale * sgb_ref[0].astype(F32)).astype(BF16)


def fourier_branch(cmat, smat, p, q, sgb, scale, tm, tk):
    bsz, length, width = p.shape
    return pl.pallas_call(
        functools.partial(_dft_kernel, scale=scale),
        name="fourier_branch",
        grid=(bsz, length // tm, length // tk),
        in_specs=[
            pl.BlockSpec((tm, tk), lambda b, m, k: (m, k)),
            pl.BlockSpec((tm, tk), lambda b, m, k: (m, k)),
            pl.BlockSpec((1, tk, width), lambda b, m, k: (b, k, 0)),
            pl.BlockSpec((1, tk, width), lambda b, m, k: (b, k, 0)),
            pl.BlockSpec((1, tm, width), lambda b, m, k: (b, m, 0)),
        ],
        out_specs=pl.BlockSpec((1, tm, width), lambda b, m, k: (b, m, 0)),
        out_shape=jax.ShapeDtypeStruct((bsz, length, width), BF16),
        scratch_shapes=[pltpu.VMEM((tm, width), F32)],
        compiler_params=_params(("parallel", "parallel", "arbitrary")),
    )(cmat, smat, p, q, sgb)


FFT_MINOR = 64
FFT_COLS = 256


def _fft_kernel(p_ref, q_ref, k1a_ref, k1b_ref, twr_ref, twi_ref, hre_ref, him_ref, sgb_ref, o_ref,
                x_ref, y_ref, z_ref, *, scale):
    length, cw = p_ref.shape[1], p_ref.shape[2]
    ncb = cw // LANES
    n_m = length // FFT_MINOR
    n_a1 = FFT_MINOR // SUBLANES
    n_kb = n_m // SUBLANES
    slab = n_m * SUBLANES
    piece = SUBLANES * SUBLANES
    for cb in range(ncb):
        cols = pl.ds(cb * LANES, LANES)
        x_ref[cb] = p_ref[0, :, cols].astype(F32).reshape(n_m, FFT_MINOR, LANES)
        x_ref[ncb + cb] = q_ref[0, :, cols].astype(F32).reshape(n_m, FFT_MINOR, LANES)

    def slab_of(part, a1):
        tiles = [x_ref[part * ncb + cb, :, pl.ds(a1 * SUBLANES, SUBLANES), :].reshape(slab, LANES)
                 for cb in range(ncb)]
        return jnp.concatenate(tiles, axis=1).astype(BF16)

    for a1 in range(n_a1):
        y = _dot(k1a_ref[...], slab_of(0, a1)) + _dot(k1b_ref[...], slab_of(1, a1))
        tr, ti = twr_ref[a1], twi_ref[a1]
        for cb in range(ncb):
            yre = y[:slab, cb * LANES:(cb + 1) * LANES]
            yim = y[slab:, cb * LANES:(cb + 1) * LANES]
            zre = (tr * yre - ti * yim).astype(BF16)
            zim = (tr * yim + ti * yre).astype(BF16)
            for kb in range(n_kb):
                dst = (pl.ds(a1 * piece, piece), pl.ds(cb * LANES, LANES))
                y_ref[(kb, 0) + dst] = zre[kb * piece:(kb + 1) * piece]
                y_ref[(kb, 1) + dst] = zim[kb * piece:(kb + 1) * piece]

    for kb in range(n_kb):
        out = _dot(hre_ref[...], y_ref[kb, 0]) + _dot(him_ref[...], y_ref[kb, 1])
        z_ref[:, kb] = out.reshape(FFT_MINOR, SUBLANES, cw)

    o_ref[0] = (z_ref[...].reshape(length, cw) * scale * sgb_ref[0].astype(F32)).astype(BF16)


def _fft_tables(length):
    n_a, s = FFT_MINOR, SUBLANES
    n_m = length // n_a
    n_a1 = n_a // s
    eye = np.eye(s)
    ang_m = (np.outer(np.arange(n_m), np.arange(n_m)) % n_m) * (2.0 * math.pi / n_m)
    cm, sm = np.kron(np.cos(ang_m), eye), np.kron(np.sin(ang_m), eye)
    k1a = np.concatenate([cm, -sm], axis=0)
    k1b = np.concatenate([-sm, -cm], axis=0)
    a_full = s * np.arange(n_a1)[:, None, None] + np.arange(s)[None, None, :]
    ang_t = ((np.arange(n_m)[None, :, None] * a_full) % length) * (2.0 * math.pi / length)
    rep = lambda t: np.broadcast_to(t.reshape(n_a1, n_m * s, 1), (n_a1, n_m * s, LANES))
    twr, twi = rep(np.cos(ang_t)), rep(-np.sin(ang_t))
    ang_a = ((np.arange(n_a)[:, None, None] * a_full.reshape(1, n_a1, s)) % n_a) * (2.0 * math.pi / n_a)
    spread = lambda t: np.einsum("kas,pq->kpaqs", t, eye).reshape(n_a * s, n_a1 * s * s)
    hre, him = spread(np.cos(ang_a)), spread(np.sin(ang_a))
    return ([jnp.asarray(t, BF16) for t in (k1a, k1b)] + [jnp.asarray(t, F32) for t in (twr, twi)]
            + [jnp.asarray(t, BF16) for t in (hre, him)])


def fourier_branch_fft(p, q, sgb, scale):
    bsz, length, width = p.shape
    tables = _fft_tables(length)
    n_m = length // FFT_MINOR
    tok = pl.BlockSpec((1, length, FFT_COLS), lambda b, n: (b, 0, n))
    const = lambda t: pl.BlockSpec(t.shape, lambda b, n, nd=t.ndim: (0,) * nd)
    return pl.pallas_call(
        functools.partial(_fft_kernel, scale=scale),
        name="fourier_branch_fft",
        grid=(bsz, width // FFT_COLS),
        in_specs=[tok, tok] + [const(t) for t in tables] + [tok],
        out_specs=tok,
        out_shape=jax.ShapeDtypeStruct((bsz, length, width), BF16),
        scratch_shapes=[pltpu.VMEM((2 * FFT_COLS // LANES, n_m, FFT_MINOR, LANES), F32),
                        pltpu.VMEM((n_m // SUBLANES, 2, FFT_MINOR * SUBLANES, FFT_COLS), BF16),
                        pltpu.VMEM((FFT_MINOR, n_m // SUBLANES, SUBLANES, FFT_COLS), F32)],
        compiler_params=_params(("parallel", "parallel")),
    )(p, q, *tables, sgb)


def _postnorm_residual(y, h_ref, pg_ref, gt_ref, o_ref):
    o_ref[0] = h_ref[0] + gt_ref[0] * (_rms(y) * pg_ref[...])


def _even_out_kernel(a_ref, b_ref, wa_ref, wb_ref, h_ref, pg_ref, gt_ref, o_ref):
    y = _dot(a_ref[0], wa_ref[...]) + _dot(b_ref[0], wb_ref[...])
    _postnorm_residual(y, h_ref, pg_ref, gt_ref, o_ref)


def even_out_proj(a, b, wa, wb, h, post_g, gate, tm):
    bsz, length, d = h.shape
    width = a.shape[2]
    half = pl.BlockSpec((1, tm, width), lambda bb, i: (bb, i, 0))
    wspec = pl.BlockSpec((width, d), lambda bb, i: (0, 0))
    full = pl.BlockSpec((1, tm, d), lambda bb, i: (bb, i, 0))
    return pl.pallas_call(
        _even_out_kernel,
        name="even_out_proj",
        grid=(bsz, length // tm),
        in_specs=[half, half, wspec, wspec, full,
                  pl.BlockSpec((1, d), lambda bb, i: (0, 0)),
                  pl.BlockSpec((1, 1, d), lambda bb, i: (bb, 0, 0))],
        out_specs=full,
        out_shape=jax.ShapeDtypeStruct((bsz, length, d), F32),
        compiler_params=_params(("parallel", "parallel")),
    )(a, b, wa, wb, h, post_g.reshape(1, d), gate)


ODD_TN = 1024


ODD_PARTS = 2


def _odd_in_kernel(h_ref, g_ref, sh_ref, sc_ref, w_ref, o_ref, xn_ref, *, gate_parts):
    n = pl.program_id(2)

    @pl.when(n == 0)
    def _():
        _prenorm_modulate(h_ref, g_ref, sh_ref, sc_ref, xn_ref)

    xn = xn_ref[...]
    pw = ODD_TN // ODD_PARTS
    for j in range(ODD_PARTS):
        cols = pl.ds(j * pw, pw)
        acc = _dot(xn, w_ref[:, cols])
        tiles = [g // ODD_PARTS for g in gate_parts if g % ODD_PARTS == j]
        if tiles:
            is_gate = functools.reduce(jnp.logical_or, [n == t for t in tiles])
            acc = jnp.where(is_gate, _silu(acc), acc)
        o_ref[0, :, cols] = acc.astype(BF16)


def odd_in_proj(h, pre_g, shift, scale, w_all, gate_ranges, tm):
    bsz, length, d = h.shape
    n_all = w_all.shape[1]
    pw = ODD_TN // ODD_PARTS
    assert n_all % ODD_TN == 0 and all(s % pw == 0 and e % pw == 0 for s, e in gate_ranges)
    gate_parts = tuple(p for s, e in gate_ranges for p in range(s // pw, e // pw))
    vec = pl.BlockSpec((1, 1, d), lambda b, i, n: (b, 0, 0))
    return pl.pallas_call(
        functools.partial(_odd_in_kernel, gate_parts=gate_parts),
        name="odd_in_proj",
        grid=(bsz, length // tm, n_all // ODD_TN),
        in_specs=[pl.BlockSpec((1, tm, d), lambda b, i, n: (b, i, 0)),
                  pl.BlockSpec((1, d), lambda b, i, n: (0, 0)),
                  vec, vec,
                  pl.BlockSpec((d, ODD_TN), lambda b, i, n: (0, n))],
        out_specs=pl.BlockSpec((1, tm, ODD_TN), lambda b, i, n: (b, i, n)),
        out_shape=jax.ShapeDtypeStruct((bsz, length, n_all), BF16),
        scratch_shapes=[pltpu.VMEM((tm, d), BF16)],
        compiler_params=_params(("parallel", "parallel", "arbitrary")),
    )(h, pre_g.reshape(1, d), shift, scale, w_all)


def _na_pair_geometry(variant, rows):
    r0 = {0: NA_QROWS, 1: 0, 2: rows - NA_QROWS}[variant]
    ws = min(max(r0 - NA_ROWS // 2, 0), rows - NA_WIN_ROWS)
    return r0, ws


def _na_bias_kernel(rpb_ref, o_ref, *, rows):
    h = pl.program_id(0)
    n_dr, n_dc = 2 * NA_ROWS - 1, 2 * NA_COLS - 1
    qc = lax.broadcasted_iota(jnp.int32, (GRID_W, LANES), 0)
    lane = lax.broadcasted_iota(jnp.int32, (GRID_W, LANES), 1)
    kc = lane % GRID_W
    hi = lane // GRID_W
    diff = kc - qc
    c_start = jnp.clip(qc - NA_COLS // 2, 0, GRID_W - NA_COLS)
    col_ok = jnp.where(kc >= c_start, 1, 0) * jnp.where(kc < c_start + NA_COLS, 1, 0)
    blocks = []
    for dr in range(n_dr):
        val = jnp.full((GRID_W, LANES), NEG_INF, F32)
        for dc in range(n_dc):
            val = jnp.where(diff == dc - (NA_COLS - 1), rpb_ref[(h * n_dr + dr) * n_dc + dc], val)
        blocks.append(jnp.where(col_ok == 1, val, NEG_INF))
    masked = jnp.full((GRID_W, LANES), NEG_INF, F32)
    for variant in range(NA_VARIANTS):
        r0, ws = _na_pair_geometry(variant, rows)
        for ri in range(NA_QROWS):
            r = r0 + ri
            r_start = min(max(r - NA_ROWS // 2, 0), rows - NA_ROWS)
            for wp in range(NA_WIN_ROWS // 2):
                halves = []
                for a in (ws + 2 * wp, ws + 2 * wp + 1):
                    in_window = r_start <= a < r_start + NA_ROWS
                    halves.append(blocks[a - r + NA_ROWS - 1] if in_window else masked)
                o_ref[0, variant, pl.ds(ri * GRID_W, GRID_W), pl.ds(wp * LANES, LANES)] = jnp.where(
                    hi == 1, halves[1], halves[0])


def na_bias_table(rpb, rows):
    heads = rpb.shape[0]
    return pl.pallas_call(
        functools.partial(_na_bias_kernel, rows=rows),
        name="na_bias_table",
        grid=(heads,),
        in_specs=[pl.BlockSpec(memory_space=pltpu.SMEM)],
        out_specs=pl.BlockSpec((1, NA_VARIANTS, NA_PAIR, NA_WIN), lambda h: (h, 0, 0, 0)),
        out_shape=jax.ShapeDtypeStruct((heads, NA_VARIANTS, NA_PAIR, NA_WIN), F32),
        compiler_params=_params(("parallel",)),
    )(rpb.reshape(-1))


def _na_kernel(q_ref, k_ref, v_ref, kc_ref, vc_ref, bias_ref, sg_ref, o_ref,
               vx_ref, vcx_ref, s0_ref, s1_ref, p0_ref, p1_ref, *, rows):
    npairs = rows // NA_QROWS
    last = npairs - 1
    n_heads = q_ref.shape[2] // HEAD_DIM
    nt = (((1,), (1,)), ((), ()))
    head = lambda hh: pl.ds(hh * HEAD_DIM, HEAD_DIM)
    for hh in range(n_heads):
        vx_ref[hh, :, :HEAD_DIM] = v_ref[0, :, head(hh)]
        vx_ref[hh, :, HEAD_DIM:] = jnp.ones((vx_ref.shape[1], HEAD_DIM), BF16)
        vcx_ref[hh, :, :HEAD_DIM] = vc_ref[0, :, head(hh)]
        vcx_ref[hh, :, HEAD_DIM:] = jnp.ones((vcx_ref.shape[1], HEAD_DIM), BF16)

    def window(pr):
        ws = min(max(NA_QROWS * pr - NA_ROWS // 2, 0), rows - NA_WIN_ROWS)
        return pl.ds(ws * GRID_W, NA_WIN)

    def qrows(pr):
        return pl.ds(pr * NA_PAIR, NA_PAIR)

    def scores(unit, s_ref):
        hh, pr = divmod(unit, npairs)
        variant = {0: 1, last: 2}.get(pr, 0)
        q = q_ref[0, qrows(pr), head(hh)]
        s_ref[:, :NA_WIN] = lax.dot_general(q, k_ref[0, window(pr), head(hh)], nt,
                                            preferred_element_type=F32) + bias_ref[hh, variant]
        s_ref[:, NA_WIN:] = lax.dot_general(q, kc_ref[0, :, head(hh)], nt, preferred_element_type=F32)

    def probs(s_ref, p_ref):
        for rows_half in (pl.ds(0, NA_PAIR // 2), pl.ds(NA_PAIR // 2, NA_PAIR // 2)):
            s = s_ref[rows_half, :]
            p_ref[rows_half, :] = jnp.exp(s - jnp.max(s, axis=-1, keepdims=True)).astype(BF16)

    def values(unit, p_ref):
        hh, pr = divmod(unit, npairs)
        acc = _dot(p_ref[:, :NA_WIN], vx_ref[hh, window(pr), :]) + _dot(p_ref[:, NA_WIN:], vcx_ref[hh])
        o = acc[:, :HEAD_DIM] / acc[:, HEAD_DIM:]
        o_ref[0, qrows(pr), head(hh)] = (o * sg_ref[0, qrows(pr), head(hh)].astype(F32)).astype(BF16)

    units = n_heads * npairs
    bufs = ((s0_ref, p0_ref), (s1_ref, p1_ref))
    scores(0, s0_ref)
    scores(1, s1_ref)
    probs(s0_ref, p0_ref)
    for i in range(units):
        s_cur, p_cur = bufs[i % 2]
        s_nxt, p_nxt = bufs[(i + 1) % 2]
        values(i, p_cur)
        if i + 1 < units:
            probs(s_nxt, p_nxt)
        if i + 2 < units:
            scores(i + 2, s_cur)


def neighbourhood_attention(proj, proj_c, bias, na_w, gate_col):
    bsz, length, _ = proj.shape
    lc = proj_c.shape[1]
    heads = na_w // HEAD_DIM
    rows = length // GRID_W
    nh = NA_HEADS_PER_STEP
    assert heads % nh == 0 and gate_col % (nh * HEAD_DIM) == 0
    groups = heads // nh
    seq = lambda off: pl.BlockSpec((1, length, nh * HEAD_DIM), lambda h, b: (b, 0, off + h))
    cseq = lambda off: pl.BlockSpec((1, lc, nh * HEAD_DIM), lambda h, b: (b, 0, off + h))
    return pl.pallas_call(
        functools.partial(_na_kernel, rows=rows),
        name="neighbourhood_attention",
        grid=(groups, bsz),
        in_specs=[seq(0), seq(groups), seq(2 * groups), cseq(groups), cseq(2 * groups),
                  pl.BlockSpec((nh, NA_VARIANTS, NA_PAIR, NA_WIN), lambda h, b: (h, 0, 0, 0)),
                  seq(gate_col // (nh * HEAD_DIM))],
        out_specs=seq(0),
        out_shape=jax.ShapeDtypeStruct((bsz, length, na_w), BF16),
        scratch_shapes=[pltpu.VMEM((nh, length, 2 * HEAD_DIM), BF16), pltpu.VMEM((nh, lc, 2 * HEAD_DIM), BF16),
                        pltpu.VMEM((NA_PAIR, NA_WIN + lc), F32), pltpu.VMEM((NA_PAIR, NA_WIN + lc), F32),
                        pltpu.VMEM((NA_PAIR, NA_WIN + lc), BF16), pltpu.VMEM((NA_PAIR, NA_WIN + lc), BF16)],
        compiler_params=_params(("parallel", "parallel")),
    )(proj, proj, proj, proj_c, proj_c, bias, proj)


S5_CHUNK = 128
S5_IN_BLOCK = LANES
S5_OUT_BLOCK = 256
S5_SCAN_COLS = 1024


def _s5_kernel(ufc_ref, ufx_ref, ubc_ref, ubx_ref, bf_ref, bb_ref, crf_ref, cif_ref, crb_ref, cib_ref,
               lre_ref, lim_ref, yf_ref, yb_ref, buf_ref, state_ref, stage_ref, *, ctx_chunks):
    bsz, t_len, width = ufx_ref.shape
    n_state = lre_ref.shape[1]
    in_ctx = pl.program_id(0) < ctx_chunks
    n_in = width // S5_IN_BLOCK
    blk = n_state // n_in
    rows8 = 2 * bsz

    @pl.when(pl.program_id(0) == 0)
    def _():
        state_ref[...] = jnp.zeros_like(state_ref)

    nlb = n_state // LANES
    per_in = blk // LANES
    ti = lax.broadcasted_iota(jnp.int32, (t_len, t_len), 0)
    tj = lax.broadcasted_iota(jnp.int32, (t_len, t_len), 1)
    rev = jnp.where(ti + tj == t_len - 1, 1.0, 0.0).astype(BF16)
    u_fwd = jnp.where(in_ctx, ufc_ref[...], ufx_ref[...]).reshape(bsz * t_len, width)
    u_nat = jnp.where(in_ctx, ubc_ref[...], ubx_ref[...])
    u_bwd = jnp.concatenate([_dot(rev, u_nat[b]).astype(BF16) for b in range(bsz)], axis=0)
    sides = ((u_fwd, bf_ref, yf_ref, crf_ref, cif_ref), (u_bwd, bb_ref, yb_ref, crb_ref, cib_ref))

    def project(kb):
        for di, (u, w_ref, _, _, _) in enumerate(sides):
            res = _dot(u[:, kb * S5_IN_BLOCK:(kb + 1) * S5_IN_BLOCK], w_ref[kb])
            for b in range(bsz):
                rsel = pl.ds(di * bsz + b, t_len, stride=rows8)
                for c in range(2 * per_in):
                    dst = (c // per_in) * nlb + kb * per_in + c % per_in
                    buf_ref[dst, rsel, :] = res[b * t_len:(b + 1) * t_len, c * LANES:(c + 1) * LANES]

    per = S5_SCAN_COLS // LANES

    def scan(group, t0, t1, carry):
        blocks = [group * per + i for i in range(per)]
        lre = [lre_ref[:, pl.ds(c * LANES, LANES)] for c in blocks]
        lim = [lim_ref[:, pl.ds(c * LANES, LANES)] for c in blocks]
        if carry is None:
            carry = [state_ref[:, pl.ds(part * n_state + c * LANES, LANES)] for c in blocks for part in range(2)]
        for t in range(t0, t1):
            row = pl.ds(t * rows8, rows8)
            for i, c in enumerate(blocks):
                hre, him = carry[2 * i], carry[2 * i + 1]
                carry[2 * i] = lre[i] * hre - lim[i] * him + buf_ref[c, row, :]
                carry[2 * i + 1] = lre[i] * him + lim[i] * hre + buf_ref[nlb + c, row, :]
                buf_ref[c, row, :] = carry[2 * i]
                buf_ref[nlb + c, row, :] = carry[2 * i + 1]
        if t1 == t_len:
            for i, c in enumerate(blocks):
                state_ref[:, pl.ds(c * LANES, LANES)] = carry[2 * i]
                state_ref[:, pl.ds(n_state + c * LANES, LANES)] = carry[2 * i + 1]
        return carry

    n_out = width // S5_OUT_BLOCK
    per_out = nlb // n_out

    def readout(di, j):
        _, _, y_ref, cr_ref, ci_ref = sides[di]
        stage = stage_ref.at[di * n_out + j]
        for part in range(2):
            for b in range(bsz):
                rsel = pl.ds(di * bsz + b, t_len, stride=rows8)
                for c in range(per_out):
                    stage[part, pl.ds(b * t_len, t_len), pl.ds(c * LANES, LANES)] = buf_ref[
                        part * nlb + j * per_out + c, rsel, :].astype(BF16)
        y = _dot(stage[0], cr_ref[j]) + _dot(stage[1], ci_ref[j])
        for b in range(bsz):
            yb = y[b * t_len:(b + 1) * t_len]
            if di == 1:
                hi = yb.astype(BF16)
                r1 = yb - hi.astype(F32)
                mid = r1.astype(BF16)
                lo = (r1 - mid.astype(F32)).astype(BF16)
                yb = _dot(rev, hi) + _dot(rev, mid) + _dot(rev, lo)
            y_ref[b, :, pl.ds(j * S5_OUT_BLOCK, S5_OUT_BLOCK)] = yb

    assert n_in == 4 and nlb // per == 2 and n_out == 2 and per_out == per
    half = t_len // 2
    project(0)
    project(1)
    carry = scan(0, 0, half, None)
    project(2)
    scan(0, half, t_len, carry)
    project(3)
    carry = scan(1, 0, half, None)
    readout(0, 0)
    scan(1, half, t_len, carry)
    readout(1, 0)
    readout(0, 1)
    readout(1, 1)


def _block_diag(x):
    nblk, g, r, c = x.shape
    eye = jnp.eye(g, dtype=x.dtype)
    return jnp.einsum("kgrc,gh->kgrhc", x, eye).reshape(nblk, g * r, g * c)


def _s5_direction_params(a_re, a_im, log_dt, b_re, b_im, c_re, c_im):
    groups, n_p = a_re.shape
    lam = lax.complex(a_re.astype(F32), a_im.astype(F32))
    dt = jnp.exp(log_dt.astype(F32))[:, None]
    lam_bar = jnp.exp(lam * dt)
    b_bar = ((lam_bar - 1.0) / lam)[..., None] * lax.complex(b_re.astype(F32), b_im.astype(F32))
    gi = S5_IN_BLOCK // SSM_GROUP
    bt = jnp.swapaxes(b_bar, 1, 2).reshape(groups // gi, gi, SSM_GROUP, n_p)
    b_mat = jnp.concatenate([_block_diag(jnp.real(bt)), _block_diag(jnp.imag(bt))], axis=-1)
    go = S5_OUT_BLOCK // SSM_GROUP
    ct = lambda c: jnp.swapaxes(c.astype(F32), 1, 2).reshape(groups // go, go, n_p, SSM_GROUP)
    return (b_mat.astype(BF16), _block_diag(ct(c_re)).astype(BF16), _block_diag(-ct(c_im)).astype(BF16),
            jnp.real(lam_bar).reshape(-1), jnp.imag(lam_bar).reshape(-1))


def s5_scan(proj_c, proj_x, d_col, width, fwd, bwd):
    bsz, lc, _ = proj_c.shape
    length = proj_x.shape[1]
    t_len = S5_CHUNK
    ncc, ncx = lc // t_len, length // t_len
    nc = ncc + ncx
    col = d_col // width
    n_state = fwd[3].shape[0]
    lre = jnp.concatenate([jnp.broadcast_to(fwd[3], (bsz, n_state)), jnp.broadcast_to(bwd[3], (bsz, n_state))])
    lim = jnp.concatenate([jnp.broadcast_to(fwd[4], (bsz, n_state)), jnp.broadcast_to(bwd[4], (bsz, n_state))])
    full = lambda x: pl.BlockSpec(x.shape, lambda c, nd=x.ndim: (0,) * nd)
    blk = lambda index: pl.BlockSpec((bsz, t_len, width), index)
    in_blocks = [blk(lambda c: (0, jnp.minimum(c, ncc - 1), col)),
                 blk(lambda c: (0, jnp.maximum(c - ncc, 0), col)),
                 blk(lambda c: (0, jnp.maximum(ncc - 1 - c, 0), col)),
                 blk(lambda c: (0, jnp.minimum(nc - 1 - c, ncx - 1), col))]
    out = jax.ShapeDtypeStruct((bsz, length, width), F32)
    consts = (fwd[0], bwd[0], fwd[1], fwd[2], bwd[1], bwd[2], lre, lim)
    return pl.pallas_call(
        functools.partial(_s5_kernel, ctx_chunks=ncc),
        name="s5_scan",
        grid=(nc,),
        in_specs=in_blocks + [full(x) for x in consts],
        out_specs=[blk(lambda c: (0, jnp.maximum(c - ncc, 0), 0)),
                   blk(lambda c: (0, jnp.minimum(nc - 1 - c, ncx - 1), 0))],
        out_shape=[out, out],
        scratch_shapes=[pltpu.VMEM((2 * n_state // LANES, t_len * 2 * bsz, LANES), F32),
                        pltpu.VMEM((2 * bsz, 2 * n_state), F32),
                        pltpu.VMEM((2 * (width // S5_OUT_BLOCK), 2, bsz * t_len,
                                    n_state // (width // S5_OUT_BLOCK)), BF16)],
        compiler_params=_params(("arbitrary",)),
    )(proj_c, proj_x, proj_c, proj_x, *consts)


def _odd_out_kernel(na_ref, yf_ref, yb_ref, d_ref, sdg_ref, dsk_ref, wglu_ref, wna_ref, wssm_ref,
                    h_ref, pg_ref, gt_ref, o_ref):
    y = yf_ref[0] + yb_ref[0] + dsk_ref[...] * d_ref[0].astype(F32)
    y = 0.5 * y * (1.0 + jnp.tanh(math.sqrt(2.0 / math.pi) * (y + 0.044715 * (y * y * y))))
    z = y * jax.nn.sigmoid(_dot(y.astype(BF16), wglu_ref[...]))
    s = (z * sdg_ref[0].astype(F32)).astype(BF16)
    out = _dot(na_ref[0], wna_ref[...]) + _dot(s, wssm_ref[...])
    _postnorm_residual(out, h_ref, pg_ref, gt_ref, o_ref)


def odd_out_proj(na, yf, yb, proj, d_col, dg_col, d_skip, w_glu, w_na, w_ssm, h, post_g, gate, tm):
    bsz, length, dm = h.shape
    na_w, ssm_w = na.shape[2], yf.shape[2]
    tok = lambda w, col=0: pl.BlockSpec((1, tm, w), lambda b, i: (b, i, col // w))
    const = lambda r, c: pl.BlockSpec((r, c), lambda b, i: (0, 0))
    return pl.pallas_call(
        _odd_out_kernel,
        name="odd_out_proj",
        grid=(bsz, length // tm),
        in_specs=[tok(na_w), tok(ssm_w), tok(ssm_w),
                  tok(ssm_w, d_col), tok(ssm_w, dg_col),
                  const(1, ssm_w), const(ssm_w, ssm_w), const(na_w, dm), const(ssm_w, dm),
                  tok(dm), const(1, dm),
                  pl.BlockSpec((1, 1, dm), lambda b, i: (b, 0, 0))],
        out_specs=tok(dm),
        out_shape=jax.ShapeDtypeStruct((bsz, length, dm), F32),
        compiler_params=_params(("parallel", "parallel")),
    )(na, yf, yb, proj, proj, d_skip.reshape(1, ssm_w), w_glu, w_na, w_ssm, h, post_g.reshape(1, dm), gate)


def _small_dft(n):
    ang = (np.outer(np.arange(n), np.arange(n)) % n) * (2.0 * math.pi / n)
    return jnp.asarray(np.concatenate([np.cos(ang), np.sin(ang)], axis=1), BF16)


def _conv_fourier_layer(h, pre_g, post_g, shift, scale, gate, w5, w_out_a, w_out_b, conv_w, conv_b,
                        ln_g, ln_b, fourier_g, cs_small, shared_mod):
    bsz, length, _ = h.shape
    gd = fourier_g.shape[1]
    fold = (lambda t: t.reshape(1, bsz * length, t.shape[2])) if shared_mod else (lambda t: t)
    unfold = (lambda t: t.reshape(bsz, length, t.shape[2])) if shared_mod else (lambda t: t)
    vec = (lambda v: v[:1]) if shared_mod else (lambda v: v)
    tokens = bsz * length if shared_mod else length
    outs = even_in_proj(fold(h), pre_g, vec(shift), vec(scale), w5, fourier_g, cs_small,
                        _token_tile(tokens, IN_PROJ_ROWS))
    a, sga, p, q, sgb = [unfold(t) for t in outs]
    a = conv_branch(a, sga, conv_w, conv_b, ln_g, ln_b, _token_tile(length, CONV_TILE_ROWS))
    ortho = 1.0 / math.sqrt(length * gd)
    if (length // FFT_MINOR) % SUBLANES == 0:
        f = fourier_branch_fft(p, q, sgb, ortho)
    else:
        cmat, smat = dft_matrices(length)
        f = fourier_branch(cmat, smat, p, q, sgb, ortho, length, length)
    return unfold(even_out_proj(fold(a), fold(f), w_out_a, w_out_b, fold(h), post_g, vec(gate),
                                _token_tile(tokens, OUT_PROJ_ROWS)))


def kernel(x, c, ctx, c_ctx, pre_g, post_g, ada_w, ada_b, ab_w_in, ab_w_out, conv_w, conv_b, conv_ln_g,
           conv_ln_b, fourier_g, cd_w_in, cd_w_out, na_rpb, s5_a_re, s5_a_im, s5_log_dt, s5_b_re, s5_b_im,
           s5_c_re, s5_c_im, s5_d, s5_w_glu):
    bsz, length, d = x.shape
    lc = ctx.shape[1]
    depth = ada_w.shape[0]
    assert depth == 2 and length % NA_PAIR == 0 and length // GRID_W >= NA_WIN_ROWS
    assert lc % S5_CHUNK == 0 and length % S5_CHUNK == 0

    cond_rows = jnp.zeros((SUBLANES, d), F32).at[:bsz].set(c).at[bsz].set(c_ctx)
    mods = ada_modulation(cond_rows, ada_w, ada_b)

    def mod_vectors(i):
        xs = [mods[i, :bsz, k * d:(k + 1) * d][:, None, :] for k in range(3)]
        cs = [jnp.broadcast_to(mods[i, bsz, k * d:(k + 1) * d], (bsz, 1, d)) for k in range(3)]
        return xs, cs

    (sh_x, sc_x, gt_x), (sh_c, sc_c, gt_c) = mod_vectors(0)
    w_in = ab_w_in[0]
    conv_width = conv_w.shape[2]
    cw = conv_width
    fw = (w_in.shape[1] - 3 * cw) // 2
    assert cw == fw
    w5 = w_in.astype(BF16)
    w_out_a = ab_w_out[0, :cw].astype(BF16)
    w_out_b = ab_w_out[0, cw:].astype(BF16)
    cs_small = _small_dft(fourier_g.shape[2])
    layer0 = functools.partial(
        _conv_fourier_layer, pre_g=pre_g[0], post_g=post_g[0], w5=w5, w_out_a=w_out_a, w_out_b=w_out_b,
        conv_w=conv_w[0], conv_b=conv_b[0], ln_g=conv_ln_g[0], ln_b=conv_ln_b[0], fourier_g=fourier_g[0],
        cs_small=cs_small)
    h_x = layer0(x, shift=sh_x, scale=sc_x, gate=gt_x, shared_mod=False)
    h_c = layer0(ctx, shift=sh_c, scale=sc_c, gate=gt_c, shared_mod=True)

    (sh_x, sc_x, gt_x), (sh_c, sc_c, _) = mod_vectors(1)
    w_in = cd_w_in[0]
    ssm_w = s5_d.shape[1]
    na_w = (w_in.shape[1] - 2 * ssm_w) // 4
    g_col, d_col, dg_col = 3 * na_w, 4 * na_w, 4 * na_w + ssm_w
    gates = ((g_col, d_col), (dg_col, dg_col + ssm_w))
    col_scale = jnp.where(jnp.arange(w_in.shape[1]) < na_w, HEAD_DIM ** -0.5, 1.0).astype(F32)
    w_all = (w_in * col_scale).astype(BF16)
    proj_x = odd_in_proj(h_x, pre_g[1], sh_x, sc_x, w_all, gates, _token_tile(length, IN_PROJ_ROWS))
    proj_c = odd_in_proj(h_c.reshape(1, bsz * lc, d), pre_g[1], sh_c[:1], sc_c[:1], w_all, gates,
                         _token_tile(bsz * lc, IN_PROJ_ROWS)).reshape(bsz, lc, -1)

    bias = na_bias_table(na_rpb[0], length // GRID_W)
    na = neighbourhood_attention(proj_x, proj_c, bias, na_w, g_col)

    dirs = [_s5_direction_params(s5_a_re[0, i], s5_a_im[0, i], s5_log_dt[0, i], s5_b_re[0, i], s5_b_im[0, i],
                                 s5_c_re[0, i], s5_c_im[0, i]) for i in range(2)]
    y_f, y_b = s5_scan(proj_c, proj_x, d_col, ssm_w, *dirs)

    return odd_out_proj(na, y_f, y_b, proj_x, d_col, dg_col, s5_d[0], s5_w_glu[0].astype(BF16),
                        cd_w_out[0, :na_w].astype(BF16), cd_w_out[0, na_w:].astype(BF16),
                        h_x, post_g[1], gt_x, _token_tile(length, OUT_PROJ_ROWS))
```

```python
import functools
import math

import jax
import jax.numpy as jnp
import numpy as np
from jax import lax
from jax.experimental import pallas as pl
from jax.experimental.pallas import tpu as pltpu

F32 = jnp.float32
BF16 = jnp.bfloat16

EPS = 1e-6
NEG_INF = -1e30

GRID_W = 64
CONV_K = 31
FOURIER_GROUPS = 4
HEAD_DIM = 128
NA_ROWS = 8
NA_COLS = 16
SSM_GROUP = 16
SSM_STATE = 64

LANES = 128
SUBLANES = 8
VMEM_LIMIT = 56 * 1024 * 1024

IN_PROJ_ROWS = 1024
OUT_PROJ_ROWS = 512
CONV_TILE_ROWS = 256
ADA_COLS = 1024


def _token_tile(n_tokens, cap):
    tile = min(cap, n_tokens)
    assert n_tokens % tile == 0
    return tile

NA_QROWS = 4
NA_PAIR = NA_QROWS * GRID_W
NA_WIN_ROWS = NA_ROWS + NA_QROWS
NA_WIN = NA_WIN_ROWS * GRID_W
NA_VARIANTS = 3
NA_HEADS_PER_STEP = 2


def _params(sem, vmem=VMEM_LIMIT):
    return pltpu.CompilerParams(dimension_semantics=sem, vmem_limit_bytes=vmem)


def _silu(x):
    return x * jax.nn.sigmoid(x)


def _rms(x):
    return x * lax.rsqrt(jnp.mean(x * x, axis=-1, keepdims=True) + EPS)


def _dot(a, b):
    return jnp.dot(a, b, preferred_element_type=F32)


def _ada_kernel(c_ref, w_ref, b_ref, o_ref):
    cond = _silu(c_ref[...])
    o_ref[0] = _dot(cond.astype(BF16), w_ref[0].astype(BF16)) + b_ref[0]


def ada_modulation(cond_rows, ada_w, ada_b):
    depth, d, n = ada_w.shape
    tn = _token_tile(n, ADA_COLS)
    rows = cond_rows.shape[0]
    return pl.pallas_call(
        _ada_kernel,
        name="ada_modulation",
        grid=(depth, n // tn),
        in_specs=[
            pl.BlockSpec((rows, d), lambda i, j: (0, 0)),
            pl.BlockSpec((1, d, tn), lambda i, j: (i, 0, j)),
            pl.BlockSpec((1, 1, tn), lambda i, j: (i, 0, j)),
        ],
        out_specs=pl.BlockSpec((1, rows, tn), lambda i, j: (i, 0, j)),
        out_shape=jax.ShapeDtypeStruct((depth, rows, n), F32),
        compiler_params=_params(("parallel", "parallel")),
    )(cond_rows, ada_w, ada_b.reshape(depth, 1, n))


PRENORM_ROWS = 16


def _prenorm_modulate(h_ref, g_ref, sh_ref, sc_ref, xn_ref):
    gain = g_ref[...] * (1.0 + sc_ref[0])
    shift = sh_ref[0]

    def chunk(r, carry):
        rows = pl.ds(pl.multiple_of(r * PRENORM_ROWS, PRENORM_ROWS), PRENORM_ROWS)
        xn_ref[rows, :] = (_rms(h_ref[0, rows, :]) * gain + shift).astype(BF16)
        return carry

    lax.fori_loop(0, xn_ref.shape[0] // PRENORM_ROWS, chunk, 0, unroll=4)


EVEN_PARTS = 5


def _even_in_kernel(h_ref, g_ref, sh_ref, sc_ref, w0_ref, w1_ref, w2_ref, w3_ref, w4_ref, fg_ref, cs_ref,
                    a_ref, sga_ref, p_ref, q_ref, sgb_ref, xn_ref):
    @pl.when(pl.program_id(2) == 0)
    def _():
        _prenorm_modulate(h_ref, g_ref, sh_ref, sc_ref, xn_ref)

    gd = a_ref.shape[2]
    w = jnp.concatenate([w0_ref[...], w1_ref[...], w2_ref[...], w3_ref[...], w4_ref[...]], axis=1)
    acc = _dot(xn_ref[...], w)
    part = lambda k: acc[:, k * gd:(k + 1) * gd]
    a_ref[0] = (part(0) * jax.nn.sigmoid(part(1))).astype(BF16)
    sga_ref[0] = _silu(part(2)).astype(BF16)
    bn = _rms(part(3)) * fg_ref[0]
    pq = _dot(bn.astype(BF16), cs_ref[...])
    p_ref[0] = pq[:, :gd].astype(BF16)
    q_ref[0] = pq[:, gd:].astype(BF16)
    sgb_ref[0] = _silu(part(4)).astype(BF16)


def even_in_proj(h, pre_g, shift, scale, w_in, fourier_g, cs_small, tm):
    bsz, length, d = h.shape
    gd = fourier_g.shape[1]
    width = FOURIER_GROUPS * gd
    wspecs = [pl.BlockSpec((d, gd), lambda b, i, n, k=k: (0, k * FOURIER_GROUPS + n)) for k in range(EVEN_PARTS)]
    ospec = pl.BlockSpec((1, tm, gd), lambda b, i, n: (b, i, n))
    vec = pl.BlockSpec((1, 1, d), lambda b, i, n: (b, 0, 0))
    out = jax.ShapeDtypeStruct((bsz, length, width), BF16)
    return pl.pallas_call(
        _even_in_kernel,
        name="even_in_proj",
        grid=(bsz, length // tm, FOURIER_GROUPS),
        in_specs=[
            pl.BlockSpec((1, tm, d), lambda b, i, n: (b, i, 0)),
            pl.BlockSpec((1, d), lambda b, i, n: (0, 0)),
            vec, vec,
            *wspecs,
            pl.BlockSpec((1, 1, gd), lambda b, i, n: (n, 0, 0)),
            pl.BlockSpec((gd, 2 * gd), lambda b, i, n: (0, 0)),
        ],
        out_specs=[ospec] * 5,
        out_shape=[out] * 5,
        scratch_shapes=[pltpu.VMEM((tm, d), BF16)],
        compiler_params=_params(("parallel", "parallel", "arbitrary")),
    )(h, pre_g.reshape(1, d), shift, scale, *([w_in] * EVEN_PARTS),
      fourier_g.reshape(FOURIER_GROUPS, 1, gd), cs_small)


CONV_HALO = 16
CONV_ROWS = 32


def _conv_kernel(a_ref, prev_ref, next_ref, w_ref, cb_ref, lg_ref, lb_ref, sga_ref, o_ref,
                 ext_ref, sh_ref, acc_ref):
    i = pl.program_id(1)
    last = pl.num_programs(1) - 1
    tl, width = acc_ref.shape
    ext_ref[pl.ds(CONV_HALO, tl), :] = a_ref[0].astype(F32)
    ext_ref[pl.ds(0, CONV_HALO), :] = jnp.where(i > 0, prev_ref[0].astype(F32), 0.0)
    ext_ref[pl.ds(CONV_HALO + tl, CONV_HALO), :] = jnp.where(i < last, next_ref[0].astype(F32), 0.0)
    base = CONV_HALO - CONV_K // 2
    span = sh_ref.shape[1]
    for r in range(1, SUBLANES):
        sh_ref[r - 1] = ext_ref[pl.ds(r, span), :]

    for cb in range(width // LANES):
        cols = pl.ds(cb * LANES, LANES)
        taps = [jnp.broadcast_to(w_ref[pl.ds(k, 1), cols], (SUBLANES, LANES)) for k in range(CONV_K)]
        bias = jnp.broadcast_to(cb_ref[:, cols], (SUBLANES, LANES))

        def rows(rb, carry, cols=cols, taps=taps, bias=bias):
            r0 = pl.multiple_of(rb * CONV_ROWS, CONV_ROWS)
            accs = [bias] * (CONV_ROWS // SUBLANES)
            for k in range(CONV_K):
                q, r = divmod(base + k, SUBLANES)
                src = ext_ref if r == 0 else sh_ref.at[r - 1]
                for j in range(len(accs)):
                    accs[j] = accs[j] + src[pl.ds(r0 + (q + j) * SUBLANES, SUBLANES), cols] * taps[k]
            for j, acc in enumerate(accs):
                acc_ref[pl.ds(r0 + j * SUBLANES, SUBLANES), cols] = acc
            return carry

        lax.fori_loop(0, tl // CONV_ROWS, rows, 0, unroll=2)
    x = acc_ref[...]
    mu = jnp.mean(x, axis=-1, keepdims=True)
    xc = x - mu
    var = jnp.mean(xc * xc, axis=-1, keepdims=True)
    y = xc * lax.rsqrt(var + EPS) * lg_ref[...] + lb_ref[...]
    o_ref[0] = (_silu(y) * sga_ref[0].astype(F32)).astype(BF16)


def conv_branch(a, sga, conv_w, conv_b, ln_g, ln_b, tl):
    bsz, length, width = a.shape
    hb = tl // CONV_HALO
    nh = length // CONV_HALO
    row = pl.BlockSpec((1, width), lambda b, i: (0, 0))
    main = pl.BlockSpec((1, tl, width), lambda b, i: (b, i, 0))
    return pl.pallas_call(
        _conv_kernel,
        name="conv_branch",
        grid=(bsz, length // tl),
        in_specs=[
            main,
            pl.BlockSpec((1, CONV_HALO, width), lambda b, i: (b, jnp.maximum(i * hb - 1, 0), 0)),
            pl.BlockSpec((1, CONV_HALO, width), lambda b, i: (b, jnp.minimum((i + 1) * hb, nh - 1), 0)),
            pl.BlockSpec((CONV_K, width), lambda b, i: (0, 0)),
            row, row, row,
            main,
        ],
        out_specs=main,
        out_shape=jax.ShapeDtypeStruct((bsz, length, width), BF16),
        scratch_shapes=[pltpu.VMEM((tl + 2 * CONV_HALO, width), F32),
                        pltpu.VMEM((SUBLANES - 1, tl + 2 * CONV_HALO - SUBLANES, width), F32),
                        pltpu.VMEM((tl, width), F32)],
        compiler_params=_params(("parallel", "parallel")),
    )(a, a, a, conv_w, conv_b.reshape(1, width), ln_g.reshape(1, width), ln_b.reshape(1, width), sga)


DFT_SPLIT = 64


def _dft_gen_kernel(ca_ref, sa_ref, cb_ref, sb_ref, c_ref, s_ref):
    ca, sa = ca_ref[0], sa_ref[0]
    cb, sb = cb_ref[...], sb_ref[...]
    c_ref[...] = (ca * cb - sa * sb).astype(BF16)
    s_ref[...] = (-(sa * cb + ca * sb)).astype(BF16)


def dft_matrices(length):
    na = length // DFT_SPLIT
    k = np.arange(length)[None, :]
    w = 2.0 * math.pi / length
    ang_a = ((DFT_SPLIT * np.arange(na)[:, None] * k) % length) * w
    ang_b = ((np.arange(DFT_SPLIT)[:, None] * k) % length) * w
    ca, sa = (jnp.asarray(f(ang_a).reshape(na, 1, length), F32) for f in (np.cos, np.sin))
    cb, sb = (jnp.asarray(f(ang_b), F32) for f in (np.cos, np.sin))
    tab_a = pl.BlockSpec((1, 1, length), lambda a: (a, 0, 0))
    tab_b = pl.BlockSpec((DFT_SPLIT, length), lambda a: (0, 0))
    out = pl.BlockSpec((DFT_SPLIT, length), lambda a: (a, 0))
    shp = jax.ShapeDtypeStruct((length, length), BF16)
    return pl.pallas_call(
        _dft_gen_kernel,
        name="dft_matrices",
        grid=(na,),
        in_specs=[tab_a, tab_a, tab_b, tab_b],
        out_specs=[out, out],
        out_shape=[shp, shp],
        compiler_params=_params(("parallel",)),
    )(ca, sa, cb, sb)


def _dft_kernel(c_ref, s_ref, p_ref, q_ref, sgb_ref, o_ref, acc_ref, *, scale):
    kk = pl.program_id(2)

    @pl.when(kk == 0)
    def _():
        acc_ref[...] = jnp.zeros_like(acc_ref)

    acc_ref[...] += _dot(c_ref[...], p_ref[0]) + _dot(s_ref[...], q_ref[0])

    @pl.when(kk == pl.num_programs(2) - 1)
    def _():
        o_ref[0] = (acc_ref[...] * scale * sgb_ref[0].astype(F32)).astype(BF16)


def fourier_branch(cmat, smat, p, q, sgb, scale, tm, tk):
    bsz, length, width = p.shape
    return pl.pallas_call(
        functools.partial(_dft_kernel, scale=scale),
        name="fourier_branch",
        grid=(bsz, length // tm, length // tk),
        in_specs=[
            pl.BlockSpec((tm, tk), lambda b, m, k: (m, k)),
            pl.BlockSpec((tm, tk), lambda b, m, k: (m, k)),
            pl.BlockSpec((1, tk, width), lambda b, m, k: (b, k, 0)),
            pl.BlockSpec((1, tk, width), lambda b, m, k: (b, k, 0)),
            pl.BlockSpec((1, tm, width), lambda b, m, k: (b, m, 0)),
        ],
        out_specs=pl.BlockSpec((1, tm, width), lambda b, m, k: (b, m, 0)),
        out_shape=jax.ShapeDtypeStruct((bsz, length, width), BF16),
        scratch_shapes=[pltpu.VMEM((tm, width), F32)],
        compiler_params=_params(("parallel", "parallel", "arbitrary")),
    )(cmat, smat, p, q, sgb)


FFT_MINOR = 64
FFT_COLS = 256


def _fft_kernel(p_ref, q_ref, k1a_ref, k1b_ref, twr_ref, twi_ref, hre_ref, him_ref, sgb_ref, o_ref,
                x_ref, y_ref, z_ref, *, scale):
    length, cw = p_ref.shape[1], p_ref.shape[2]
    ncb = cw // LANES
    n_m = length // FFT_MINOR
    n_a1 = FFT_MINOR // SUBLANES
    n_kb = n_m // SUBLANES
    slab = n_m * SUBLANES
    piece = SUBLANES * SUBLANES
    for cb in range(ncb):
        cols = pl.ds(cb * LANES, LANES)
        x_ref[cb] = p_ref[0, :, cols].astype(F32).reshape(n_m, FFT_MINOR, LANES)
        x_ref[ncb + cb] = q_ref[0, :, cols].astype(F32).reshape(n_m, FFT_MINOR, LANES)

    def slab_of(part, a1):
        tiles = [x_ref[part * ncb + cb, :, pl.ds(a1 * SUBLANES, SUBLANES), :].reshape(slab, LANES)
                 for cb in range(ncb)]
        return jnp.concatenate(tiles, axis=1).astype(BF16)

    for a1 in range(n_a1):
        y = _dot(k1a_ref[...], slab_of(0, a1)) + _dot(k1b_ref[...], slab_of(1, a1))
        tr, ti = twr_ref[a1], twi_ref[a1]
        for cb in range(ncb):
            yre = y[:slab, cb * LANES:(cb + 1) * LANES]
            yim = y[slab:, cb * LANES:(cb + 1) * LANES]
            zre = (tr * yre - ti * yim).astype(BF16)
            zim = (tr * yim + ti * yre).astype(BF16)
            for kb in range(n_kb):
                dst = (pl.ds(a1 * piece, piece), pl.ds(cb * LANES, LANES))
                y_ref[(kb, 0) + dst] = zre[kb * piece:(kb + 1) * piece]
                y_ref[(kb, 1) + dst] = zim[kb * piece:(kb + 1) * piece]

    for kb in range(n_kb):
        out = _dot(hre_ref[...], y_ref[kb, 0]) + _dot(him_ref[...], y_ref[kb, 1])
        z_ref[:, kb] = out.reshape(FFT_MINOR, SUBLANES, cw)

    o_ref[0] = (z_ref[...].reshape(length, cw) * scale * sgb_ref[0].astype(F32)).astype(BF16)


def _fft_tables(length):
    n_a, s = FFT_MINOR, SUBLANES
    n_m = length // n_a
    n_a1 = n_a // s
    eye = np.eye(s)
    ang_m = (np.outer(np.arange(n_m), np.arange(n_m)) % n_m) * (2.0 * math.pi / n_m)
    cm, sm = np.kron(np.cos(ang_m), eye), np.kron(np.sin(ang_m), eye)
    k1a = np.concatenate([cm, -sm], axis=0)
    k1b = np.concatenate([-sm, -cm], axis=0)
    a_full = s * np.arange(n_a1)[:, None, None] + np.arange(s)[None, None, :]
    ang_t = ((np.arange(n_m)[None, :, None] * a_full) % length) * (2.0 * math.pi / length)
    rep = lambda t: np.broadcast_to(t.reshape(n_a1, n_m * s, 1), (n_a1, n_m * s, LANES))
    twr, twi = rep(np.cos(ang_t)), rep(-np.sin(ang_t))
    ang_a = ((np.arange(n_a)[:, None, None] * a_full.reshape(1, n_a1, s)) % n_a) * (2.0 * math.pi / n_a)
    spread = lambda t: np.einsum("kas,pq->kpaqs", t, eye).reshape(n_a * s, n_a1 * s * s)
    hre, him = spread(np.cos(ang_a)), spread(np.sin(ang_a))
    return ([jnp.asarray(t, BF16) for t in (k1a, k1b)] + [jnp.asarray(t, F32) for t in (twr, twi)]
            + [jnp.asarray(t, BF16) for t in (hre, him)])


def fourier_branch_fft(p, q, sgb, scale):
    bsz, length, width = p.shape
    tables = _fft_tables(length)
    n_m = length // FFT_MINOR
    tok = pl.BlockSpec((1, length, FFT_COLS), lambda b, n: (b, 0, n))
    const = lambda t: pl.BlockSpec(t.shape, lambda b, n, nd=t.ndim: (0,) * nd)
    return pl.pallas_call(
        functools.partial(_fft_kernel, scale=scale),
        name="fourier_branch_fft",
        grid=(bsz, width // FFT_COLS),
        in_specs=[tok, tok] + [const(t) for t in tables] + [tok],
        out_specs=tok,
        out_shape=jax.ShapeDtypeStruct((bsz, length, width), BF16),
        scratch_shapes=[pltpu.VMEM((2 * FFT_COLS // LANES, n_m, FFT_MINOR, LANES), F32),
                        pltpu.VMEM((n_m // SUBLANES, 2, FFT_MINOR * SUBLANES, FFT_COLS), BF16),
                        pltpu.VMEM((FFT_MINOR, n_m // SUBLANES, SUBLANES, FFT_COLS), F32)],
        compiler_params=_params(("parallel", "parallel")),
    )(p, q, *tables, sgb)


def _postnorm_residual(y, h_ref, pg_ref, gt_ref, o_ref):
    o_ref[0] = h_ref[0] + gt_ref[0] * (_rms(y) * pg_ref[...])


def _even_out_kernel(a_ref, b_ref, wa_ref, wb_ref, h_ref, pg_ref, gt_ref, o_ref):
    y = _dot(a_ref[0], wa_ref[...]) + _dot(b_ref[0], wb_ref[...])
    _postnorm_residual(y, h_ref, pg_ref, gt_ref, o_ref)


def even_out_proj(a, b, wa, wb, h, post_g, gate, tm):
    bsz, length, d = h.shape
    width = a.shape[2]
    half = pl.BlockSpec((1, tm, width), lambda bb, i: (bb, i, 0))
    wspec = pl.BlockSpec((width, d), lambda bb, i: (0, 0))
    full = pl.BlockSpec((1, tm, d), lambda bb, i: (bb, i, 0))
    return pl.pallas_call(
        _even_out_kernel,
        name="even_out_proj",
        grid=(bsz, length // tm),
        in_specs=[half, half, wspec, wspec, full,
                  pl.BlockSpec((1, d), lambda bb, i: (0, 0)),
                  pl.BlockSpec((1, 1, d), lambda bb, i: (bb, 0, 0))],
        out_specs=full,
        out_shape=jax.ShapeDtypeStruct((bsz, length, d), F32),
        compiler_params=_params(("parallel", "parallel")),
    )(a, b, wa, wb, h, post_g.reshape(1, d), gate)


ODD_TN = 1024


ODD_PARTS = 2


def _odd_in_kernel(h_ref, g_ref, sh_ref, sc_ref, w_ref, o_ref, xn_ref, *, gate_parts):
    n = pl.program_id(2)

    @pl.when(n == 0)
    def _():
        _prenorm_modulate(h_ref, g_ref, sh_ref, sc_ref, xn_ref)

    xn = xn_ref[...]
    pw = ODD_TN // ODD_PARTS
    for j in range(ODD_PARTS):
        cols = pl.ds(j * pw, pw)
        acc = _dot(xn, w_ref[:, cols])
        tiles = [g // ODD_PARTS for g in gate_parts if g % ODD_PARTS == j]
        if tiles:
            is_gate = functools.reduce(jnp.logical_or, [n == t for t in tiles])
            acc = jnp.where(is_gate, _silu(acc), acc)
        o_ref[0, :, cols] = acc.astype(BF16)


def odd_in_proj(h, pre_g, shift, scale, w_all, gate_ranges, tm):
    bsz, length, d = h.shape
    n_all = w_all.shape[1]
    pw = ODD_TN // ODD_PARTS
    assert n_all % ODD_TN == 0 and all(s % pw == 0 and e % pw == 0 for s, e in gate_ranges)
    gate_parts = tuple(p for s, e in gate_ranges for p in range(s // pw, e // pw))
    vec = pl.BlockSpec((1, 1, d), lambda b, i, n: (b, 0, 0))
    return pl.pallas_call(
        functools.partial(_odd_in_kernel, gate_parts=gate_parts),
        name="odd_in_proj",
        grid=(bsz, length // tm, n_all // ODD_TN),
        in_specs=[pl.BlockSpec((1, tm, d), lambda b, i, n: (b, i, 0)),
                  pl.BlockSpec((1, d), lambda b, i, n: (0, 0)),
                  vec, vec,
                  pl.BlockSpec((d, ODD_TN), lambda b, i, n: (0, n))],
        out_specs=pl.BlockSpec((1, tm, ODD_TN), lambda b, i, n: (b, i, n)),
        out_shape=jax.ShapeDtypeStruct((bsz, length, n_all), BF16),
        scratch_shapes=[pltpu.VMEM((tm, d), BF16)],
        compiler_params=_params(("parallel", "parallel", "arbitrary")),
    )(h, pre_g.reshape(1, d), shift, scale, w_all)


def _na_pair_geometry(variant, rows):
    r0 = {0: NA_QROWS, 1: 0, 2: rows - NA_QROWS}[variant]
    ws = min(max(r0 - NA_ROWS // 2, 0), rows - NA_WIN_ROWS)
    return r0, ws


def _na_bias_kernel(rpb_ref, o_ref, *, rows):
    h = pl.program_id(0)
    n_dr, n_dc = 2 * NA_ROWS - 1, 2 * NA_COLS - 1
    qc = lax.broadcasted_iota(jnp.int32, (GRID_W, LANES), 0)
    lane = lax.broadcasted_iota(jnp.int32, (GRID_W, LANES), 1)
    kc = lane % GRID_W
    hi = lane // GRID_W
    diff = kc - qc
    c_start = jnp.clip(qc - NA_COLS // 2, 0, GRID_W - NA_COLS)
    col_ok = jnp.where(kc >= c_start, 1, 0) * jnp.where(kc < c_start + NA_COLS, 1, 0)
    blocks = []
    for dr in range(n_dr):
        val = jnp.full((GRID_W, LANES), NEG_INF, F32)
        for dc in range(n_dc):
            val = jnp.where(diff == dc - (NA_COLS - 1), rpb_ref[(h * n_dr + dr) * n_dc + dc], val)
        blocks.append(jnp.where(col_ok == 1, val, NEG_INF))
    masked = jnp.full((GRID_W, LANES), NEG_INF, F32)
    for variant in range(NA_VARIANTS):
        r0, ws = _na_pair_geometry(variant, rows)
        for ri in range(NA_QROWS):
            r = r0 + ri
            r_start = min(max(r - NA_ROWS // 2, 0), rows - NA_ROWS)
            for wp in range(NA_WIN_ROWS // 2):
                halves = []
                for a in (ws + 2 * wp, ws + 2 * wp + 1):
                    in_window = r_start <= a < r_start + NA_ROWS
                    halves.append(blocks[a - r + NA_ROWS - 1] if in_window else masked)
                o_ref[0, variant, pl.ds(ri * GRID_W, GRID_W), pl.ds(wp * LANES, LANES)] = jnp.where(
                    hi == 1, halves[1], halves[0])


def na_bias_table(rpb, rows):
    heads = rpb.shape[0]
    return pl.pallas_call(
        functools.partial(_na_bias_kernel, rows=rows),
        name="na_bias_table",
        grid=(heads,),
        in_specs=[pl.BlockSpec(memory_space=pltpu.SMEM)],
        out_specs=pl.BlockSpec((1, NA_VARIANTS, NA_PAIR, NA_WIN), lambda h: (h, 0, 0, 0)),
        out_shape=jax.ShapeDtypeStruct((heads, NA_VARIANTS, NA_PAIR, NA_WIN), F32),
        compiler_params=_params(("parallel",)),
    )(rpb.reshape(-1))


def _na_kernel(q_ref, k_ref, v_ref, kc_ref, vc_ref, bias_ref, sg_ref, o_ref,
               vx_ref, vcx_ref, s0_ref, s1_ref, p0_ref, p1_ref, *, rows):
    npairs = rows // NA_QROWS
    last = npairs - 1
    n_heads = q_ref.shape[2] // HEAD_DIM
    nt = (((1,), (1,)), ((), ()))
    head = lambda hh: pl.ds(hh * HEAD_DIM, HEAD_DIM)
    for hh in range(n_heads):
        vx_ref[hh, :, :HEAD_DIM] = v_ref[0, :, head(hh)]
        vx_ref[hh, :, HEAD_DIM:] = jnp.ones((vx_ref.shape[1], HEAD_DIM), BF16)
        vcx_ref[hh, :, :HEAD_DIM] = vc_ref[0, :, head(hh)]
        vcx_ref[hh, :, HEAD_DIM:] = jnp.ones((vcx_ref.shape[1], HEAD_DIM), BF16)

    def window(pr):
        ws = min(max(NA_QROWS * pr - NA_ROWS // 2, 0), rows - NA_WIN_ROWS)
        return pl.ds(ws * GRID_W, NA_WIN)

    def qrows(pr):
        return pl.ds(pr * NA_PAIR, NA_PAIR)

    def scores(unit, s_ref):
        hh, pr = divmod(unit, npairs)
        variant = {0: 1, last: 2}.get(pr, 0)
        q = q_ref[0, qrows(pr), head(hh)]
        s_ref[:, :NA_WIN] = lax.dot_general(q, k_ref[0, window(pr), head(hh)], nt,
                                            preferred_element_type=F32) + bias_ref[hh, variant]
        s_ref[:, NA_WIN:] = lax.dot_general(q, kc_ref[0, :, head(hh)], nt, preferred_element_type=F32)

    def probs(s_ref, p_ref):
        s = s_ref[...]
        p_ref[...] = jnp.exp(s - jnp.max(s, axis=-1, keepdims=True)).astype(BF16)

    def values(unit, p_ref):
        hh, pr = divmod(unit, npairs)
        acc = _dot(p_ref[:, :NA_WIN], vx_ref[hh, window(pr), :]) + _dot(p_ref[:, NA_WIN:], vcx_ref[hh])
        o = acc[:, :HEAD_DIM] / acc[:, HEAD_DIM:]
        o_ref[0, qrows(pr), head(hh)] = (o * sg_ref[0, qrows(pr), head(hh)].astype(F32)).astype(BF16)

    units = n_heads * npairs
    bufs = ((s0_ref, p0_ref), (s1_ref, p1_ref))
    scores(0, s0_ref)
    scores(1, s1_ref)
    probs(s0_ref, p0_ref)
    for i in range(units):
        s_cur, p_cur = bufs[i % 2]
        s_nxt, p_nxt = bufs[(i + 1) % 2]
        values(i, p_cur)
        if i + 1 < units:
            probs(s_nxt, p_nxt)
        if i + 2 < units:
            scores(i + 2, s_cur)


def neighbourhood_attention(proj, proj_c, bias, na_w, gate_col):
    bsz, length, _ = proj.shape
    lc = proj_c.shape[1]
    heads = na_w // HEAD_DIM
    rows = length // GRID_W
    nh = NA_HEADS_PER_STEP
    assert heads % nh == 0 and gate_col % (nh * HEAD_DIM) == 0
    groups = heads // nh
    seq = lambda off: pl.BlockSpec((1, length, nh * HEAD_DIM), lambda h, b: (b, 0, off + h))
    cseq = lambda off: pl.BlockSpec((1, lc, nh * HEAD_DIM), lambda h, b: (b, 0, off + h))
    return pl.pallas_call(
        functools.partial(_na_kernel, rows=rows),
        name="neighbourhood_attention",
        grid=(groups, bsz),
        in_specs=[seq(0), seq(groups), seq(2 * groups), cseq(groups), cseq(2 * groups),
                  pl.BlockSpec((nh, NA_VARIANTS, NA_PAIR, NA_WIN), lambda h, b: (h, 0, 0, 0)),
                  seq(gate_col // (nh * HEAD_DIM))],
        out_specs=seq(0),
        out_shape=jax.ShapeDtypeStruct((bsz, length, na_w), BF16),
        scratch_shapes=[pltpu.VMEM((nh, length, 2 * HEAD_DIM), BF16), pltpu.VMEM((nh, lc, 2 * HEAD_DIM), BF16),
                        pltpu.VMEM((NA_PAIR, NA_WIN + lc), F32), pltpu.VMEM((NA_PAIR, NA_WIN + lc), F32),
                        pltpu.VMEM((NA_PAIR, NA_WIN + lc), BF16), pltpu.VMEM((NA_PAIR, NA_WIN + lc), BF16)],
        compiler_params=_params(("parallel", "parallel")),
    )(proj, proj, proj, proj_c, proj_c, bias, proj)


S5_CHUNK = 128
S5_IN_BLOCK = LANES
S5_OUT_BLOCK = 256
S5_SCAN_COLS = 1024


def _s5_kernel(ufc_ref, ufx_ref, ubc_ref, ubx_ref, bf_ref, bb_ref, crf_ref, cif_ref, crb_ref, cib_ref,
               lre_ref, lim_ref, yf_ref, yb_ref, buf_ref, state_ref, stage_ref, *, ctx_chunks):
    bsz, t_len, width = ufx_ref.shape
    n_state = lre_ref.shape[1]
    in_ctx = pl.program_id(0) < ctx_chunks
    n_in = width // S5_IN_BLOCK
    blk = n_state // n_in
    rows8 = 2 * bsz

    @pl.when(pl.program_id(0) == 0)
    def _():
        state_ref[...] = jnp.zeros_like(state_ref)

    nlb = n_state // LANES
    per_in = blk // LANES
    ti = lax.broadcasted_iota(jnp.int32, (t_len, t_len), 0)
    tj = lax.broadcasted_iota(jnp.int32, (t_len, t_len), 1)
    rev = jnp.where(ti + tj == t_len - 1, 1.0, 0.0).astype(BF16)
    u_fwd = jnp.where(in_ctx, ufc_ref[...], ufx_ref[...]).reshape(bsz * t_len, width)
    u_nat = jnp.where(in_ctx, ubc_ref[...], ubx_ref[...])
    u_bwd = jnp.concatenate([_dot(rev, u_nat[b]).astype(BF16) for b in range(bsz)], axis=0)
    sides = ((u_fwd, bf_ref, yf_ref, crf_ref, cif_ref), (u_bwd, bb_ref, yb_ref, crb_ref, cib_ref))

    def project(kb):
        for di, (u, w_ref, _, _, _) in enumerate(sides):
            res = _dot(u[:, kb * S5_IN_BLOCK:(kb + 1) * S5_IN_BLOCK], w_ref[kb])
            for b in range(bsz):
                rsel = pl.ds(di * bsz + b, t_len, stride=rows8)
                for c in range(2 * per_in):
                    dst = (c // per_in) * nlb + kb * per_in + c % per_in
                    buf_ref[dst, rsel, :] = res[b * t_len:(b + 1) * t_len, c * LANES:(c + 1) * LANES]

    per = S5_SCAN_COLS // LANES

    def scan(group, t0, t1, carry):
        blocks = [group * per + i for i in range(per)]
        lre = [lre_ref[:, pl.ds(c * LANES, LANES)] for c in blocks]
        lim = [lim_ref[:, pl.ds(c * LANES, LANES)] for c in blocks]
        if carry is None:
            carry = [state_ref[:, pl.ds(part * n_state + c * LANES, LANES)] for c in blocks for part in range(2)]
        for t in range(t0, t1):
            row = pl.ds(t * rows8, rows8)
            for i, c in enumerate(blocks):
                hre, him = carry[2 * i], carry[2 * i + 1]
                carry[2 * i] = lre[i] * hre - lim[i] * him + buf_ref[c, row, :]
                carry[2 * i + 1] = lre[i] * him + lim[i] * hre + buf_ref[nlb + c, row, :]
                buf_ref[c, row, :] = carry[2 * i]
                buf_ref[nlb + c, row, :] = carry[2 * i + 1]
        if t1 == t_len:
            for i, c in enumerate(blocks):
                state_ref[:, pl.ds(c * LANES, LANES)] = carry[2 * i]
                state_ref[:, pl.ds(n_state + c * LANES, LANES)] = carry[2 * i + 1]
        return carry

    n_out = width // S5_OUT_BLOCK
    per_out = nlb // n_out

    def readout(di, j):
        _, _, y_ref, cr_ref, ci_ref = sides[di]
        def gathered(part):
            return jnp.concatenate([
                jnp.concatenate([buf_ref[part * nlb + j * per_out + c, pl.ds(di * bsz + b, t_len, stride=rows8), :]
                                 for c in range(per_out)], axis=1)
                for b in range(bsz)], axis=0).astype(BF16)

        y = _dot(gathered(0), cr_ref[j]) + _dot(gathered(1), ci_ref[j])
        for b in range(bsz):
            yb = y[b * t_len:(b + 1) * t_len]
            if di == 1:
                hi = yb.astype(BF16)
                r1 = yb - hi.astype(F32)
                mid = r1.astype(BF16)
                lo = (r1 - mid.astype(F32)).astype(BF16)
                yb = _dot(rev, hi) + _dot(rev, mid) + _dot(rev, lo)
            y_ref[b, :, pl.ds(j * S5_OUT_BLOCK, S5_OUT_BLOCK)] = yb

    assert n_in == 4 and nlb // per == 2 and n_out == 2 and per_out == per
    half = t_len // 2
    project(0)
    project(1)
    carry = scan(0, 0, half, None)
    project(2)
    scan(0, half, t_len, carry)
    project(3)
    carry = scan(1, 0, half, None)
    readout(0, 0)
    scan(1, half, t_len, carry)
    readout(1, 0)
    readout(0, 1)
    readout(1, 1)


def _block_diag(x):
    nblk, g, r, c = x.shape
    eye = jnp.eye(g, dtype=x.dtype)
    return jnp.einsum("kgrc,gh->kgrhc", x, eye).reshape(nblk, g * r, g * c)


def _s5_direction_params(a_re, a_im, log_dt, b_re, b_im, c_re, c_im):
    groups, n_p = a_re.shape
    lam = lax.complex(a_re.astype(F32), a_im.astype(F32))
    dt = jnp.exp(log_dt.astype(F32))[:, None]
    lam_bar = jnp.exp(lam * dt)
    b_bar = ((lam_bar - 1.0) / lam)[..., None] * lax.complex(b_re.astype(F32), b_im.astype(F32))
    gi = S5_IN_BLOCK // SSM_GROUP
    bt = jnp.swapaxes(b_bar, 1, 2).reshape(groups // gi, gi, SSM_GROUP, n_p)
    b_mat = jnp.concatenate([_block_diag(jnp.real(bt)), _block_diag(jnp.imag(bt))], axis=-1)
    go = S5_OUT_BLOCK // SSM_GROUP
    ct = lambda c: jnp.swapaxes(c.astype(F32), 1, 2).reshape(groups // go, go, n_p, SSM_GROUP)
    return (b_mat.astype(BF16), _block_diag(ct(c_re)).astype(BF16), _block_diag(-ct(c_im)).astype(BF16),
            jnp.real(lam_bar).reshape(-1), jnp.imag(lam_bar).reshape(-1))


def s5_scan(proj_c, proj_x, d_col, width, fwd, bwd):
    bsz, lc, _ = proj_c.shape
    length = proj_x.shape[1]
    t_len = S5_CHUNK
    ncc, ncx = lc // t_len, length // t_len
    nc = ncc + ncx
    col = d_col // width
    n_state = fwd[3].shape[0]
    lre = jnp.concatenate([jnp.broadcast_to(fwd[3], (bsz, n_state)), jnp.broadcast_to(bwd[3], (bsz, n_state))])
    lim = jnp.concatenate([jnp.broadcast_to(fwd[4], (bsz, n_state)), jnp.broadcast_to(bwd[4], (bsz, n_state))])
    full = lambda x: pl.BlockSpec(x.shape, lambda c, nd=x.ndim: (0,) * nd)
    blk = lambda index: pl.BlockSpec((bsz, t_len, width), index)
    in_blocks = [blk(lambda c: (0, jnp.minimum(c, ncc - 1), col)),
                 blk(lambda c: (0, jnp.maximum(c - ncc, 0), col)),
                 blk(lambda c: (0, jnp.maximum(ncc - 1 - c, 0), col)),
                 blk(lambda c: (0, jnp.minimum(nc - 1 - c, ncx - 1), col))]
    out = jax.ShapeDtypeStruct((bsz, length, width), F32)
    consts = (fwd[0], bwd[0], fwd[1], fwd[2], bwd[1], bwd[2], lre, lim)
    return pl.pallas_call(
        functools.partial(_s5_kernel, ctx_chunks=ncc),
        name="s5_scan",
        grid=(nc,),
        in_specs=in_blocks + [full(x) for x in consts],
        out_specs=[blk(lambda c: (0, jnp.maximum(c - ncc, 0), 0)),
                   blk(lambda c: (0, jnp.minimum(nc - 1 - c, ncx - 1), 0))],
        out_shape=[out, out],
        scratch_shapes=[pltpu.VMEM((2 * n_state // LANES, t_len * 2 * bsz, LANES), F32),
                        pltpu.VMEM((2 * bsz, 2 * n_state), F32),
                        pltpu.VMEM((2 * (width // S5_OUT_BLOCK), 2, bsz * t_len,
                                    n_state // (width // S5_OUT_BLOCK)), BF16)],
        compiler_params=_params(("arbitrary",)),
    )(proj_c, proj_x, proj_c, proj_x, *consts)


def _odd_out_kernel(na_ref, yf_ref, yb_ref, d_ref, sdg_ref, dsk_ref, wglu_ref, wna_ref, wssm_ref,
                    h_ref, pg_ref, gt_ref, o_ref):
    y = yf_ref[0] + yb_ref[0] + dsk_ref[...] * d_ref[0].astype(F32)
    y = 0.5 * y * (1.0 + jnp.tanh(math.sqrt(2.0 / math.pi) * (y + 0.044715 * (y * y * y))))
    z = y * jax.nn.sigmoid(_dot(y.astype(BF16), wglu_ref[...]))
    s = (z * sdg_ref[0].astype(F32)).astype(BF16)
    out = _dot(na_ref[0], wna_ref[...]) + _dot(s, wssm_ref[...])
    _postnorm_residual(out, h_ref, pg_ref, gt_ref, o_ref)


def odd_out_proj(na, yf, yb, proj, d_col, dg_col, d_skip, w_glu, w_na, w_ssm, h, post_g, gate, tm):
    bsz, length, dm = h.shape
    na_w, ssm_w = na.shape[2], yf.shape[2]
    tok = lambda w, col=0: pl.BlockSpec((1, tm, w), lambda b, i: (b, i, col // w))
    const = lambda r, c: pl.BlockSpec((r, c), lambda b, i: (0, 0))
    return pl.pallas_call(
        _odd_out_kernel,
        name="odd_out_proj",
        grid=(bsz, length // tm),
        in_specs=[tok(na_w), tok(ssm_w), tok(ssm_w),
                  tok(ssm_w, d_col), tok(ssm_w, dg_col),
                  const(1, ssm_w), const(ssm_w, ssm_w), const(na_w, dm), const(ssm_w, dm),
                  tok(dm), const(1, dm),
                  pl.BlockSpec((1, 1, dm), lambda b, i: (b, 0, 0))],
        out_specs=tok(dm),
        out_shape=jax.ShapeDtypeStruct((bsz, length, dm), F32),
        compiler_params=_params(("parallel", "parallel")),
    )(na, yf, yb, proj, proj, d_skip.reshape(1, ssm_w), w_glu, w_na, w_ssm, h, post_g.reshape(1, dm), gate)


def _small_dft(n):
    ang = (np.outer(np.arange(n), np.arange(n)) % n) * (2.0 * math.pi / n)
    return jnp.asarray(np.concatenate([np.cos(ang), np.sin(ang)], axis=1), BF16)


def _conv_fourier_layer(h, pre_g, post_g, shift, scale, gate, w5, w_out_a, w_out_b, conv_w, conv_b,
                        ln_g, ln_b, fourier_g, cs_small, shared_mod):
    bsz, length, _ = h.shape
    gd = fourier_g.shape[1]
    fold = (lambda t: t.reshape(1, bsz * length, t.shape[2])) if shared_mod else (lambda t: t)
    unfold = (lambda t: t.reshape(bsz, length, t.shape[2])) if shared_mod else (lambda t: t)
    vec = (lambda v: v[:1]) if shared_mod else (lambda v: v)
    tokens = bsz * length if shared_mod else length
    outs = even_in_proj(fold(h), pre_g, vec(shift), vec(scale), w5, fourier_g, cs_small,
                        _token_tile(tokens, IN_PROJ_ROWS))
    a, sga, p, q, sgb = [unfold(t) for t in outs]
    a = conv_branch(a, sga, conv_w, conv_b, ln_g, ln_b, _token_tile(length, CONV_TILE_ROWS))
    ortho = 1.0 / math.sqrt(length * gd)
    if (length // FFT_MINOR) % SUBLANES == 0:
        f = fourier_branch_fft(p, q, sgb, ortho)
    else:
        cmat, smat = dft_matrices(length)
        f = fourier_branch(cmat, smat, p, q, sgb, ortho, length, length)
    return unfold(even_out_proj(fold(a), fold(f), w_out_a, w_out_b, fold(h), post_g, vec(gate),
                                _token_tile(tokens, OUT_PROJ_ROWS)))


def kernel(x, c, ctx, c_ctx, pre_g, post_g, ada_w, ada_b, ab_w_in, ab_w_out, conv_w, conv_b, conv_ln_g,
           conv_ln_b, fourier_g, cd_w_in, cd_w_out, na_rpb, s5_a_re, s5_a_im, s5_log_dt, s5_b_re, s5_b_im,
           s5_c_re, s5_c_im, s5_d, s5_w_glu):
    bsz, length, d = x.shape
    lc = ctx.shape[1]
    depth = ada_w.shape[0]
    assert depth == 2 and length % NA_PAIR == 0 and length // GRID_W >= NA_WIN_ROWS
    assert lc % S5_CHUNK == 0 and length % S5_CHUNK == 0

    cond_rows = jnp.zeros((SUBLANES, d), F32).at[:bsz].set(c).at[bsz].set(c_ctx)
    mods = ada_modulation(cond_rows, ada_w, ada_b)

    def mod_vectors(i):
        xs = [mods[i, :bsz, k * d:(k + 1) * d][:, None, :] for k in range(3)]
        cs = [jnp.broadcast_to(mods[i, bsz, k * d:(k + 1) * d], (bsz, 1, d)) for k in range(3)]
        return xs, cs

    (sh_x, sc_x, gt_x), (sh_c, sc_c, gt_c) = mod_vectors(0)
    w_in = ab_w_in[0]
    conv_width = conv_w.shape[2]
    cw = conv_width
    fw = (w_in.shape[1] - 3 * cw) // 2
    assert cw == fw
    w5 = w_in.astype(BF16)
    w_out_a = ab_w_out[0, :cw].astype(BF16)
    w_out_b = ab_w_out[0, cw:].astype(BF16)
    cs_small = _small_dft(fourier_g.shape[2])
    layer0 = functools.partial(
        _conv_fourier_layer, pre_g=pre_g[0], post_g=post_g[0], w5=w5, w_out_a=w_out_a, w_out_b=w_out_b,
        conv_w=conv_w[0], conv_b=conv_b[0], ln_g=conv_ln_g[0], ln_b=conv_ln_b[0], fourier_g=fourier_g[0],
        cs_small=cs_small)
    h_x = layer0(x, shift=sh_x, scale=sc_x, gate=gt_x, shared_mod=False)
    h_c = layer0(ctx, shift=sh_c, scale=sc_c, gate=gt_c, shared_mod=True)

    (sh_x, sc_x, gt_x), (sh_c, sc_c, _) = mod_vectors(1)
    w_in = cd_w_in[0]
    ssm_w = s5_d.shape[1]
    na_w = (w_in.shape[1] - 2 * ssm_w) // 4
    g_col, d_col, dg_col = 3 * na_w, 4 * na_w, 4 * na_w + ssm_w
    gates = ((g_col, d_col), (dg_col, dg_col + ssm_w))
    col_scale = jnp.where(jnp.arange(w_in.shape[1]) < na_w, HEAD_DIM ** -0.5, 1.0).astype(F32)
    w_all = (w_in * col_scale).astype(BF16)
    proj_x = odd_in_proj(h_x, pre_g[1], sh_x, sc_x, w_all, gates, _token_tile(length, IN_PROJ_ROWS))
    proj_c = odd_in_proj(h_c.reshape(1, bsz * lc, d), pre_g[1], sh_c[:1], sc_c[:1], w_all, gates,
                         _token_tile(bsz * lc, IN_PROJ_ROWS)).reshape(bsz, lc, -1)

    bias = na_bias_table(na_rpb[0], length // GRID_W)
    na = neighbourhood_attention(proj_x, proj_c, bias, na_w, g_col)

    dirs = [_s5_direction_params(s5_a_re[0, i], s5_a_im[0, i], s5_log_dt[0, i], s5_b_re[0, i], s5_b_im[0, i],
                                 s5_c_re[0, i], s5_c_im[0, i]) for i in range(2)]
    y_f, y_b = s5_scan(proj_c, proj_x, d_col, ssm_w, *dirs)

    return odd_out_proj(na, y_f, y_b, proj_x, d_col, dg_col, s5_d[0], s5_w_glu[0].astype(BF16),
                        cd_w_out[0, :na_w].astype(BF16), cd_w_out[0, na_w:].astype(BF16),
                        h_x, post_g[1], gt_x, _token_tile(length, OUT_PROJ_ROWS))
```
